```python
import math, functools
import jax, jax.numpy as jnp
from jax import lax
import numpy as np

D_MODEL = 2048
BATCH = 2
SEQ = 8192
DEPTH = 1
DEC_BATCH = 32
DEC_SEQ = 16
PAST_LEN = 2048

CHUNK = 64
Q_BLOCK = 128
EPS = 1e-6
MLA_HEADS = 8
QK_NOPE = 128
QK_ROPE = 64
V_HEAD = 128
Q_RANK = 512
KV_RANK = 512
ROPE_BASE = 10000.0
ATTN_SCALE = (QK_NOPE + QK_ROPE) ** -0.5
HG_HEADS = 8
HG_EXPAND = 128
HG_VDIM = 128
HG_FK = HG_HEADS * HG_EXPAND
HG_V = HG_HEADS * HG_VDIM
MIX_WIDTH = MLA_HEADS * V_HEAD + HG_V
D_FF = 4 * D_MODEL
N_IN = Q_RANK + KV_RANK + QK_ROPE + 2 * HG_FK + 2 * HG_V
IN_SPLITS = [Q_RANK, Q_RANK + KV_RANK, Q_RANK + KV_RANK + QK_ROPE,
             Q_RANK + KV_RANK + QK_ROPE + HG_FK, Q_RANK + KV_RANK + QK_ROPE + 2 * HG_FK,
             Q_RANK + KV_RANK + QK_ROPE + 2 * HG_FK + HG_V]

kernel_name = 'chunk_causal_mla_hgrn2_adaln_encoder_step'


def rms_norm(x, gain=None):
    xf = x.astype(jnp.float32)
    y = xf * lax.rsqrt(jnp.mean(xf * xf, axis=-1, keepdims=True) + EPS)
    if gain is not None:
        y = y * gain.astype(jnp.float32)
    return y.astype(x.dtype)


def rope_tables(pos):
    half = QK_ROPE // 2
    inv_freq = jnp.exp(-math.log(ROPE_BASE) * jnp.arange(half, dtype=jnp.float32) / half)
    ang = pos.astype(jnp.float32)[:, None] * inv_freq[None, :]
    return jnp.cos(ang), jnp.sin(ang)


def apply_rope(x, cos, sin):
    xf = x.astype(jnp.float32)
    x1, x2 = xf[..., :QK_ROPE // 2], xf[..., QK_ROPE // 2:]
    return jnp.concatenate([x1 * cos - x2 * sin, x2 * cos + x1 * sin], axis=-1).astype(x.dtype)


def mla_prompt_attn(q_nope, q_pe, ckv, kpe, w_uk, w_uv):
    B, S, H, _ = q_nope.shape
    k_nope = jnp.einsum('bsc,chd->bshd', ckv, w_uk)
    v = jnp.einsum('bsc,chd->bshd', ckv, w_uv)
    nb = S // Q_BLOCK
    key_chunk = jnp.arange(S) // CHUNK

    def block(args):
        qn, qp, i = args
        s = (jnp.einsum('bqhd,bkhd->bhqk', qn, k_nope)
             + jnp.einsum('bqhr,bkr->bhqk', qp, kpe)).astype(jnp.float32) * ATTN_SCALE
        q_chunk = (i * Q_BLOCK + jnp.arange(Q_BLOCK)) // CHUNK
        s = jnp.where(key_chunk[None, :] <= q_chunk[:, None], s, -jnp.inf)
        p = jax.nn.softmax(s, axis=-1).astype(v.dtype)
        return jnp.einsum('bhqk,bkhd->bqhd', p, v)

    to_blocks = lambda a: jnp.moveaxis(a.reshape(B, nb, Q_BLOCK, *a.shape[2:]), 1, 0)
    o = lax.map(block, (to_blocks(q_nope), to_blocks(q_pe), jnp.arange(nb)))
    return jnp.moveaxis(o, 0, 1).reshape(B, S, H * V_HEAD)


def mla_sample_attn(past_ckv, past_kpe, q_nope, q_pe, ckv, kpe, w_uk, w_uv):
    B, L, H, _ = q_nope.shape
    ckv_all = jnp.concatenate([past_ckv.astype(ckv.dtype), ckv], axis=1)
    kpe_all = jnp.concatenate([past_kpe.astype(kpe.dtype), kpe], axis=1)
    q_lat = jnp.einsum('bqhd,chd->bqhc', q_nope, w_uk)
    s = (jnp.einsum('bqhc,bkc->bhqk', q_lat, ckv_all)
         + jnp.einsum('bqhr,bkr->bhqk', q_pe, kpe_all)).astype(jnp.float32) * ATTN_SCALE
    p = jax.nn.softmax(s, axis=-1).astype(ckv_all.dtype)
    ctx = jnp.einsum('bhqk,bkc->bqhc', p, ckv_all)
    return jnp.einsum('bqhc,chd->bqhd', ctx, w_uv).reshape(B, L, H * V_HEAD)


def gla_chunk(S, q, k, v, g):
    L = q.shape[1]
    b = jnp.cumsum(g, axis=1)
    o = jnp.einsum('blhk,bhkv->blhv', q * jnp.exp(b), S)
    causal = jnp.tril(jnp.ones((L, L), dtype=bool))[None, :, :, None, None]
    decay = jnp.exp(jnp.where(causal, b[:, :, None] - b[:, None, :], -jnp.inf))
    a = jnp.einsum('bthk,bshk,btshk->bhts', q, k, decay)
    o = o + jnp.einsum('bhts,bshv->bthv', a, v)
    b_last = b[:, -1]
    S_new = (jnp.exp(b_last)[..., None] * S
             + jnp.einsum('bshk,bshv->bhkv', k * jnp.exp(b_last[:, None] - b), v))
    return S_new, o


def hgrn_prompt(q, k, v, g):
    B, S, H, K = q.shape
    n = S // CHUNK
    to_chunks = lambda a: jnp.moveaxis(a.reshape(B, n, CHUNK, *a.shape[2:]), 1, 0)
    S0 = jnp.zeros((B, H, K, HG_VDIM), jnp.float32)
    S_fin, o = lax.scan(lambda s, xs: gla_chunk(s, *xs), S0,
                        (to_chunks(q), to_chunks(k), to_chunks(v), to_chunks(g)))
    return jnp.moveaxis(o, 0, 1).reshape(B, S, H, HG_VDIM), S_fin


def hgrn_sample(state0, q, k, v, g):
    S_new, o = gla_chunk(state0.astype(jnp.float32), q, k, v, g)
    return o, S_new


def trunk_layer(x, c, cos, sin, lb, lw, attend, recur):
    (w_ada, b_ada, w_in, w_uq, g_q, g_kv, w_uk, w_uv, g_hgrn, w_out, w_ff1, w_ff2) = lw
    B, L, _ = x.shape
    mod = (jax.nn.silu(c) @ w_ada + b_ada)[:, None, :]
    sh1, sc1, gt1, sh2, sc2, gt2 = jnp.split(mod, 6, axis=-1)
    h = rms_norm(x) * (1 + sc1) + sh1
    cq, ckv, kpe, hq, hf, hi, hg = jnp.split(h @ w_in, IN_SPLITS, axis=-1)
    q = (rms_norm(cq, g_q) @ w_uq).reshape(B, L, MLA_HEADS, QK_NOPE + QK_ROPE)
    q_nope = q[..., :QK_NOPE]
    q_pe = apply_rope(q[..., QK_NOPE:], cos[:, None], sin[:, None])
    ckv = rms_norm(ckv, g_kv)
    kpe = apply_rope(kpe, cos, sin)
    o_mla = attend(q_nope, q_pe, ckv, kpe, w_uk, w_uv)
    f = lb + (1 - lb) * jax.nn.sigmoid(hf.astype(jnp.float32))
    heads = lambda a, d: a.reshape(B, L, HG_HEADS, d)
    q_h = heads(jax.nn.silu(hq.astype(jnp.float32)) * HG_EXPAND ** -0.5, HG_EXPAND)
    o_hg, s_fin = recur(q_h, heads(1 - f, HG_EXPAND), heads(hi.astype(jnp.float32), HG_VDIM),
                        heads(jnp.log(f), HG_EXPAND))
    o_hg = rms_norm(o_hg.astype(x.dtype), g_hgrn) * jax.nn.silu(heads(hg, HG_VDIM))
    mix = jnp.concatenate([o_mla, o_hg.reshape(B, L, HG_V)], axis=-1) @ w_out
    x = x + gt1 * mix
    h = rms_norm(x) * (1 + sc2) + sh2
    x = x + gt2 * (jnp.square(jax.nn.relu(h @ w_ff1)) @ w_ff2)
    return x, ckv, kpe, s_fin.astype(x.dtype)


def setup_inputs(seed: int = 0) -> dict:
    key = jax.random.key(seed)
    ks = jax.random.split(key, 24)
    nrm = lambda k, shape, s: jax.random.normal(k, shape, jnp.float32) * s
    return {
        'x_prompt': nrm(ks[0], (BATCH, SEQ, D_MODEL), 1.0),
        'x_sample': nrm(ks[1], (DEC_BATCH, DEC_SEQ, D_MODEL), 1.0),
        'c_prompt': nrm(ks[2], (BATCH, D_MODEL), 1.0),
        'c_sample': nrm(ks[3], (DEC_BATCH, D_MODEL), 1.0),
        'cache_ckv': nrm(ks[4], (DEPTH, DEC_BATCH, PAST_LEN, KV_RANK), 1.0),
        'cache_kpe': nrm(ks[5], (DEPTH, DEC_BATCH, PAST_LEN, QK_ROPE), 1.0),
        'state_hgrn': nrm(ks[6], (DEPTH, DEC_BATCH, HG_HEADS, HG_EXPAND, HG_VDIM), 0.5),
        'w_ada': nrm(ks[7], (DEPTH, D_MODEL, 6 * D_MODEL), 0.5 * D_MODEL ** -0.5),
        'b_ada': nrm(ks[8], (DEPTH, 6 * D_MODEL), 0.02),
        'w_in': nrm(ks[9], (DEPTH, D_MODEL, N_IN), D_MODEL ** -0.5),
        'w_uq': nrm(ks[10], (DEPTH, Q_RANK, MLA_HEADS * (QK_NOPE + QK_ROPE)), Q_RANK ** -0.5),
        'g_q': 1.0 + nrm(ks[11], (DEPTH, Q_RANK), 0.01),
        'g_kv': 1.0 + nrm(ks[12], (DEPTH, KV_RANK), 0.01),
        'w_uk': nrm(ks[13], (DEPTH, KV_RANK, MLA_HEADS, QK_NOPE), KV_RANK ** -0.5),
        'w_uv': nrm(ks[14], (DEPTH, KV_RANK, MLA_HEADS, V_HEAD), KV_RANK ** -0.5),
        'hg_lower_bounds': nrm(ks[15], (DEPTH + 1, HG_FK), 0.1),
        'g_hgrn': 1.0 + nrm(ks[16], (DEPTH, HG_VDIM), 0.01),
        'w_out': nrm(ks[17], (DEPTH, MIX_WIDTH, D_MODEL), MIX_WIDTH ** -0.5),
        'w_ff1': nrm(ks[18], (DEPTH, D_MODEL, D_FF), D_MODEL ** -0.5),
        'w_ff2': nrm(ks[19], (DEPTH, D_FF, D_MODEL), D_FF ** -0.5),
        'g_final': 1.0 + nrm(ks[20], (D_MODEL,), 0.01),
    }


def reference(x_prompt, x_sample, c_prompt, c_sample, cache_ckv, cache_kpe, state_hgrn,
              w_ada, b_ada, w_in, w_uq, g_q, g_kv, w_uk, w_uv, hg_lower_bounds, g_hgrn,
              w_out, w_ff1, w_ff2, g_final):
    lb_all = jnp.cumsum(jax.nn.softmax(hg_lower_bounds.astype(jnp.float32), axis=0), axis=0)
    cos_p, sin_p = rope_tables(jnp.arange(x_prompt.shape[1]))
    cos_s, sin_s = rope_tables(PAST_LEN + jnp.arange(x_sample.shape[1]))
    yp, ys = x_prompt, x_sample
    ckv_p, kpe_p, st_p, ckv_s, kpe_s, st_s = [], [], [], [], [], []
    for l in range(DEPTH):
        lw = (w_ada[l], b_ada[l], w_in[l], w_uq[l], g_q[l], g_kv[l], w_uk[l], w_uv[l],
              g_hgrn[l], w_out[l], w_ff1[l], w_ff2[l])
        yp, a, b, s = trunk_layer(yp, c_prompt, cos_p, sin_p, lb_all[l], lw,
                                  mla_prompt_attn, hgrn_prompt)
        ckv_p.append(a); kpe_p.append(b); st_p.append(s)
        ys, a, b, s = trunk_layer(ys, c_sample, cos_s, sin_s, lb_all[l], lw,
                                  functools.partial(mla_sample_attn, cache_ckv[l], cache_kpe[l]),
                                  functools.partial(hgrn_sample, state_hgrn[l]))
        ckv_s.append(a); kpe_s.append(b); st_s.append(s)
    y_prompt = rms_norm(yp, g_final)
    y_sample = rms_norm(ys, g_final)
    return (y_prompt, y_sample, jnp.stack(ckv_p), jnp.stack(kpe_p), jnp.stack(st_p),
            jnp.stack(ckv_s), jnp.stack(kpe_s), jnp.stack(st_s))
```

```python
import functools
import math

import jax
import jax.numpy as jnp
from jax import lax
from jax.experimental import pallas as pl
from jax.experimental.pallas import tpu as pltpu

F32 = jnp.float32
BF16 = jnp.bfloat16

EPS = 1e-6
CHUNK = 64
N_HEADS = 8
QK_NOPE = 128
QK_ROPE = 64
V_HEAD = 128
Q_RANK = 512
KV_RANK = 512
ROPE_BASE = 10000.0
ATTN_SCALE = (QK_NOPE + QK_ROPE) ** -0.5
HG_HEADS = 8
HG_DIM = 128
HG_WIDTH = HG_HEADS * HG_DIM
HEAD_PACK = 2 * QK_NOPE
LANES = 128
VMEM_LIMIT = 56 * 1024 * 1024

_NT = (((1,), (1,)), ((), ()))
_TN = (((0,), (0,)), ((), ()))


def _rms(x):
    return x * lax.rsqrt(jnp.mean(x * x, axis=-1, keepdims=True) + EPS)


def _dot(a, b, dims=None):
    if dims is None:
        return jnp.dot(a, b, preferred_element_type=F32)
    return lax.dot_general(a, b, dims, preferred_element_type=F32)


def _resident(shape):
    return pl.BlockSpec(shape, lambda *_: (0,) * len(shape), pipeline_mode=pl.Buffered(1))


def _params(*sem):
    return pltpu.CompilerParams(dimension_semantics=sem, vmem_limit_bytes=VMEM_LIMIT)


def _mod_kernel(c_ref, w_ref, b_ref, o_ref):
    c = c_ref[...]
    a = (c * jax.nn.sigmoid(c)).astype(BF16)
    o_ref[...] = _dot(a, w_ref[...].astype(BF16)) + b_ref[...]


def _modulation(c_all, w_ada, b_ada):
    m, d = c_all.shape
    n = w_ada.shape[1]
    tn = 1024
    return pl.pallas_call(
        _mod_kernel,
        out_shape=jax.ShapeDtypeStruct((m, n), F32),
        grid=(n // tn,),
        in_specs=[pl.BlockSpec((m, d), lambda j: (0, 0)),
                  pl.BlockSpec((d, tn), lambda j: (0, j)),
                  pl.BlockSpec((1, tn), lambda j: (0, j))],
        out_specs=pl.BlockSpec((m, tn), lambda j: (0, j)),
        compiler_params=_params("arbitrary"),
        name="mod",
    )(c_all, w_ada, b_ada.reshape(1, n))


def _mla_in_kernel(x_ref, sh_ref, sc_ref, cos_ref, sin_ref, wa_ref, wkpe_ref, wuq_ref, wukv_ref,
                   gq_ref, gkv_ref, q_ref, ckv_ref, kpe_ref, *kv_refs):
    x = x_ref[...]
    h = (_rms(x) * (1.0 + sc_ref[...]) + sh_ref[...]).astype(BF16)
    cos = cos_ref[...]
    sin = sin_ref[...]

    def rope(t):
        return t * cos + pltpu.roll(t, QK_ROPE, 1) * sin

    a = _dot(h, wa_ref[...])
    cqn = (_rms(a[:, :Q_RANK]) * gq_ref[...]).astype(BF16)
    ckvn = _rms(a[:, Q_RANK:]) * gkv_ref[...]
    ckv_ref[...] = ckvn

    q = _dot(cqn, wuq_ref[...])
    for hd in range(N_HEADS):
        lo = hd * HEAD_PACK
        q_ref[:, lo:lo + QK_NOPE] = q[:, lo:lo + QK_NOPE].astype(BF16)
        q_ref[:, lo + QK_NOPE:lo + HEAD_PACK] = rope(q[:, lo + QK_NOPE:lo + HEAD_PACK]).astype(BF16)

    kp = rope(_dot(h, wkpe_ref[...]))
    kpe_ref[...] = kp[:, :QK_ROPE]

    if kv_refs:
        kf_ref, v_ref = kv_refs
        kv = _dot(ckvn.astype(BF16), wukv_ref[...])
        kpb = kp.astype(BF16)
        for hd in range(N_HEADS):
            lo = hd * HEAD_PACK
            kf_ref[:, lo:lo + QK_NOPE] = kv[:, hd * QK_NOPE:(hd + 1) * QK_NOPE].astype(BF16)
            kf_ref[:, lo + QK_NOPE:lo + HEAD_PACK] = kpb
        v_ref[...] = kv[:, N_HEADS * QK_NOPE:].astype(BF16)


def _mla_in(x, sh, sc, cos, sin, w, tm, with_kv):
    t, d = x.shape
    n_tiles = t // tm
    tiles_per_group = n_tiles // sh.shape[0]
    pos_tiles = cos.shape[0] // tm
    r = sh.shape[1]
    row = lambda n: pl.BlockSpec((tm, n), lambda i: (i, 0))
    mod = pl.BlockSpec((None, r, d), lambda i: (i // tiles_per_group, 0, 0))
    pos = pl.BlockSpec((tm, LANES), lambda i: (i % pos_tiles, 0))
    packed = N_HEADS * HEAD_PACK
    out_shape = [jax.ShapeDtypeStruct((t, packed), BF16),
                 jax.ShapeDtypeStruct((t, KV_RANK), F32),
                 jax.ShapeDtypeStruct((t, QK_ROPE), F32)]
    out_specs = [row(packed), row(KV_RANK), row(QK_ROPE)]
    if with_kv:
        out_shape += [jax.ShapeDtypeStruct((t, packed), BF16),
                      jax.ShapeDtypeStruct((t, N_HEADS * V_HEAD), BF16)]
        out_specs += [row(packed), row(N_HEADS * V_HEAD)]
    return pl.pallas_call(
        _mla_in_kernel,
        out_shape=out_shape,
        grid=(n_tiles,),
        in_specs=[row(d), mod, mod, pos, pos,
                  _resident(w["wa"].shape), _resident(w["wkpe"].shape),
                  _resident(w["wuq"].shape), _resident(w["wukv"].shape),
                  _resident((1, Q_RANK)), _resident((1, KV_RANK))],
        out_specs=out_specs,
        compiler_params=_params("arbitrary"),
        name="mla_in",
    )(x, sh, sc, cos, sin, w["wa"], w["wkpe"], w["wuq"], w["wukv"], w["gq"], w["gkv"])


def _hg_in_kernel(x_ref, sh_ref, sc_ref, wh_ref, lb_ref, qh_ref, kh_ref, vh_ref, gate_ref, g_ref):
    x = x_ref[...]
    h = (_rms(x) * (1.0 + sc_ref[...]) + sh_ref[...]).astype(BF16)
    n = HG_WIDTH
    hq = _dot(h, wh_ref[:, 0:n])
    qh_ref[...] = (hq * jax.nn.sigmoid(hq) * (HG_DIM ** -0.5)).astype(BF16)
    hf = _dot(h, wh_ref[:, n:2 * n])
    lb = lb_ref[...]
    f = lb + (1.0 - lb) * jax.nn.sigmoid(hf)
    kh_ref[...] = (1.0 - f).astype(BF16)
    g_ref[...] = jnp.log(f)
    vh_ref[...] = _dot(h, wh_ref[:, 2 * n:3 * n]).astype(BF16)
    hg = _dot(h, wh_ref[:, 3 * n:4 * n])
    gate_ref[...] = (hg * jax.nn.sigmoid(hg)).astype(BF16)


def _hg_in(x, sh, sc, w, tm):
    t, d = x.shape
    n_tiles = t // tm
    tiles_per_group = n_tiles // sh.shape[0]
    r = sh.shape[1]
    row = lambda n: pl.BlockSpec((tm, n), lambda i: (i, 0))
    mod = pl.BlockSpec((None, r, d), lambda i: (i // tiles_per_group, 0, 0))
    half = jax.ShapeDtypeStruct((t, HG_WIDTH), BF16)
    return pl.pallas_call(
        _hg_in_kernel,
        out_shape=[half, half, half, half, jax.ShapeDtypeStruct((t, HG_WIDTH), F32)],
        grid=(n_tiles,),
        in_specs=[row(d), mod, mod, _resident(w["wh"].shape), _resident((1, HG_WIDTH))],
        out_specs=[row(HG_WIDTH)] * 5,
        compiler_params=_params("arbitrary"),
        name="hg_in",
    )(x, sh, sc, w["wh"], w["lb"])


def _attn_kernel(q_ref, k_ref, v_ref, o_ref, *, tq, tk):
    qi = pl.program_id(2)
    q = q_ref[...]

    def step(start, carry, masked):
        m, l, acc = carry
        k = k_ref[pl.ds(start, tk), :]
        s = _dot(q, k, _NT)
        if masked:
            qc = (qi * tq + lax.broadcasted_iota(jnp.int32, (tq, tk), 0)) // CHUNK
            kc = (start + lax.broadcasted_iota(jnp.int32, (tq, tk), 1)) // CHUNK
            s = jnp.where(kc <= qc, s, -1e30)
        m_new = jnp.maximum(m, jnp.max(s, axis=-1, keepdims=True))
        p = jnp.exp2(s - m_new)
        alpha = jnp.exp2(m - m_new)
        l = alpha * l + jnp.sum(p, axis=-1, keepdims=True)
        acc = alpha * acc + _dot(p.astype(BF16), v_ref[pl.ds(start, tk), :])
        return m_new, l, acc

    init = (jnp.full((tq, 1), -1e30, F32), jnp.zeros((tq, 1), F32), jnp.zeros((tq, V_HEAD), F32))
    per_q = tq // tk
    carry = lax.fori_loop(
        0, qi * per_q,
        lambda j, c: step(pl.multiple_of(j * tk, tk), c, False), init)
    for d in range(per_q):
        carry = step(pl.multiple_of(qi * tq + d * tk, tk), carry, True)
    _, l, acc = carry
    o_ref[...] = (acc / l).astype(BF16)


def _attention(q, kf, v, batch, seq, tq, tk):
    nq = seq // tq
    return pl.pallas_call(
        functools.partial(_attn_kernel, tq=tq, tk=tk),
        out_shape=jax.ShapeDtypeStruct((batch * seq, N_HEADS * V_HEAD), BF16),
        grid=(batch, N_HEADS, nq),
        in_specs=[pl.BlockSpec((tq, HEAD_PACK), lambda b, h, i: (b * nq + i, h)),
                  pl.BlockSpec((seq, HEAD_PACK), lambda b, h, i: (b, h)),
                  pl.BlockSpec((seq, V_HEAD), lambda b, h, i: (b, h))],
        out_specs=pl.BlockSpec((tq, V_HEAD), lambda b, h, i: (b * nq + i, h)),
        compiler_params=_params("arbitrary", "arbitrary", "arbitrary"),
        name="attn",
    )(q, kf, v)


def _qlat_kernel(qn_ref, qp_ref, wuk_ref, ql_ref, qpe_ref):
    nb = ql_ref.shape[0]
    ql = _dot(qn_ref[...], wuk_ref[...], _NT)
    ql_ref[...] = ql.reshape(nb, -1, KV_RANK)
    qpe_ref[...] = qp_ref[...].astype(F32).reshape(nb, -1, LANES)


def _q_latent(q, wuk_t, nb, ln):
    t = q.shape[0]
    return pl.pallas_call(
        _qlat_kernel,
        out_shape=[jax.ShapeDtypeStruct((nb, N_HEADS, ln, KV_RANK), F32),
                   jax.ShapeDtypeStruct((nb, N_HEADS, ln, LANES), F32)],
        grid=(N_HEADS,),
        in_specs=[pl.BlockSpec((t, QK_NOPE), lambda h: (0, 2 * h)),
                  pl.BlockSpec((t, QK_NOPE), lambda h: (0, 2 * h + 1)),
                  pl.BlockSpec((None, KV_RANK, QK_NOPE), lambda h: (h, 0, 0))],
        out_specs=[pl.BlockSpec((nb, None, ln, KV_RANK), lambda h: (0, h, 0, 0)),
                   pl.BlockSpec((nb, None, ln, LANES), lambda h: (0, h, 0, 0))],
        compiler_params=_params("arbitrary"),
        name="q_latent",
    )(q, q, wuk_t)


def _dec_attn_kernel(ql_ref, qpe_ref, cckv_ref, ckpe_ref, nckv_ref, nkpe_ref, wuv_ref, o_ref, *, ln):
    rows = N_HEADS * ln
    ql = ql_ref[...].reshape(rows, KV_RANK).astype(BF16)
    qp = qpe_ref[...].reshape(rows, LANES)[:, :QK_ROPE].astype(BF16)
    ckc = cckv_ref[...].astype(BF16)
    kpc = ckpe_ref[...].astype(BF16)
    ckn = nckv_ref[...].astype(BF16)
    kpn = nkpe_ref[...].astype(BF16)
    s_c = _dot(ql, ckc, _NT) + _dot(qp, kpc, _NT)
    s_n = _dot(ql, ckn, _NT) + _dot(qp, kpn, _NT)
    m = jnp.maximum(jnp.max(s_c, axis=-1, keepdims=True), jnp.max(s_n, axis=-1, keepdims=True))
    p_c = jnp.exp2(s_c - m)
    p_n = jnp.exp2(s_n - m)
    l = jnp.sum(p_c, axis=-1, keepdims=True) + jnp.sum(p_n, axis=-1, keepdims=True)
    ctx = (_dot(p_c.astype(BF16), ckc) + _dot(p_n.astype(BF16), ckn)) / l
    ctx = ctx.astype(BF16)
    for hd in range(N_HEADS):
        o_ref[:, hd * V_HEAD:(hd + 1) * V_HEAD] = _dot(
            ctx[hd * ln:(hd + 1) * ln], wuv_ref[hd]).astype(BF16)


def _dec_attention(ql, qpe, cache_ckv, cache_kpe, ckv_new, kpe_new, wuv_t, ln):
    nb, past = cache_ckv.shape[0], cache_ckv.shape[1]
    return pl.pallas_call(
        functools.partial(_dec_attn_kernel, ln=ln),
        out_shape=jax.ShapeDtypeStruct((nb * ln, N_HEADS * V_HEAD), BF16),
        grid=(nb,),
        in_specs=[pl.BlockSpec((None, N_HEADS, ln, KV_RANK), lambda b: (b, 0, 0, 0)),
                  pl.BlockSpec((None, N_HEADS, ln, LANES), lambda b: (b, 0, 0, 0)),
                  pl.BlockSpec((None, past, KV_RANK), lambda b: (b, 0, 0)),
                  pl.BlockSpec((None, past, QK_ROPE), lambda b: (b, 0, 0)),
                  pl.BlockSpec((ln, KV_RANK), lambda b: (b, 0)),
                  pl.BlockSpec((ln, QK_ROPE), lambda b: (b, 0)),
                  _resident(wuv_t.shape)],
        out_specs=pl.BlockSpec((ln, N_HEADS * V_HEAD), lambda b: (b, 0)),
        compiler_params=_params("arbitrary"),
        name="dec_attn",
    )(ql, qpe, cache_ckv, cache_kpe, ckv_new, kpe_new, wuv_t)


def _gla_chunk(q, k, v, g, st, b_ref, ln):
    row = lax.broadcasted_iota(jnp.int32, (ln, HG_DIM), 0)
    b = g
    sft = 1
    while sft < ln:
        b = b + jnp.where(row >= sft, pltpu.roll(b, sft, 0), 0.0)
        sft *= 2
    b_ref[...] = b
    b_last = b_ref[ln - 1:ln, :]

    o = _dot((q * jnp.exp(b)).astype(BF16), st.astype(BF16), _NT)
    o = o + jnp.sum(q * k, axis=-1, keepdims=True) * v

    def mid_rows(offset):
        return jnp.concatenate(
            [jnp.broadcast_to(b_ref[r:r + 1, :], (8, HG_DIM)) for r in range(offset, ln, 8)], axis=0)

    t_idx = lax.broadcasted_iota(jnp.int32, (ln, ln), 0)
    s_idx = lax.broadcasted_iota(jnp.int32, (ln, ln), 1)
    a = jnp.zeros((ln, ln), F32)
    for lev in range(int(math.log2(ln))):
        m = 1 << lev
        upper = (row & m) != 0
        if lev == 0:
            e = jnp.where(upper, g, 0.0)
        else:
            if lev == 1:
                c = jnp.where((row & 4) == 0, mid_rows(1), mid_rows(5))
            elif lev == 2:
                c = mid_rows(3)
            else:
                c = jnp.concatenate(
                    [jnp.broadcast_to(b_ref[r:r + 1, :], (2 * m, HG_DIM))
                     for r in range(m - 1, ln, 2 * m)], axis=0)
            e = -jnp.abs(b - c)
        w = jnp.exp(e)
        qt = jnp.where(upper, q * w, 0.0).astype(BF16)
        kt = jnp.where(upper, 0.0, k * w).astype(BF16)
        same_block = ((t_idx ^ s_idx) >> (lev + 1)) == 0
        a = a + jnp.where(same_block, _dot(qt, kt, _NT), 0.0)
    vb = v.astype(BF16)
    o = o + _dot(a.astype(BF16), vb)

    kd = (k * jnp.exp(b_last - b)).astype(BF16)
    st_new = st * jnp.exp(b_last) + _dot(vb, kd, _TN)
    return o, st_new


def _hgrn_kernel(*refs, ln, n_chunks, has_init):
    if has_init:
        qh_ref, kh_ref, vh_ref, gate_ref, g_ref, gw_ref, s0_ref, o_ref, sout_ref, st_ref, b_ref = refs
    else:
        qh_ref, kh_ref, vh_ref, gate_ref, g_ref, gw_ref, o_ref, sout_ref, st_ref, b_ref = refs
        s0_ref = None
    si = pl.program_id(1)

    @pl.when(si == 0)
    def _():
        for hd in range(HG_HEADS):
            st_ref[hd] = s0_ref[hd].T if has_init else jnp.zeros((HG_DIM, HG_DIM), F32)

    gw = gw_ref[...]

    def body(c, carry):
        rows = pl.ds(pl.multiple_of(c * ln, ln), ln)
        for hd in range(HG_HEADS):
            lanes = slice(hd * HG_DIM, (hd + 1) * HG_DIM)
            o, st = _gla_chunk(qh_ref[rows, lanes].astype(F32), kh_ref[rows, lanes].astype(F32),
                               vh_ref[rows, lanes].astype(F32), g_ref[rows, lanes],
                               st_ref[hd], b_ref.at[hd], ln)
            st_ref[hd] = st
            o_ref[rows, lanes] = (_rms(o) * gw * gate_ref[rows, lanes].astype(F32)).astype(BF16)
        return carry

    lax.fori_loop(0, n_chunks, body, 0)

    @pl.when(si == pl.num_programs(1) - 1)
    def _():
        for hd in range(HG_HEADS):
            sout_ref[hd] = st_ref[hd].T


def _hgrn(qh, kh, vh, gate, g, gw, state0, batch, seq, ln, ts):
    ns = seq // ts
    has_init = state0 is not None
    row = pl.BlockSpec((ts, HG_WIDTH), lambda b, s: (b * ns + s, 0))
    st_spec = pl.BlockSpec((None, HG_HEADS, HG_DIM, HG_DIM), lambda b, s: (b, 0, 0, 0))
    in_specs = [row] * 5 + [_resident((1, HG_DIM))]
    args = [qh, kh, vh, gate, g, gw]
    if has_init:
        in_specs.append(st_spec)
        args.append(state0)
    return pl.pallas_call(
        functools.partial(_hgrn_kernel, ln=ln, n_chunks=ts // ln, has_init=has_init),
        out_shape=[jax.ShapeDtypeStruct((batch * seq, HG_WIDTH), BF16),
                   jax.ShapeDtypeStruct((batch, HG_HEADS, HG_DIM, HG_DIM), F32)],
        grid=(batch, ns),
        in_specs=in_specs,
        out_specs=[row, st_spec],
        scratch_shapes=[pltpu.VMEM((HG_HEADS, HG_DIM, HG_DIM), F32),
                        pltpu.VMEM((HG_HEADS, ln, HG_DIM), F32)],
        compiler_params=_params("arbitrary", "arbitrary"),
        name="hgrn",
    )(*args)


def _out_proj_kernel(x_ref, gt_ref, om_ref, oh_ref, w_ref, o_ref):
    n = om_ref.shape[1]
    mix = _dot(om_ref[...], w_ref[0:n, :]) + _dot(oh_ref[...], w_ref[n:, :])
    o_ref[...] = x_ref[...] + gt_ref[...] * mix


def _out_proj(x, gt, o_mla, o_hg, w_out, tm):
    t, d = x.shape
    n_tiles = t // tm
    tiles_per_group = n_tiles // gt.shape[0]
    r = gt.shape[1]
    row = lambda n: pl.BlockSpec((tm, n), lambda i: (i, 0))
    return pl.pallas_call(
        _out_proj_kernel,
        out_shape=jax.ShapeDtypeStruct((t, d), F32),
        grid=(n_tiles,),
        in_specs=[row(d), pl.BlockSpec((None, r, d), lambda i: (i // tiles_per_group, 0, 0)),
                  row(o_mla.shape[1]), row(o_hg.shape[1]), _resident(w_out.shape)],
        out_specs=row(d),
        compiler_params=_params("arbitrary"),
        name="out_proj",
    )(x, gt, o_mla, o_hg, w_out)


def _ffn_kernel(x_ref, sh_ref, sc_ref, gt_ref, w1_ref, w2_ref, gf_ref, o_ref, h_ref):
    f = pl.program_id(1)

    @pl.when(f == 0)
    def _():
        h_ref[...] = (_rms(x_ref[...]) * (1.0 + sc_ref[...]) + sh_ref[...]).astype(BF16)
        o_ref[...] = jnp.zeros_like(o_ref)

    a = jnp.maximum(_dot(h_ref[...], w1_ref[...]), 0.0)
    o_ref[...] += _dot((a * a).astype(BF16), w2_ref[...])

    @pl.when(f == pl.num_programs(1) - 1)
    def _():
        x2 = x_ref[...] + gt_ref[...] * o_ref[...]
        o_ref[...] = _rms(x2) * gf_ref[...]


def _ffn(x, sh, sc, gt, w1, w2, gf, tm, tf):
    t, d = x.shape
    dff = w1.shape[1]
    n_tiles = t // tm
    tiles_per_group = n_tiles // sh.shape[0]
    r = sh.shape[1]
    row = pl.BlockSpec((tm, d), lambda i, f: (i, 0))
    mod = pl.BlockSpec((None, r, d), lambda i, f: (i // tiles_per_group, 0, 0))
    return pl.pallas_call(
        _ffn_kernel,
        out_shape=jax.ShapeDtypeStruct((t, d), F32),
        grid=(n_tiles, dff // tf),
        in_specs=[row, mod, mod, mod,
                  pl.BlockSpec((d, tf), lambda i, f: (0, f)),
                  pl.BlockSpec((tf, d), lambda i, f: (f, 0)),
                  pl.BlockSpec((1, d), lambda i, f: (0, 0))],
        out_specs=row,
        scratch_shapes=[pltpu.VMEM((tm, d), BF16)],
        compiler_params=_params("arbitrary", "arbitrary"),
        name="ffn",
    )(x, sh, sc, gt, w1, w2, gf)


def _rot_half_cols(w):
    half = w.shape[-1] // 2
    return jnp.concatenate([-w[..., half:], w[..., :half]], axis=-1)


def _prep_weights(w_in, w_uq, g_q, g_kv, w_uk, w_uv, lb, g_hgrn, w_out, w_ff1, w_ff2, g_final):
    d = w_in.shape[0]
    o_kpe = Q_RANK + KV_RANK
    o_hg = o_kpe + QK_ROPE
    w_kpe = w_in[:, o_kpe:o_hg]
    q_scale = ATTN_SCALE * math.log2(math.e)
    wq = w_uq.reshape(Q_RANK, N_HEADS, QK_NOPE + QK_ROPE) * q_scale
    wq_pe = wq[..., QK_NOPE:]
    wuq = jnp.concatenate([wq[..., :QK_NOPE], wq_pe, _rot_half_cols(wq_pe)], axis=-1)
    return {
        "wa": w_in[:, :o_kpe].astype(BF16),
        "wkpe": jnp.concatenate([w_kpe, _rot_half_cols(w_kpe)], axis=-1).astype(BF16),
        "wh": w_in[:, o_hg:].astype(BF16),
        "wuq": wuq.reshape(Q_RANK, N_HEADS * HEAD_PACK).astype(BF16),
        "wukv": jnp.concatenate([w_uk.reshape(KV_RANK, -1), w_uv.reshape(KV_RANK, -1)], axis=-1).astype(BF16),
        "wuk_t": jnp.transpose(w_uk, (1, 0, 2)).astype(BF16),
        "wuv_t": jnp.transpose(w_uv, (1, 0, 2)).astype(BF16),
        "gq": g_q.reshape(1, -1), "gkv": g_kv.reshape(1, -1),
        "lb": lb.reshape(1, -1), "ghg": g_hgrn.reshape(1, -1),
        "wout": w_out.astype(BF16), "w1": w_ff1.astype(BF16), "w2": w_ff2.astype(BF16),
        "gfin": g_final.reshape(1, d),
    }


def _rope_tables(pos):
    half = QK_ROPE // 2
    inv_freq = jnp.exp(-math.log(ROPE_BASE) * jnp.arange(half, dtype=F32) / half)
    ang = pos.astype(F32)[:, None] * inv_freq[None, :]
    pad = jnp.zeros((pos.shape[0], LANES - QK_ROPE), F32)
    cos, sin = jnp.cos(ang), jnp.sin(ang)
    return jnp.concatenate([cos, cos, pad], axis=-1), jnp.concatenate([sin, sin, pad], axis=-1)


def _layer(x, mods, cos, sin, w, batch, seq, cache):
    sh1, sc1, gt1, sh2, sc2, gt2 = mods
    t = x.shape[0]
    tm = min(512, t)
    prompt = cache is None
    if prompt:
        q, ckv, kpe, kf, v = _mla_in(x, sh1, sc1, cos, sin, w, tm, True)
        o_mla = _attention(q, kf, v, batch, seq, 512, 512)
    else:
        cache_ckv, cache_kpe, state0 = cache
        q, ckv, kpe = _mla_in(x, sh1, sc1, cos, sin, w, tm, False)
        ql, qpe = _q_latent(q, w["wuk_t"], batch, seq)
        o_mla = _dec_attention(ql, qpe, cache_ckv, cache_kpe, ckv, kpe, w["wuv_t"], seq)
    qh, kh, vh, gate, g = _hg_in(x, sh1, sc1, w, tm)
    if prompt:
        o_hg, s_fin = _hgrn(qh, kh, vh, gate, g, w["ghg"], None, batch, seq, CHUNK, 512)
    else:
        o_hg, s_fin = _hgrn(qh, kh, vh, gate, g, w["ghg"], state0, batch, seq, seq, seq)
    x1 = _out_proj(x, gt1, o_mla, o_hg, w["wout"], tm)
    y = _ffn(x1, sh2, sc2, gt2, w["w1"], w["w2"], w["gfin"], min(1024, t), 512)
    return y, ckv, kpe, s_fin


def kernel(x_prompt, x_sample, c_prompt, c_sample, cache_ckv, cache_kpe, state_hgrn, w_ada, b_ada, w_in, w_uq, g_q, g_kv, w_uk, w_uv, hg_lower_bounds, g_hgrn, w_out, w_ff1, w_ff2, g_final):
    depth = w_ada.shape[0]
    assert depth == 1, "single-layer step"
    bp, sp, d = x_prompt.shape
    bs, ss, _ = x_sample.shape
    past = cache_ckv.shape[2]

    lb_all = jnp.cumsum(jax.nn.softmax(hg_lower_bounds.astype(F32), axis=0), axis=0)
    w = _prep_weights(w_in[0], w_uq[0], g_q[0], g_kv[0], w_uk[0], w_uv[0], lb_all[0], g_hgrn[0],
                      w_out[0], w_ff1[0], w_ff2[0], g_final)

    c_all = jnp.concatenate([c_prompt, c_sample], axis=0)
    n_c = c_all.shape[0]
    c_pad = jnp.pad(c_all, ((0, -n_c % 8), (0, 0)))
    mod = _modulation(c_pad, w_ada[0], b_ada[0])[:n_c]
    mod_p = [m.reshape(bp, 1, d) for m in jnp.split(mod[:bp], 6, axis=-1)]
    mod_s = [jnp.repeat(m, ss, axis=0).reshape(1, bs * ss, d) for m in jnp.split(mod[bp:], 6, axis=-1)]

    cos_p, sin_p = _rope_tables(jnp.arange(sp))
    cos_s, sin_s = _rope_tables(jnp.tile(past + jnp.arange(ss), bs))

    yp, ckv_p, kpe_p, st_p = _layer(x_prompt.reshape(bp * sp, d), mod_p, cos_p, sin_p, w, bp, sp, None)
    ys, ckv_s, kpe_s, st_s = _layer(x_sample.reshape(bs * ss, d), mod_s, cos_s, sin_s, w, bs, ss,
                                    (cache_ckv[0], cache_kpe[0], state_hgrn[0]))
    return (yp.reshape(bp, sp, d), ys.reshape(bs, ss, d),
            ckv_p.reshape(1, bp, sp, KV_RANK), kpe_p.reshape(1, bp, sp, QK_ROPE), st_p[None],
            ckv_s.reshape(1, bs, ss, KV_RANK), kpe_s.reshape(1, bs, ss, QK_ROPE), st_s[None])
```

```python
import functools
import math

import jax
import jax.numpy as jnp
from jax import lax
from jax.experimental import pallas as pl
from jax.experimental.pallas import tpu as pltpu

F32 = jnp.float32
BF16 = jnp.bfloat16

EPS = 1e-6
CHUNK = 64
N_HEADS = 8
QK_NOPE = 128
QK_ROPE = 64
V_HEAD = 128
Q_RANK = 512
KV_RANK = 512
ROPE_BASE = 10000.0
ATTN_SCALE = (QK_NOPE + QK_ROPE) ** -0.5
HG_HEADS = 8
HG_DIM = 128
HG_WIDTH = HG_HEADS * HG_DIM
HEAD_PACK = 2 * QK_NOPE
LANES = 128
VMEM_LIMIT = 56 * 1024 * 1024

_NT = (((1,), (1,)), ((), ()))
_TN = (((0,), (0,)), ((), ()))


def _rms(x):
    return x * lax.rsqrt(jnp.mean(x * x, axis=-1, keepdims=True) + EPS)


def _dot(a, b, dims=None):
    if dims is None:
        return jnp.dot(a, b, preferred_element_type=F32)
    return lax.dot_general(a, b, dims, preferred_element_type=F32)


def _resident(shape):
    return pl.BlockSpec(shape, lambda *_: (0,) * len(shape), pipeline_mode=pl.Buffered(1))


def _params(*sem):
    return pltpu.CompilerParams(dimension_semantics=sem, vmem_limit_bytes=VMEM_LIMIT)


def _mod_kernel(c_ref, w_ref, b_ref, o_ref):
    c = c_ref[...]
    a = (c * jax.nn.sigmoid(c)).astype(BF16)
    o_ref[...] = _dot(a, w_ref[...].astype(BF16)) + b_ref[...]


def _modulation(c_all, w_ada, b_ada):
    m, d = c_all.shape
    n = w_ada.shape[1]
    tn = 1024
    return pl.pallas_call(
        _mod_kernel,
        out_shape=jax.ShapeDtypeStruct((m, n), F32),
        grid=(n // tn,),
        in_specs=[pl.BlockSpec((m, d), lambda j: (0, 0)),
                  pl.BlockSpec((d, tn), lambda j: (0, j)),
                  pl.BlockSpec((1, tn), lambda j: (0, j))],
        out_specs=pl.BlockSpec((m, tn), lambda j: (0, j)),
        compiler_params=_params("arbitrary"),
        name="mod",
    )(c_all, w_ada, b_ada.reshape(1, n))


def _mla_in_kernel(x_ref, sh_ref, sc_ref, cos_ref, sin_ref, wa_ref, wkpe_ref, wuq_ref, wukv_ref,
                   gq_ref, gkv_ref, q_ref, ckv_ref, kpe_ref, *kv_refs):
    x = x_ref[...]
    h = (_rms(x) * (1.0 + sc_ref[...]) + sh_ref[...]).astype(BF16)
    cos = cos_ref[...]
    sin = sin_ref[...]

    def rope(t):
        return t * cos + pltpu.roll(t, QK_ROPE, 1) * sin

    a = _dot(h, wa_ref[...])
    cqn = (_rms(a[:, :Q_RANK]) * gq_ref[...]).astype(BF16)
    ckvn = _rms(a[:, Q_RANK:]) * gkv_ref[...]
    ckv_ref[...] = ckvn

    q = _dot(cqn, wuq_ref[...])
    for hd in range(N_HEADS):
        lo = hd * HEAD_PACK
        q_ref[:, lo:lo + QK_NOPE] = q[:, lo:lo + QK_NOPE].astype(BF16)
        q_ref[:, lo + QK_NOPE:lo + HEAD_PACK] = rope(q[:, lo + QK_NOPE:lo + HEAD_PACK]).astype(BF16)

    kp = rope(_dot(h, wkpe_ref[...]))
    kpe_ref[...] = kp[:, :QK_ROPE]

    if kv_refs:
        kf_ref, v_ref = kv_refs
        kv = _dot(ckvn.astype(BF16), wukv_ref[...])
        kpb = kp.astype(BF16)
        for hd in range(N_HEADS):
            lo = hd * HEAD_PACK
            kf_ref[:, lo:lo + QK_NOPE] = kv[:, hd * QK_NOPE:(hd + 1) * QK_NOPE].astype(BF16)
            kf_ref[:, lo + QK_NOPE:lo + HEAD_PACK] = kpb
            vo = N_HEADS * QK_NOPE + hd * V_HEAD
            v_ref[:, lo:lo + V_HEAD] = kv[:, vo:vo + V_HEAD].astype(BF16)
            v_ref[:, lo + V_HEAD:lo + HEAD_PACK] = jnp.ones((kv.shape[0], HEAD_PACK - V_HEAD), BF16)


def _mla_in(x, sh, sc, cos, sin, w, tm, with_kv):
    t, d = x.shape
    n_tiles = t // tm
    tiles_per_group = n_tiles // sh.shape[0]
    pos_tiles = cos.shape[0] // tm
    r = sh.shape[1]
    row = lambda n: pl.BlockSpec((tm, n), lambda i: (i, 0))
    mod = pl.BlockSpec((None, r, d), lambda i: (i // tiles_per_group, 0, 0))
    pos = pl.BlockSpec((tm, LANES), lambda i: (i % pos_tiles, 0))
    packed = N_HEADS * HEAD_PACK
    out_shape = [jax.ShapeDtypeStruct((t, packed), BF16),
                 jax.ShapeDtypeStruct((t, KV_RANK), F32),
                 jax.ShapeDtypeStruct((t, QK_ROPE), F32)]
    out_specs = [row(packed), row(KV_RANK), row(QK_ROPE)]
    if with_kv:
        out_shape += [jax.ShapeDtypeStruct((t, packed), BF16),
                      jax.ShapeDtypeStruct((t, packed), BF16)]
        out_specs += [row(packed), row(packed)]
    return pl.pallas_call(
        _mla_in_kernel,
        out_shape=out_shape,
        grid=(n_tiles,),
        in_specs=[row(d), mod, mod, pos, pos,
                  _resident(w["wa"].shape), _resident(w["wkpe"].shape),
                  _resident(w["wuq"].shape), _resident(w["wukv"].shape),
                  _resident((1, Q_RANK)), _resident((1, KV_RANK))],
        out_specs=out_specs,
        compiler_params=_params("arbitrary"),
        name="mla_in",
    )(x, sh, sc, cos, sin, w["wa"], w["wkpe"], w["wuq"], w["wukv"], w["gq"], w["gkv"])


def _hg_in_kernel(x_ref, sh_ref, sc_ref, wh_ref, lb_ref, qh_ref, kh_ref, vh_ref, gate_ref, g_ref):
    x = x_ref[...]
    h = (_rms(x) * (1.0 + sc_ref[...]) + sh_ref[...]).astype(BF16)
    n = HG_WIDTH
    hq = _dot(h, wh_ref[:, 0:n])
    qh_ref[...] = (hq * jax.nn.sigmoid(hq) * (HG_DIM ** -0.5)).astype(BF16)
    hf = _dot(h, wh_ref[:, n:2 * n])
    lb = lb_ref[...]
    f = lb + (1.0 - lb) * jax.nn.sigmoid(hf)
    kh_ref[...] = (1.0 - f).astype(BF16)
    g_ref[...] = jnp.log(f)
    vh_ref[...] = _dot(h, wh_ref[:, 2 * n:3 * n]).astype(BF16)
    hg = _dot(h, wh_ref[:, 3 * n:4 * n])
    gate_ref[...] = (hg * jax.nn.sigmoid(hg)).astype(BF16)


def _hg_in(x, sh, sc, w, tm):
    t, d = x.shape
    n_tiles = t // tm
    tiles_per_group = n_tiles // sh.shape[0]
    r = sh.shape[1]
    row = lambda n: pl.BlockSpec((tm, n), lambda i: (i, 0))
    mod = pl.BlockSpec((None, r, d), lambda i: (i // tiles_per_group, 0, 0))
    half = jax.ShapeDtypeStruct((t, HG_WIDTH), BF16)
    return pl.pallas_call(
        _hg_in_kernel,
        out_shape=[half, half, half, half, jax.ShapeDtypeStruct((t, HG_WIDTH), F32)],
        grid=(n_tiles,),
        in_specs=[row(d), mod, mod, _resident(w["wh"].shape), _resident((1, HG_WIDTH))],
        out_specs=[row(HG_WIDTH)] * 5,
        compiler_params=_params("arbitrary"),
        name="hg_in",
    )(x, sh, sc, w["wh"], w["lb"])


def _attn_kernel(q_ref, k_ref, v_ref, o_ref, s0_ref, s1_ref, m_ref, acc_ref, *, tk):
    qi = pl.program_id(2)
    qs = [q_ref[0:tk, :], q_ref[tk:2 * tk, :]]
    diag_ok = (lax.broadcasted_iota(jnp.int32, (tk, tk), 1) // CHUNK
               <= lax.broadcasted_iota(jnp.int32, (tk, tk), 0) // CHUNK)

    def kv_rows(blk):
        return pl.ds(pl.multiple_of(blk * tk, tk), tk)

    def scores(r, blk):
        return _dot(qs[r], k_ref[kv_rows(blk), :], _NT)

    def update(r, s, blk, masked):
        if masked:
            s = jnp.where(diag_ok, s, -1e30)
        m = m_ref[r]
        m_new = jnp.maximum(m, jnp.max(s, axis=-1, keepdims=True))
        p = jnp.exp2(s - m_new).astype(BF16)
        m_ref[r] = m_new
        acc_ref[r] = jnp.exp2(m - m_new) * acc_ref[r] + _dot(p, v_ref[kv_rows(blk), :])

    for r in range(2):
        m_ref[r] = jnp.full((tk, 1), -1e30, F32)
        acc_ref[r] = jnp.zeros((tk, HEAD_PACK), F32)
        s0_ref[r] = scores(r, 0)

    def body(j, carry):
        blk = 2 * j
        for cur, nxt in ((s0_ref, s1_ref), (s1_ref, s0_ref)):
            s = [cur[0], cur[1]]
            for r in range(2):
                nxt[r] = scores(r, blk + 1)
            for r in range(2):
                update(r, s[r], blk, False)
            blk = blk + 1
        return carry

    lax.fori_loop(0, qi, body, 0)
    s_last = scores(1, 2 * qi + 1)
    update(0, s0_ref[0], 2 * qi, True)
    update(1, s0_ref[1], 2 * qi, False)
    update(1, s_last, 2 * qi + 1, True)
    for r in range(2):
        acc = acc_ref[r]
        o_ref[r * tk:(r + 1) * tk, :] = (acc[:, :V_HEAD] / acc[:, V_HEAD:]).astype(BF16)


def _attention(q, kf, vp, batch, seq, tk):
    tq = 2 * tk
    nq = seq // tq
    return pl.pallas_call(
        functools.partial(_attn_kernel, tk=tk),
        out_shape=jax.ShapeDtypeStruct((batch * seq, N_HEADS * V_HEAD), BF16),
        grid=(batch, N_HEADS, nq),
        in_specs=[pl.BlockSpec((tq, HEAD_PACK), lambda b, h, i: (b * nq + i, h)),
                  pl.BlockSpec((seq, HEAD_PACK), lambda b, h, i: (b, h)),
                  pl.BlockSpec((seq, HEAD_PACK), lambda b, h, i: (b, h))],
        out_specs=pl.BlockSpec((tq, V_HEAD), lambda b, h, i: (b * nq + i, h)),
        scratch_shapes=[pltpu.VMEM((2, tk, tk), F32), pltpu.VMEM((2, tk, tk), F32),
                        pltpu.VMEM((2, tk, 1), F32), pltpu.VMEM((2, tk, HEAD_PACK), F32)],
        compiler_params=_params("arbitrary", "arbitrary", "arbitrary"),
        name="attn",
    )(q, kf, vp)


def _qlat_kernel(qn_ref, qp_ref, wuk_ref, ql_ref, qpe_ref):
    nb = ql_ref.shape[0]
    ql = _dot(qn_ref[...], wuk_ref[...], _NT)
    ql_ref[...] = ql.reshape(nb, -1, KV_RANK)
    qpe_ref[...] = qp_ref[...].astype(F32).reshape(nb, -1, LANES)


def _q_latent(q, wuk_t, nb, ln):
    t = q.shape[0]
    return pl.pallas_call(
        _qlat_kernel,
        out_shape=[jax.ShapeDtypeStruct((nb, N_HEADS, ln, KV_RANK), F32),
                   jax.ShapeDtypeStruct((nb, N_HEADS, ln, LANES), F32)],
        grid=(N_HEADS,),
        in_specs=[pl.BlockSpec((t, QK_NOPE), lambda h: (0, 2 * h)),
                  pl.BlockSpec((t, QK_NOPE), lambda h: (0, 2 * h + 1)),
                  pl.BlockSpec((None, KV_RANK, QK_NOPE), lambda h: (h, 0, 0))],
        out_specs=[pl.BlockSpec((nb, None, ln, KV_RANK), lambda h: (0, h, 0, 0)),
                   pl.BlockSpec((nb, None, ln, LANES), lambda h: (0, h, 0, 0))],
        compiler_params=_params("arbitrary"),
        name="q_latent",
    )(q, q, wuk_t)


def _dec_attn_kernel(ql_ref, qpe_ref, cckv_ref, ckpe_ref, nckv_ref, nkpe_ref, wuv_ref, o_ref, *, ln):
    rows = N_HEADS * ln
    ql = ql_ref[...].reshape(rows, KV_RANK).astype(BF16)
    qp = qpe_ref[...].reshape(rows, LANES)[:, :QK_ROPE].astype(BF16)
    ckc = cckv_ref[...].astype(BF16)
    kpc = ckpe_ref[...].astype(BF16)
    ckn = nckv_ref[...].astype(BF16)
    kpn = nkpe_ref[...].astype(BF16)
    s_c = _dot(ql, ckc, _NT) + _dot(qp, kpc, _NT)
    s_n = _dot(ql, ckn, _NT) + _dot(qp, kpn, _NT)
    m = jnp.maximum(jnp.max(s_c, axis=-1, keepdims=True), jnp.max(s_n, axis=-1, keepdims=True))
    p_c = jnp.exp2(s_c - m)
    p_n = jnp.exp2(s_n - m)
    l = jnp.sum(p_c, axis=-1, keepdims=True) + jnp.sum(p_n, axis=-1, keepdims=True)
    ctx = (_dot(p_c.astype(BF16), ckc) + _dot(p_n.astype(BF16), ckn)) / l
    ctx = ctx.astype(BF16)
    for hd in range(N_HEADS):
        o_ref[:, hd * V_HEAD:(hd + 1) * V_HEAD] = _dot(
            ctx[hd * ln:(hd + 1) * ln], wuv_ref[hd]).astype(BF16)


def _dec_attention(ql, qpe, cache_ckv, cache_kpe, ckv_new, kpe_new, wuv_t, ln):
    nb, past = cache_ckv.shape[0], cache_ckv.shape[1]
    return pl.pallas_call(
        functools.partial(_dec_attn_kernel, ln=ln),
        out_shape=jax.ShapeDtypeStruct((nb * ln, N_HEADS * V_HEAD), BF16),
        grid=(nb,),
        in_specs=[pl.BlockSpec((None, N_HEADS, ln, KV_RANK), lambda b: (b, 0, 0, 0)),
                  pl.BlockSpec((None, N_HEADS, ln, LANES), lambda b: (b, 0, 0, 0)),
                  pl.BlockSpec((None, past, KV_RANK), lambda b: (b, 0, 0)),
                  pl.BlockSpec((None, past, QK_ROPE), lambda b: (b, 0, 0)),
                  pl.BlockSpec((ln, KV_RANK), lambda b: (b, 0)),
                  pl.BlockSpec((ln, QK_ROPE), lambda b: (b, 0)),
                  _resident(wuv_t.shape)],
        out_specs=pl.BlockSpec((ln, N_HEADS * V_HEAD), lambda b: (b, 0)),
        compiler_params=_params("arbitrary"),
        name="dec_attn",
    )(ql, qpe, cache_ckv, cache_kpe, ckv_new, kpe_new, wuv_t)


def _gla_chunk(q, k, v, g, st, b_ref, ln):
    row = lax.broadcasted_iota(jnp.int32, (ln, HG_DIM), 0)
    b = g
    sft = 1
    while sft < ln:
        b = b + jnp.where(row >= sft, pltpu.roll(b, sft, 0), 0.0)
        sft *= 2
    b_ref[...] = b
    b_last = b_ref[ln - 1:ln, :]

    o = _dot((q * jnp.exp(b)).astype(BF16), st.astype(BF16), _NT)
    o = o + jnp.sum(q * k, axis=-1, keepdims=True) * v

    def mid_rows(offset):
        return jnp.concatenate(
            [jnp.broadcast_to(b_ref[r:r + 1, :], (8, HG_DIM)) for r in range(offset, ln, 8)], axis=0)

    t_idx = lax.broadcasted_iota(jnp.int32, (ln, ln), 0)
    s_idx = lax.broadcasted_iota(jnp.int32, (ln, ln), 1)
    a = jnp.zeros((ln, ln), F32)
    for lev in range(int(math.log2(ln))):
        m = 1 << lev
        upper = (row & m) != 0
        if lev == 0:
            e = jnp.where(upper, g, 0.0)
        else:
            if lev == 1:
                c = jnp.where((row & 4) == 0, mid_rows(1), mid_rows(5))
            elif lev == 2:
                c = mid_rows(3)
            else:
                c = jnp.concatenate(
                    [jnp.broadcast_to(b_ref[r:r + 1, :], (2 * m, HG_DIM))
                     for r in range(m - 1, ln, 2 * m)], axis=0)
            e = -jnp.abs(b - c)
        w = jnp.exp(e)
        qt = jnp.where(upper, q * w, 0.0).astype(BF16)
        kt = jnp.where(upper, 0.0, k * w).astype(BF16)
        same_block = ((t_idx ^ s_idx) >> (lev + 1)) == 0
        a = a + jnp.where(same_block, _dot(qt, kt, _NT), 0.0)
    vb = v.astype(BF16)
    o = o + _dot(a.astype(BF16), vb)

    kd = (k * jnp.exp(b_last - b)).astype(BF16)
    st_new = st * jnp.exp(b_last) + _dot(vb, kd, _TN)
    return o, st_new


def _hgrn_kernel(*refs, ln, n_chunks, has_init):
    if has_init:
        qh_ref, kh_ref, vh_ref, gate_ref, g_ref, gw_ref, s0_ref, o_ref, sout_ref, st_ref, b_ref = refs
    else:
        qh_ref, kh_ref, vh_ref, gate_ref, g_ref, gw_ref, o_ref, sout_ref, st_ref, b_ref = refs
        s0_ref = None
    si = pl.program_id(1)

    @pl.when(si == 0)
    def _():
        for hd in range(HG_HEADS):
            st_ref[hd] = s0_ref[hd].T if has_init else jnp.zeros((HG_DIM, HG_DIM), F32)

    gw = gw_ref[...]

    def body(c, carry):
        rows = pl.ds(pl.multiple_of(c * ln, ln), ln)
        for hd in range(HG_HEADS):
            lanes = slice(hd * HG_DIM, (hd + 1) * HG_DIM)
            o, st = _gla_chunk(qh_ref[rows, lanes].astype(F32), kh_ref[rows, lanes].astype(F32),
                               vh_ref[rows, lanes].astype(F32), g_ref[rows, lanes],
                               st_ref[hd], b_ref.at[hd], ln)
            st_ref[hd] = st
            o_ref[rows, lanes] = (_rms(o) * gw * gate_ref[rows, lanes].astype(F32)).astype(BF16)
        return carry

    lax.fori_loop(0, n_chunks, body, 0)

    @pl.when(si == pl.num_programs(1) - 1)
    def _():
        for hd in range(HG_HEADS):
            sout_ref[hd] = st_ref[hd].T


def _hgrn(qh, kh, vh, gate, g, gw, state0, batch, seq, ln, ts):
    ns = seq // ts
    has_init = state0 is not None
    row = pl.BlockSpec((ts, HG_WIDTH), lambda b, s: (b * ns + s, 0))
    st_spec = pl.BlockSpec((None, HG_HEADS, HG_DIM, HG_DIM), lambda b, s: (b, 0, 0, 0))
    in_specs = [row] * 5 + [_resident((1, HG_DIM))]
    args = [qh, kh, vh, gate, g, gw]
    if has_init:
        in_specs.append(st_spec)
        args.append(state0)
    return pl.pallas_call(
        functools.partial(_hgrn_kernel, ln=ln, n_chunks=ts // ln, has_init=has_init),
        out_shape=[jax.ShapeDtypeStruct((batch * seq, HG_WIDTH), BF16),
                   jax.ShapeDtypeStruct((batch, HG_HEADS, HG_DIM, HG_DIM), F32)],
        grid=(batch, ns),
        in_specs=in_specs,
        out_specs=[row, st_spec],
        scratch_shapes=[pltpu.VMEM((HG_HEADS, HG_DIM, HG_DIM), F32),
                        pltpu.VMEM((HG_HEADS, ln, HG_DIM), F32)],
        compiler_params=_params("arbitrary", "arbitrary"),
        name="hgrn",
    )(*args)


def _out_proj_kernel(x_ref, gt_ref, om_ref, oh_ref, w_ref, o_ref):
    n = om_ref.shape[1]
    mix = _dot(om_ref[...], w_ref[0:n, :]) + _dot(oh_ref[...], w_ref[n:, :])
    o_ref[...] = x_ref[...] + gt_ref[...] * mix


def _out_proj(x, gt, o_mla, o_hg, w_out, tm):
    t, d = x.shape
    n_tiles = t // tm
    tiles_per_group = n_tiles // gt.shape[0]
    r = gt.shape[1]
    row = lambda n: pl.BlockSpec((tm, n), lambda i: (i, 0))
    return pl.pallas_call(
        _out_proj_kernel,
        out_shape=jax.ShapeDtypeStruct((t, d), F32),
        grid=(n_tiles,),
        in_specs=[row(d), pl.BlockSpec((None, r, d), lambda i: (i // tiles_per_group, 0, 0)),
                  row(o_mla.shape[1]), row(o_hg.shape[1]), _resident(w_out.shape)],
        out_specs=row(d),
        compiler_params=_params("arbitrary"),
        name="out_proj",
    )(x, gt, o_mla, o_hg, w_out)


def _ffn_kernel(x_ref, sh_ref, sc_ref, gt_ref, w1_ref, w2_ref, gf_ref, o_ref, h_ref):
    f = pl.program_id(1)

    @pl.when(f == 0)
    def _():
        h_ref[...] = (_rms(x_ref[...]) * (1.0 + sc_ref[...]) + sh_ref[...]).astype(BF16)
        o_ref[...] = jnp.zeros_like(o_ref)

    a = jnp.maximum(_dot(h_ref[...], w1_ref[...]), 0.0)
    o_ref[...] += _dot((a * a).astype(BF16), w2_ref[...])

    @pl.when(f == pl.num_programs(1) - 1)
    def _():
        x2 = x_ref[...] + gt_ref[...] * o_ref[...]
        o_ref[...] = _rms(x2) * gf_ref[...]


def _ffn(x, sh, sc, gt, w1, w2, gf, tm, tf):
    t, d = x.shape
    dff = w1.shape[1]
    n_tiles = t // tm
    tiles_per_group = n_tiles // sh.shape[0]
    r = sh.shape[1]
    row = pl.BlockSpec((tm, d), lambda i, f: (i, 0))
    mod = pl.BlockSpec((None, r, d), lambda i, f: (i // tiles_per_group, 0, 0))
    return pl.pallas_call(
        _ffn_kernel,
        out_shape=jax.ShapeDtypeStruct((t, d), F32),
        grid=(n_tiles, dff // tf),
        in_specs=[row, mod, mod, mod,
                  pl.BlockSpec((d, tf), lambda i, f: (0, f)),
                  pl.BlockSpec((tf, d), lambda i, f: (f, 0)),
                  pl.BlockSpec((1, d), lambda i, f: (0, 0))],
        out_specs=row,
        scratch_shapes=[pltpu.VMEM((tm, d), BF16)],
        compiler_params=_params("arbitrary", "arbitrary"),
        name="ffn",
    )(x, sh, sc, gt, w1, w2, gf)


def _rot_half_cols(w):
    half = w.shape[-1] // 2
    return jnp.concatenate([-w[..., half:], w[..., :half]], axis=-1)


def _prep_weights(w_in, w_uq, g_q, g_kv, w_uk, w_uv, lb, g_hgrn, w_out, w_ff1, w_ff2, g_final):
    d = w_in.shape[0]
    o_kpe = Q_RANK + KV_RANK
    o_hg = o_kpe + QK_ROPE
    w_kpe = w_in[:, o_kpe:o_hg]
    q_scale = ATTN_SCALE * math.log2(math.e)
    wq = w_uq.reshape(Q_RANK, N_HEADS, QK_NOPE + QK_ROPE) * q_scale
    wq_pe = wq[..., QK_NOPE:]
    wuq = jnp.concatenate([wq[..., :QK_NOPE], wq_pe, _rot_half_cols(wq_pe)], axis=-1)
    return {
        "wa": w_in[:, :o_kpe].astype(BF16),
        "wkpe": jnp.concatenate([w_kpe, _rot_half_cols(w_kpe)], axis=-1).astype(BF16),
        "wh": w_in[:, o_hg:].astype(BF16),
        "wuq": wuq.reshape(Q_RANK, N_HEADS * HEAD_PACK).astype(BF16),
        "wukv": jnp.concatenate([w_uk.reshape(KV_RANK, -1), w_uv.reshape(KV_RANK, -1)], axis=-1).astype(BF16),
        "wuk_t": jnp.transpose(w_uk, (1, 0, 2)).astype(BF16),
        "wuv_t": jnp.transpose(w_uv, (1, 0, 2)).astype(BF16),
        "gq": g_q.reshape(1, -1), "gkv": g_kv.reshape(1, -1),
        "lb": lb.reshape(1, -1), "ghg": g_hgrn.reshape(1, -1),
        "wout": w_out.astype(BF16), "w1": w_ff1.astype(BF16), "w2": w_ff2.astype(BF16),
        "gfin": g_final.reshape(1, d),
    }


def _rope_tables(pos):
    half = QK_ROPE // 2
    inv_freq = jnp.exp(-math.log(ROPE_BASE) * jnp.arange(half, dtype=F32) / half)
    ang = pos.astype(F32)[:, None] * inv_freq[None, :]
    pad = jnp.zeros((pos.shape[0], LANES - QK_ROPE), F32)
    cos, sin = jnp.cos(ang), jnp.sin(ang)
    return jnp.concatenate([cos, cos, pad], axis=-1), jnp.concatenate([sin, sin, pad], axis=-1)


def _layer(x, mods, cos, sin, w, batch, seq, cache):
    sh1, sc1, gt1, sh2, sc2, gt2 = mods
    t = x.shape[0]
    tm = min(512, t)
    prompt = cache is None
    if prompt:
        q, ckv, kpe, kf, v = _mla_in(x, sh1, sc1, cos, sin, w, tm, True)
        o_mla = _attention(q, kf, v, batch, seq, 512)
    else:
        cache_ckv, cache_kpe, state0 = cache
        q, ckv, kpe = _mla_in(x, sh1, sc1, cos, sin, w, tm, False)
        ql, qpe = _q_latent(q, w["wuk_t"], batch, seq)
        o_mla = _dec_attention(ql, qpe, cache_ckv, cache_kpe, ckv, kpe, w["wuv_t"], seq)
    qh, kh, vh, gate, g = _hg_in(x, sh1, sc1, w, tm)
    if prompt:
        o_hg, s_fin = _hgrn(qh, kh, vh, gate, g, w["ghg"], None, batch, seq, CHUNK, 512)
    else:
        o_hg, s_fin = _hgrn(qh, kh, vh, gate, g, w["ghg"], state0, batch, seq, seq, seq)
    x1 = _out_proj(x, gt1, o_mla, o_hg, w["wout"], tm)
    y = _ffn(x1, sh2, sc2, gt2, w["w1"], w["w2"], w["gfin"], min(1024, t), 512)
    return y, ckv, kpe, s_fin


def kernel(x_prompt, x_sample, c_prompt, c_sample, cache_ckv, cache_kpe, state_hgrn, w_ada, b_ada, w_in, w_uq, g_q, g_kv, w_uk, w_uv, hg_lower_bounds, g_hgrn, w_out, w_ff1, w_ff2, g_final):
    depth = w_ada.shape[0]
    assert depth == 1, "single-layer step"
    bp, sp, d = x_prompt.shape
    bs, ss, _ = x_sample.shape
    past = cache_ckv.shape[2]

    lb_all = jnp.cumsum(jax.nn.softmax(hg_lower_bounds.astype(F32), axis=0), axis=0)
    w = _prep_weights(w_in[0], w_uq[0], g_q[0], g_kv[0], w_uk[0], w_uv[0], lb_all[0], g_hgrn[0],
                      w_out[0], w_ff1[0], w_ff2[0], g_final)

    c_all = jnp.concatenate([c_prompt, c_sample], axis=0)
    n_c = c_all.shape[0]
    c_pad = jnp.pad(c_all, ((0, -n_c % 8), (0, 0)))
    mod = _modulation(c_pad, w_ada[0], b_ada[0])[:n_c]
    mod_p = [m.reshape(bp, 1, d) for m in jnp.split(mod[:bp], 6, axis=-1)]
    mod_s = [jnp.repeat(m, ss, axis=0).reshape(1, bs * ss, d) for m in jnp.split(mod[bp:], 6, axis=-1)]

    cos_p, sin_p = _rope_tables(jnp.arange(sp))
    cos_s, sin_s = _rope_tables(jnp.tile(past + jnp.arange(ss), bs))

    yp, ckv_p, kpe_p, st_p = _layer(x_prompt.reshape(bp * sp, d), mod_p, cos_p, sin_p, w, bp, sp, None)
    ys, ckv_s, kpe_s, st_s = _layer(x_sample.reshape(bs * ss, d), mod_s, cos_s, sin_s, w, bs, ss,
                                    (cache_ckv[0], cache_kpe[0], state_hgrn[0]))
    return (yp.reshape(bp, sp, d), ys.reshape(bs, ss, d),
            ckv_p.reshape(1, bp, sp, KV_RANK), kpe_p.reshape(1, bp, sp, QK_ROPE), st_p[None],
            ckv_s.reshape(1, bs, ss, KV_RANK), kpe_s.reshape(1, bs, ss, QK_ROPE), st_s[None])
```

```python
import functools
import math

import jax
import jax.numpy as jnp
from jax import lax
from jax.experimental import pallas as pl
from jax.experimental.pallas import tpu as pltpu

F32 = jnp.float32
BF16 = jnp.bfloat16

EPS = 1e-6
CHUNK = 64
N_HEADS = 8
QK_NOPE = 128
QK_ROPE = 64
V_HEAD = 128
Q_RANK = 512
KV_RANK = 512
ROPE_BASE = 10000.0
ATTN_SCALE = (QK_NOPE + QK_ROPE) ** -0.5
HG_HEADS = 8
HG_DIM = 128
HG_WIDTH = HG_HEADS * HG_DIM
HEAD_PACK = 2 * QK_NOPE
LANES = 128
VMEM_LIMIT = 56 * 1024 * 1024

_NT = (((1,), (1,)), ((), ()))
_TN = (((0,), (0,)), ((), ()))


def _rms(x):
    return x * lax.rsqrt(jnp.mean(x * x, axis=-1, keepdims=True) + EPS)


def _dot(a, b, dims=None):
    if dims is None:
        return jnp.dot(a, b, preferred_element_type=F32)
    return lax.dot_general(a, b, dims, preferred_element_type=F32)


def _resident(shape):
    return pl.BlockSpec(shape, lambda *_: (0,) * len(shape), pipeline_mode=pl.Buffered(1))


def _params(*sem):
    return pltpu.CompilerParams(dimension_semantics=sem, vmem_limit_bytes=VMEM_LIMIT)


def _mod_kernel(c_ref, w_ref, b_ref, o_ref):
    c = c_ref[...]
    a = (c * jax.nn.sigmoid(c)).astype(BF16)
    o_ref[...] = _dot(a, w_ref[...].astype(BF16)) + b_ref[...]


def _modulation(c_all, w_ada, b_ada):
    m, d = c_all.shape
    n = w_ada.shape[1]
    tn = 1024
    return pl.pallas_call(
        _mod_kernel,
        out_shape=jax.ShapeDtypeStruct((m, n), F32),
        grid=(n // tn,),
        in_specs=[pl.BlockSpec((m, d), lambda j: (0, 0)),
                  pl.BlockSpec((d, tn), lambda j: (0, j)),
                  pl.BlockSpec((1, tn), lambda j: (0, j))],
        out_specs=pl.BlockSpec((m, tn), lambda j: (0, j)),
        compiler_params=_params("arbitrary"),
        name="mod",
    )(c_all, w_ada, b_ada.reshape(1, n))


def _mla_in_kernel(x_ref, sh_ref, sc_ref, wa_ref, wkpe_ref, wuq_ref, wukv_ref,
                   gq_ref, gkv_ref, q_ref, ckv_ref, kpe_ref, *kv_refs, pos_base, pos_period):
    x = x_ref[...]
    tm = x.shape[0]
    h = (_rms(x) * (1.0 + sc_ref[...]) + sh_ref[...]).astype(BF16)

    half = QK_ROPE // 2
    lane = lax.broadcasted_iota(jnp.int32, (tm, LANES), 1)
    tok = pl.program_id(0) * tm + lax.broadcasted_iota(jnp.int32, (tm, LANES), 0)
    pos = (pos_base + lax.rem(tok, pos_period)).astype(F32)
    inv_freq = jnp.exp(-math.log(ROPE_BASE) * lax.rem(lane, half).astype(F32) / half)
    ang = pos * inv_freq
    cos = jnp.where(lane < QK_ROPE, jnp.cos(ang), 0.0)
    sin = jnp.where(lane < QK_ROPE, jnp.sin(ang), 0.0)

    def rope(t):
        return t * cos + pltpu.roll(t, QK_ROPE, 1) * sin

    a = _dot(h, wa_ref[...])
    cqn = (_rms(a[:, :Q_RANK]) * gq_ref[...]).astype(BF16)
    ckvn = _rms(a[:, Q_RANK:]) * gkv_ref[...]
    ckv_ref[...] = ckvn

    q = _dot(cqn, wuq_ref[...])
    for hd in range(N_HEADS):
        lo = hd * HEAD_PACK
        q_ref[:, lo:lo + QK_NOPE] = q[:, lo:lo + QK_NOPE].astype(BF16)
        q_ref[:, lo + QK_NOPE:lo + HEAD_PACK] = rope(q[:, lo + QK_NOPE:lo + HEAD_PACK]).astype(BF16)

    kp = rope(_dot(h, wkpe_ref[...]))
    kpe_ref[...] = kp[:, :QK_ROPE]

    if kv_refs:
        kf_ref, v_ref = kv_refs
        kv = _dot(ckvn.astype(BF16), wukv_ref[...])
        kpb = kp.astype(BF16)
        for hd in range(N_HEADS):
            lo = hd * HEAD_PACK
            kf_ref[:, lo:lo + QK_NOPE] = kv[:, hd * QK_NOPE:(hd + 1) * QK_NOPE].astype(BF16)
            kf_ref[:, lo + QK_NOPE:lo + HEAD_PACK] = kpb
            vo = N_HEADS * QK_NOPE + hd * V_HEAD
            v_ref[:, lo:lo + V_HEAD] = kv[:, vo:vo + V_HEAD].astype(BF16)
            v_ref[:, lo + V_HEAD:lo + HEAD_PACK] = jnp.ones((kv.shape[0], HEAD_PACK - V_HEAD), BF16)


def _mla_in(x, sh, sc, w, tm, with_kv, pos_base, pos_period):
    t, d = x.shape
    n_tiles = t // tm
    tiles_per_group = n_tiles // sh.shape[0]
    r = sh.shape[1]
    row = lambda n: pl.BlockSpec((tm, n), lambda i: (i, 0))
    mod = pl.BlockSpec((None, r, d), lambda i: (i // tiles_per_group, 0, 0))
    packed = N_HEADS * HEAD_PACK
    out_shape = [jax.ShapeDtypeStruct((t, packed), BF16),
                 jax.ShapeDtypeStruct((t, KV_RANK), F32),
                 jax.ShapeDtypeStruct((t, QK_ROPE), F32)]
    out_specs = [row(packed), row(KV_RANK), row(QK_ROPE)]
    if with_kv:
        out_shape += [jax.ShapeDtypeStruct((t, packed), BF16),
                      jax.ShapeDtypeStruct((t, packed), BF16)]
        out_specs += [row(packed), row(packed)]
    return pl.pallas_call(
        functools.partial(_mla_in_kernel, pos_base=pos_base, pos_period=pos_period),
        out_shape=out_shape,
        grid=(n_tiles,),
        in_specs=[row(d), mod, mod,
                  _resident(w["wa"].shape), _resident(w["wkpe"].shape),
                  _resident(w["wuq"].shape), _resident(w["wukv"].shape),
                  _resident((1, Q_RANK)), _resident((1, KV_RANK))],
        out_specs=out_specs,
        compiler_params=_params("arbitrary"),
        name="mla_in",
    )(x, sh, sc, w["wa"], w["wkpe"], w["wuq"], w["wukv"], w["gq"], w["gkv"])


def _hg_in_kernel(x_ref, sh_ref, sc_ref, wh_ref, lb_ref, qh_ref, kh_ref, vh_ref, gate_ref, g_ref):
    x = x_ref[...]
    h = (_rms(x) * (1.0 + sc_ref[...]) + sh_ref[...]).astype(BF16)
    n = HG_WIDTH
    hq = _dot(h, wh_ref[:, 0:n])
    qh_ref[...] = (hq * jax.nn.sigmoid(hq) * (HG_DIM ** -0.5)).astype(BF16)
    hf = _dot(h, wh_ref[:, n:2 * n])
    lb = lb_ref[...]
    f = lb + (1.0 - lb) * jax.nn.sigmoid(hf)
    kh_ref[...] = (1.0 - f).astype(BF16)
    g_ref[...] = jnp.log(f)
    vh_ref[...] = _dot(h, wh_ref[:, 2 * n:3 * n]).astype(BF16)
    hg = _dot(h, wh_ref[:, 3 * n:4 * n])
    gate_ref[...] = (hg * jax.nn.sigmoid(hg)).astype(BF16)


def _hg_in(x, sh, sc, w, tm):
    t, d = x.shape
    n_tiles = t // tm
    tiles_per_group = n_tiles // sh.shape[0]
    r = sh.shape[1]
    row = lambda n: pl.BlockSpec((tm, n), lambda i: (i, 0))
    mod = pl.BlockSpec((None, r, d), lambda i: (i // tiles_per_group, 0, 0))
    half = jax.ShapeDtypeStruct((t, HG_WIDTH), BF16)
    return pl.pallas_call(
        _hg_in_kernel,
        out_shape=[half, half, half, half, jax.ShapeDtypeStruct((t, HG_WIDTH), F32)],
        grid=(n_tiles,),
        in_specs=[row(d), mod, mod, _resident(w["wh"].shape), _resident((1, HG_WIDTH))],
        out_specs=[row(HG_WIDTH)] * 5,
        compiler_params=_params("arbitrary"),
        name="hg_in",
    )(x, sh, sc, w["wh"], w["lb"])


def _attn_kernel(q_ref, k_ref, v_ref, o_ref, s0_ref, s1_ref, mx0_ref, mx1_ref, m_ref, acc_ref, *, tk):
    qi = pl.program_id(2)
    qs = [q_ref[0:tk, :], q_ref[tk:2 * tk, :]]
    diag_ok = (lax.broadcasted_iota(jnp.int32, (tk, tk), 1) // CHUNK
               <= lax.broadcasted_iota(jnp.int32, (tk, tk), 0) // CHUNK)

    def kv_rows(blk):
        return pl.ds(pl.multiple_of(blk * tk, tk), tk)

    def scores(r, blk):
        return _dot(qs[r], k_ref[kv_rows(blk), :], _NT)

    def row_max(s):
        return jnp.broadcast_to(jnp.max(s, axis=-1, keepdims=True), (tk, LANES))

    def produce(r, blk, s_ref, mx_ref):
        s = scores(r, blk)
        s_ref[r] = s
        mx_ref[r] = row_max(s)

    def update(r, s, mx, blk):
        m = m_ref[r]
        m_new = jnp.maximum(m, mx)
        m_ref[r] = m_new
        p = jnp.exp2(s - jnp.concatenate([m_new] * (tk // LANES), axis=1)).astype(BF16)
        alpha = jnp.exp2(m - m_new)
        acc_ref[r] = (jnp.concatenate([alpha] * (HEAD_PACK // LANES), axis=1) * acc_ref[r]
                      + _dot(p, v_ref[kv_rows(blk), :]))

    def update_diag(r, s, blk):
        s = jnp.where(diag_ok, s, -1e30)
        update(r, s, row_max(s), blk)

    for r in range(2):
        m_ref[r] = jnp.full((tk, LANES), -1e30, F32)
        acc_ref[r] = jnp.zeros((tk, HEAD_PACK), F32)
        produce(r, 0, s0_ref, mx0_ref)

    def body(j, carry):
        blk = 2 * j
        for (cur, cmx), (nxt, nmx) in (((s0_ref, mx0_ref), (s1_ref, mx1_ref)),
                                       ((s1_ref, mx1_ref), (s0_ref, mx0_ref))):
            for r in range(2):
                produce(r, blk + 1, nxt, nmx)
            for r in range(2):
                update(r, cur[r], cmx[r], blk)
            blk = blk + 1
        return carry

    lax.fori_loop(0, qi, body, 0)
    s_last = scores(1, 2 * qi + 1)
    update_diag(0, s0_ref[0], 2 * qi)
    update(1, s0_ref[1], mx0_ref[1], 2 * qi)
    update_diag(1, s_last, 2 * qi + 1)
    for r in range(2):
        acc = acc_ref[r]
        o_ref[r * tk:(r + 1) * tk, :] = (acc[:, :V_HEAD] / acc[:, V_HEAD:]).astype(BF16)


def _attention(q, kf, vp, batch, seq, tk):
    tq = 2 * tk
    nq = seq // tq
    return pl.pallas_call(
        functools.partial(_attn_kernel, tk=tk),
        out_shape=jax.ShapeDtypeStruct((batch * seq, N_HEADS * V_HEAD), BF16),
        grid=(batch, N_HEADS, nq),
        in_specs=[pl.BlockSpec((tq, HEAD_PACK), lambda b, h, i: (b * nq + i, h)),
                  pl.BlockSpec((seq, HEAD_PACK), lambda b, h, i: (b, h)),
                  pl.BlockSpec((seq, HEAD_PACK), lambda b, h, i: (b, h))],
        out_specs=pl.BlockSpec((tq, V_HEAD), lambda b, h, i: (b * nq + i, h)),
        scratch_shapes=[pltpu.VMEM((2, tk, tk), F32), pltpu.VMEM((2, tk, tk), F32),
                        pltpu.VMEM((2, tk, LANES), F32), pltpu.VMEM((2, tk, LANES), F32),
                        pltpu.VMEM((2, tk, LANES), F32), pltpu.VMEM((2, tk, HEAD_PACK), F32)],
        compiler_params=_params("arbitrary", "arbitrary", "arbitrary"),
        name="attn",
    )(q, kf, vp)


def _qlat_kernel(qn_ref, qp_ref, wuk_ref, ql_ref, qpe_ref):
    nb = ql_ref.shape[0]
    ql = _dot(qn_ref[...], wuk_ref[...], _NT)
    ql_ref[...] = ql.reshape(nb, -1, KV_RANK)
    qpe_ref[...] = qp_ref[...].astype(F32).reshape(nb, -1, LANES)


def _q_latent(q, wuk_t, nb, ln):
    t = q.shape[0]
    return pl.pallas_call(
        _qlat_kernel,
        out_shape=[jax.ShapeDtypeStruct((nb, N_HEADS, ln, KV_RANK), F32),
                   jax.ShapeDtypeStruct((nb, N_HEADS, ln, LANES), F32)],
        grid=(N_HEADS,),
        in_specs=[pl.BlockSpec((t, QK_NOPE), lambda h: (0, 2 * h)),
                  pl.BlockSpec((t, QK_NOPE), lambda h: (0, 2 * h + 1)),
                  pl.BlockSpec((None, KV_RANK, QK_NOPE), lambda h: (h, 0, 0))],
        out_specs=[pl.BlockSpec((nb, None, ln, KV_RANK), lambda h: (0, h, 0, 0)),
                   pl.BlockSpec((nb, None, ln, LANES), lambda h: (0, h, 0, 0))],
        compiler_params=_params("arbitrary"),
        name="q_latent",
    )(q, q, wuk_t)


def _dec_attn_kernel(ql_ref, qpe_ref, cckv_ref, ckpe_ref, nckv_ref, nkpe_ref, wuv_ref, o_ref, *, ln):
    rows = N_HEADS * ln
    ql = ql_ref[...].reshape(rows, KV_RANK).astype(BF16)
    qp = qpe_ref[...].reshape(rows, LANES)[:, :QK_ROPE].astype(BF16)
    ckc = cckv_ref[...].astype(BF16)
    kpc = ckpe_ref[...].astype(BF16)
    ckn = nckv_ref[...].astype(BF16)
    kpn = nkpe_ref[...].astype(BF16)
    s_c = _dot(ql, ckc, _NT) + _dot(qp, kpc, _NT)
    s_n = _dot(ql, ckn, _NT) + _dot(qp, kpn, _NT)
    m = jnp.maximum(jnp.max(s_c, axis=-1, keepdims=True), jnp.max(s_n, axis=-1, keepdims=True))
    p_c = jnp.exp2(s_c - m)
    p_n = jnp.exp2(s_n - m)
    l = jnp.sum(p_c, axis=-1, keepdims=True) + jnp.sum(p_n, axis=-1, keepdims=True)
    ctx = (_dot(p_c.astype(BF16), ckc) + _dot(p_n.astype(BF16), ckn)) / l
    ctx = ctx.astype(BF16)
    for hd in range(N_HEADS):
        o_ref[:, hd * V_HEAD:(hd + 1) * V_HEAD] = _dot(
            ctx[hd * ln:(hd + 1) * ln], wuv_ref[hd]).astype(BF16)


def _dec_attention(ql, qpe, cache_ckv, cache_kpe, ckv_new, kpe_new, wuv_t, ln):
    nb, past = cache_ckv.shape[0], cache_ckv.shape[1]
    return pl.pallas_call(
        functools.partial(_dec_attn_kernel, ln=ln),
        out_shape=jax.ShapeDtypeStruct((nb * ln, N_HEADS * V_HEAD), BF16),
        grid=(nb,),
        in_specs=[pl.BlockSpec((None, N_HEADS, ln, KV_RANK), lambda b: (b, 0, 0, 0)),
                  pl.BlockSpec((None, N_HEADS, ln, LANES), lambda b: (b, 0, 0, 0)),
                  pl.BlockSpec((None, past, KV_RANK), lambda b: (b, 0, 0)),
                  pl.BlockSpec((None, past, QK_ROPE), lambda b: (b, 0, 0)),
                  pl.BlockSpec((ln, KV_RANK), lambda b: (b, 0)),
                  pl.BlockSpec((ln, QK_ROPE), lambda b: (b, 0)),
                  _resident(wuv_t.shape)],
        out_specs=pl.BlockSpec((ln, N_HEADS * V_HEAD), lambda b: (b, 0)),
        compiler_params=_params("arbitrary"),
        name="dec_attn",
    )(ql, qpe, cache_ckv, cache_kpe, ckv_new, kpe_new, wuv_t)


def _gla_consts(ln, width):
    row = lax.broadcasted_iota(jnp.int32, (8, width), 0)
    t_idx = lax.broadcasted_iota(jnp.int32, (ln, ln), 0)
    s_idx = lax.broadcasted_iota(jnp.int32, (ln, ln), 1)
    level_mask = []
    for lev in range(int(math.log2(ln))):
        level_mask.append((((t_idx ^ s_idx) >> (lev + 1)) == 0)
                          & ((t_idx & (1 << lev)) != 0) & ((s_idx & (1 << lev)) == 0))
    return {
        "scan": [row >= sft for sft in (1, 2, 4)],
        "odd": (row & 1).astype(F32),
        "low4": (row & 4) == 0,
        "sign1": jnp.where((row & 2) != 0, 1.0, -1.0).astype(F32),
        "sign2": jnp.where((row & 4) != 0, 1.0, -1.0).astype(F32),
        "level_mask": level_mask,
    }


def _gla_chunk(qb, kb, vb, g, st_ref, loc_ref, cst, ln):
    nv = ln // 8
    width = g.shape[1]
    grp = lambda x: [x[8 * i:8 * i + 8, :] for i in range(nv)]
    cat = lambda xs: jnp.concatenate(xs, axis=0)
    head = lambda x, hd: x[:, hd * HG_DIM:(hd + 1) * HG_DIM]
    q, k = grp(qb.astype(F32)), grp(kb.astype(F32))
    gs = grp(g)

    loc = gs
    for sft, keep in zip((1, 2, 4), cst["scan"]):
        loc = [x + jnp.where(keep, pltpu.roll(x, sft, 0), 0.0) for x in loc]
    loc_ref[...] = cat(loc)
    bcast = lambda r: jnp.broadcast_to(loc_ref[r:r + 1, :], (8, width))
    tot = [bcast(8 * i + 7) for i in range(nv)]
    pin = [tot[0]]
    for i in range(1, nv):
        pin.append(pin[-1] + tot[i])
    b = [loc[0]] + [loc[i] + pin[i - 1] for i in range(1, nv)]
    b_last = pin[-1]

    exps = [[gs[i] * cst["odd"] for i in range(nv)],
            [(loc[i] - jnp.where(cst["low4"], bcast(8 * i + 1), bcast(8 * i + 5))) * cst["sign1"]
             for i in range(nv)],
            [(loc[i] - bcast(8 * i + 3)) * cst["sign2"] for i in range(nv)]]
    upper = [[True] * nv, [True] * nv, [True] * nv]
    lower = [[True] * nv, [True] * nv, [True] * nv]
    for lev in range(3, int(math.log2(ln))):
        m = 1 << (lev - 3)
        e, up, lo = [], [], []
        for i in range(nv):
            mid = (i // (2 * m)) * 2 * m + m - 1
            is_up = (i // m) % 2 == 1
            if is_up:
                e.append(loc[i] if mid == i - 1 else b[i] - pin[mid])
            else:
                e.append(tot[i] - loc[i] if mid == i else pin[mid] - b[i])
            up.append(is_up)
            lo.append(not is_up)
        exps.append(e)
        upper.append(up)
        lower.append(lo)

    a = [jnp.zeros((ln, ln), F32)] * HG_HEADS
    for lev, e in enumerate(exps):
        w = [jnp.exp(x) for x in e]
        qt = cat([q[i] * w[i] if upper[lev][i] else q[i] for i in range(nv)]).astype(BF16)
        kt = cat([k[i] * w[i] if lower[lev][i] else k[i] for i in range(nv)]).astype(BF16)
        a = [jnp.where(cst["level_mask"][lev], _dot(head(qt, hd), head(kt, hd), _NT), a[hd])
             for hd in range(HG_HEADS)]

    qd = cat([q[i] * jnp.exp(b[i]) for i in range(nv)]).astype(BF16)
    kd = cat([k[i] * jnp.exp(b_last - b[i]) for i in range(nv)]).astype(BF16)
    qk = cat([q[i] * k[i] for i in range(nv)])
    vf = vb.astype(F32)
    decay = jnp.exp(b_last[0:1, :])
    outs = []
    for hd in range(HG_HEADS):
        st = st_ref[hd]
        v_h = head(vb, hd)
        o = _dot(head(qd, hd), st.astype(BF16), _NT) + _dot(a[hd].astype(BF16), v_h)
        outs.append(o + jnp.sum(head(qk, hd), axis=-1, keepdims=True) * head(vf, hd))
        st_ref[hd] = st * head(decay, hd) + _dot(v_h, head(kd, hd), _TN)
    return outs


def _hgrn_kernel(*refs, ln, n_chunks, has_init):
    if has_init:
        qh_ref, kh_ref, vh_ref, gate_ref, g_ref, gw_ref, s0_ref, o_ref, sout_ref, st_ref, b_ref = refs
    else:
        qh_ref, kh_ref, vh_ref, gate_ref, g_ref, gw_ref, o_ref, sout_ref, st_ref, b_ref = refs
        s0_ref = None
    si = pl.program_id(1)

    @pl.when(si == 0)
    def _():
        for hd in range(HG_HEADS):
            st_ref[hd] = s0_ref[hd].T if has_init else jnp.zeros((HG_DIM, HG_DIM), F32)

    gw = gw_ref[...]
    cst = _gla_consts(ln, HG_WIDTH)

    def body(c, carry):
        rows = pl.ds(pl.multiple_of(c * ln, ln), ln)
        outs = _gla_chunk(qh_ref[rows, :], kh_ref[rows, :], vh_ref[rows, :], g_ref[rows, :],
                          st_ref, b_ref, cst, ln)
        for hd in range(HG_HEADS):
            lanes = slice(hd * HG_DIM, (hd + 1) * HG_DIM)
            o_ref[rows, lanes] = (_rms(outs[hd]) * gw * gate_ref[rows, lanes].astype(F32)).astype(BF16)
        return carry

    lax.fori_loop(0, n_chunks, body, 0, unroll=2 if n_chunks % 2 == 0 else 1)

    @pl.when(si == pl.num_programs(1) - 1)
    def _():
        for hd in range(HG_HEADS):
            sout_ref[hd] = st_ref[hd].T


def _hgrn(qh, kh, vh, gate, g, gw, state0, batch, seq, ln, ts):
    ns = seq // ts
    has_init = state0 is not None
    row = pl.BlockSpec((ts, HG_WIDTH), lambda b, s: (b * ns + s, 0))
    st_spec = pl.BlockSpec((None, HG_HEADS, HG_DIM, HG_DIM), lambda b, s: (b, 0, 0, 0))
    in_specs = [row] * 5 + [_resident((1, HG_DIM))]
    args = [qh, kh, vh, gate, g, gw]
    if has_init:
        in_specs.append(st_spec)
        args.append(state0)
    return pl.pallas_call(
        functools.partial(_hgrn_kernel, ln=ln, n_chunks=ts // ln, has_init=has_init),
        out_shape=[jax.ShapeDtypeStruct((batch * seq, HG_WIDTH), BF16),
                   jax.ShapeDtypeStruct((batch, HG_HEADS, HG_DIM, HG_DIM), F32)],
        grid=(batch, ns),
        in_specs=in_specs,
        out_specs=[row, st_spec],
        scratch_shapes=[pltpu.VMEM((HG_HEADS, HG_DIM, HG_DIM), F32),
                        pltpu.VMEM((ln, HG_WIDTH), F32)],
        compiler_params=_params("arbitrary", "arbitrary"),
        name="hgrn",
    )(*args)


def _out_proj_kernel(x_ref, gt_ref, om_ref, oh_ref, w_ref, o_ref):
    n = om_ref.shape[1]
    mix = _dot(om_ref[...], w_ref[0:n, :]) + _dot(oh_ref[...], w_ref[n:, :])
    o_ref[...] = x_ref[...] + gt_ref[...] * mix


def _out_proj(x, gt, o_mla, o_hg, w_out, tm):
    t, d = x.shape
    n_tiles = t // tm
    tiles_per_group = n_tiles // gt.shape[0]
    r = gt.shape[1]
    row = lambda n: pl.BlockSpec((tm, n), lambda i: (i, 0))
    return pl.pallas_call(
        _out_proj_kernel,
        out_shape=jax.ShapeDtypeStruct((t, d), F32),
        grid=(n_tiles,),
        in_specs=[row(d), pl.BlockSpec((None, r, d), lambda i: (i // tiles_per_group, 0, 0)),
                  row(o_mla.shape[1]), row(o_hg.shape[1]), _resident(w_out.shape)],
        out_specs=row(d),
        compiler_params=_params("arbitrary"),
        name="out_proj",
    )(x, gt, o_mla, o_hg, w_out)


def _ffn_kernel(x_ref, sh_ref, sc_ref, gt_ref, w1_ref, w2_ref, gf_ref, o_ref, h_ref):
    f = pl.program_id(1)

    @pl.when(f == 0)
    def _():
        h_ref[...] = (_rms(x_ref[...]) * (1.0 + sc_ref[...]) + sh_ref[...]).astype(BF16)
        o_ref[...] = jnp.zeros_like(o_ref)

    a = jnp.maximum(_dot(h_ref[...], w1_ref[...]), 0.0)
    o_ref[...] += _dot((a * a).astype(BF16), w2_ref[...])

    @pl.when(f == pl.num_programs(1) - 1)
    def _():
        x2 = x_ref[...] + gt_ref[...] * o_ref[...]
        o_ref[...] = _rms(x2) * gf_ref[...]


def _ffn(x, sh, sc, gt, w1, w2, gf, tm, tf):
    t, d = x.shape
    dff = w1.shape[1]
    n_tiles = t // tm
    tiles_per_group = n_tiles // sh.shape[0]
    r = sh.shape[1]
    row = pl.BlockSpec((tm, d), lambda i, f: (i, 0))
    mod = pl.BlockSpec((None, r, d), lambda i, f: (i // tiles_per_group, 0, 0))
    return pl.pallas_call(
        _ffn_kernel,
        out_shape=jax.ShapeDtypeStruct((t, d), F32),
        grid=(n_tiles, dff // tf),
        in_specs=[row, mod, mod, mod,
                  pl.BlockSpec((d, tf), lambda i, f: (0, f)),
                  pl.BlockSpec((tf, d), lambda i, f: (f, 0)),
                  pl.BlockSpec((1, d), lambda i, f: (0, 0))],
        out_specs=row,
        scratch_shapes=[pltpu.VMEM((tm, d), BF16)],
        compiler_params=_params("arbitrary", "arbitrary"),
        name="ffn",
    )(x, sh, sc, gt, w1, w2, gf)


def _rot_half_cols(w):
    half = w.shape[-1] // 2
    return jnp.concatenate([-w[..., half:], w[..., :half]], axis=-1)


def _prep_weights(w_in, w_uq, g_q, g_kv, w_uk, w_uv, lb, g_hgrn, w_out, w_ff1, w_ff2, g_final):
    d = w_in.shape[0]
    o_kpe = Q_RANK + KV_RANK
    o_hg = o_kpe + QK_ROPE
    w_kpe = w_in[:, o_kpe:o_hg]
    q_scale = ATTN_SCALE * math.log2(math.e)
    wq = w_uq.reshape(Q_RANK, N_HEADS, QK_NOPE + QK_ROPE) * q_scale
    wq_pe = wq[..., QK_NOPE:]
    wuq = jnp.concatenate([wq[..., :QK_NOPE], wq_pe, _rot_half_cols(wq_pe)], axis=-1)
    return {
        "wa": w_in[:, :o_kpe].astype(BF16),
        "wkpe": jnp.concatenate([w_kpe, _rot_half_cols(w_kpe)], axis=-1).astype(BF16),
        "wh": w_in[:, o_hg:].astype(BF16),
        "wuq": wuq.reshape(Q_RANK, N_HEADS * HEAD_PACK).astype(BF16),
        "wukv": jnp.concatenate([w_uk.reshape(KV_RANK, -1), w_uv.reshape(KV_RANK, -1)], axis=-1).astype(BF16),
        "wuk_t": jnp.transpose(w_uk, (1, 0, 2)).astype(BF16),
        "wuv_t": jnp.transpose(w_uv, (1, 0, 2)).astype(BF16),
        "gq": g_q.reshape(1, -1), "gkv": g_kv.reshape(1, -1),
        "lb": lb.reshape(1, -1), "ghg": g_hgrn.reshape(1, -1),
        "wout": w_out.astype(BF16), "w1": w_ff1.astype(BF16), "w2": w_ff2.astype(BF16),
        "gfin": g_final.reshape(1, d),
    }


def _layer(x, mods, w, batch, seq, cache, past):
    sh1, sc1, gt1, sh2, sc2, gt2 = mods
    t = x.shape[0]
    tm = min(512, t)
    prompt = cache is None
    if prompt:
        q, ckv, kpe, kf, v = _mla_in(x, sh1, sc1, w, tm, True, past, seq)
        o_mla = _attention(q, kf, v, batch, seq, 512)
    else:
        cache_ckv, cache_kpe, state0 = cache
        q, ckv, kpe = _mla_in(x, sh1, sc1, w, tm, False, past, seq)
        ql, qpe = _q_latent(q, w["wuk_t"], batch, seq)
        o_mla = _dec_attention(ql, qpe, cache_ckv, cache_kpe, ckv, kpe, w["wuv_t"], seq)
    qh, kh, vh, gate, g = _hg_in(x, sh1, sc1, w, tm)
    if prompt:
        o_hg, s_fin = _hgrn(qh, kh, vh, gate, g, w["ghg"], None, batch, seq, CHUNK, 512)
    else:
        o_hg, s_fin = _hgrn(qh, kh, vh, gate, g, w["ghg"], state0, batch, seq, seq, seq)
    x1 = _out_proj(x, gt1, o_mla, o_hg, w["wout"], tm)
    y = _ffn(x1, sh2, sc2, gt2, w["w1"], w["w2"], w["gfin"], min(1024, t), 512)
    return y, ckv, kpe, s_fin


def kernel(x_prompt, x_sample, c_prompt, c_sample, cache_ckv, cache_kpe, state_hgrn, w_ada, b_ada, w_in, w_uq, g_q, g_kv, w_uk, w_uv, hg_lower_bounds, g_hgrn, w_out, w_ff1, w_ff2, g_final):
    depth = w_ada.shape[0]
    assert depth == 1, "single-layer step"
    bp, sp, d = x_prompt.shape
    bs, ss, _ = x_sample.shape
    past = cache_ckv.shape[2]

    lb_all = jnp.cumsum(jax.nn.softmax(hg_lower_bounds.astype(F32), axis=0), axis=0)
    w = _prep_weights(w_in[0], w_uq[0], g_q[0], g_kv[0], w_uk[0], w_uv[0], lb_all[0], g_hgrn[0],
                      w_out[0], w_ff1[0], w_ff2[0], g_final)

    c_all = jnp.concatenate([c_prompt, c_sample], axis=0)
    n_c = c_all.shape[0]
    c_pad = jnp.pad(c_all, ((0, -n_c % 8), (0, 0)))
    mod = _modulation(c_pad, w_ada[0], b_ada[0])[:n_c]
    mod_p = [m.reshape(bp, 1, d) for m in jnp.split(mod[:bp], 6, axis=-1)]
    mod_s = [jnp.repeat(m, ss, axis=0).reshape(1, bs * ss, d) for m in jnp.split(mod[bp:], 6, axis=-1)]

    yp, ckv_p, kpe_p, st_p = _layer(x_prompt.reshape(bp * sp, d), mod_p, w, bp, sp, None, 0)
    ys, ckv_s, kpe_s, st_s = _layer(x_sample.reshape(bs * ss, d), mod_s, w, bs, ss,
                                    (cache_ckv[0], cache_kpe[0], state_hgrn[0]), past)
    return (yp.reshape(bp, sp, d), ys.reshape(bs, ss, d),
            ckv_p.reshape(1, bp, sp, KV_RANK), kpe_p.reshape(1, bp, sp, QK_ROPE), st_p[None],
            ckv_s.reshape(1, bs, ss, KV_RANK), kpe_s.reshape(1, bs, ss, QK_ROPE), st_s[None])
```

```python
import functools
import math

import jax
import jax.numpy as jnp
from jax import lax
from jax.experimental import pallas as pl
from jax.experimental.pallas import tpu as pltpu

F32 = jnp.float32
BF16 = jnp.bfloat16

EPS = 1e-6
CHUNK = 64
N_HEADS = 8
QK_NOPE = 128
QK_ROPE = 64
V_HEAD = 128
Q_RANK = 512
KV_RANK = 512
ROPE_BASE = 10000.0
ATTN_SCALE = (QK_NOPE + QK_ROPE) ** -0.5
HG_HEADS = 8
HG_DIM = 128
HG_WIDTH = HG_HEADS * HG_DIM
HEAD_PACK = 2 * QK_NOPE
LANES = 128
VMEM_LIMIT = 56 * 1024 * 1024

_NT = (((1,), (1,)), ((), ()))
_TN = (((0,), (0,)), ((), ()))


def _rms(x):
    return x * lax.rsqrt(jnp.mean(x * x, axis=-1, keepdims=True) + EPS)


def _dot(a, b, dims=None):
    if dims is None:
        return jnp.dot(a, b, preferred_element_type=F32)
    return lax.dot_general(a, b, dims, preferred_element_type=F32)


def _resident(shape):
    return pl.BlockSpec(shape, lambda *_: (0,) * len(shape), pipeline_mode=pl.Buffered(1))


def _params(*sem):
    return pltpu.CompilerParams(dimension_semantics=sem, vmem_limit_bytes=VMEM_LIMIT)


def _mod_kernel(c_ref, w_ref, b_ref, o_ref):
    c = c_ref[...]
    a = (c * jax.nn.sigmoid(c)).astype(BF16)
    o_ref[...] = _dot(a, w_ref[...].astype(BF16)) + b_ref[...]


def _modulation(c_all, w_ada, b_ada):
    m, d = c_all.shape
    n = w_ada.shape[1]
    tn = 1024
    return pl.pallas_call(
        _mod_kernel,
        out_shape=jax.ShapeDtypeStruct((m, n), F32),
        grid=(n // tn,),
        in_specs=[pl.BlockSpec((m, d), lambda j: (0, 0)),
                  pl.BlockSpec((d, tn), lambda j: (0, j)),
                  pl.BlockSpec((1, tn), lambda j: (0, j))],
        out_specs=pl.BlockSpec((m, tn), lambda j: (0, j)),
        compiler_params=_params("arbitrary"),
        name="mod",
    )(c_all, w_ada, b_ada.reshape(1, n))


def _mla_in_kernel(x_ref, sh_ref, sc_ref, cr_ref, sr_ref, wa_ref, wkpe_ref, wuq_ref, wukv_ref,
                   gq_ref, gkv_ref, q_ref, ckv_ref, kpe_ref, *kv_refs, pos_base, pos_period):
    x = x_ref[...]
    tm = x.shape[0]
    h = (_rms(x) * (1.0 + sc_ref[...]) + sh_ref[...]).astype(BF16)

    half = QK_ROPE // 2
    lane = lax.broadcasted_iota(jnp.int32, (8, LANES), 1)
    base = (pos_base + lax.rem(pl.program_id(0) * tm, pos_period)).astype(F32)
    ang_b = base * jnp.exp(-math.log(ROPE_BASE) * lax.rem(lane, half).astype(F32) / half)
    cb, sb = jnp.cos(ang_b)[0:1], jnp.sin(ang_b)[0:1]
    cr, sr = cr_ref[...], sr_ref[...]
    cos = cr * cb - sr * sb
    sin = sr * cb + cr * sb

    def rope(t):
        return t * cos + pltpu.roll(t, QK_ROPE, 1) * sin

    a = _dot(h, wa_ref[...], _NT)
    cqn = (_rms(a[:, :Q_RANK]) * gq_ref[...]).astype(BF16)
    ckvn = _rms(a[:, Q_RANK:]) * gkv_ref[...]
    ckv_ref[...] = ckvn

    q = _dot(cqn, wuq_ref[...])
    for hd in range(N_HEADS):
        lo = hd * HEAD_PACK
        q_ref[:, lo:lo + QK_NOPE] = q[:, lo:lo + QK_NOPE].astype(BF16)
        q_ref[:, lo + QK_NOPE:lo + HEAD_PACK] = rope(q[:, lo + QK_NOPE:lo + HEAD_PACK]).astype(BF16)

    kp = rope(_dot(h, wkpe_ref[...], _NT))
    kpe_ref[...] = kp[:, :QK_ROPE]

    if kv_refs:
        kf_ref, v_ref = kv_refs
        kv = _dot(ckvn.astype(BF16), wukv_ref[...])
        kpb = kp.astype(BF16)
        for hd in range(N_HEADS):
            lo = hd * HEAD_PACK
            kf_ref[:, lo:lo + QK_NOPE] = kv[:, hd * QK_NOPE:(hd + 1) * QK_NOPE].astype(BF16)
            kf_ref[:, lo + QK_NOPE:lo + HEAD_PACK] = kpb
            vo = N_HEADS * QK_NOPE + hd * V_HEAD
            v_ref[:, lo:lo + V_HEAD] = kv[:, vo:vo + V_HEAD].astype(BF16)
            v_ref[:, lo + V_HEAD:lo + HEAD_PACK] = jnp.ones((kv.shape[0], HEAD_PACK - V_HEAD), BF16)


def _mla_in(x, sh, sc, w, tm, with_kv, pos_base, pos_period):
    t, d = x.shape
    n_tiles = t // tm
    tiles_per_group = n_tiles // sh.shape[0]
    r = sh.shape[1]
    row = lambda n: pl.BlockSpec((tm, n), lambda i: (i, 0))
    mod = pl.BlockSpec((None, r, d), lambda i: (i // tiles_per_group, 0, 0))
    packed = N_HEADS * HEAD_PACK
    half = QK_ROPE // 2
    lane = jnp.arange(LANES)
    inv_freq = jnp.exp(-math.log(ROPE_BASE) * (lane % half).astype(F32) / half)
    ang = (jnp.arange(tm) % min(tm, pos_period)).astype(F32)[:, None] * inv_freq[None, :]
    cr = jnp.where(lane < QK_ROPE, jnp.cos(ang), 0.0)
    sr = jnp.where(lane < QK_ROPE, jnp.sin(ang), 0.0)
    out_shape = [jax.ShapeDtypeStruct((t, packed), BF16),
                 jax.ShapeDtypeStruct((t, KV_RANK), F32),
                 jax.ShapeDtypeStruct((t, QK_ROPE), F32)]
    out_specs = [row(packed), row(KV_RANK), row(QK_ROPE)]
    if with_kv:
        out_shape += [jax.ShapeDtypeStruct((t, packed), BF16),
                      jax.ShapeDtypeStruct((t, packed), BF16)]
        out_specs += [row(packed), row(packed)]
    return pl.pallas_call(
        functools.partial(_mla_in_kernel, pos_base=pos_base, pos_period=pos_period),
        out_shape=out_shape,
        grid=(n_tiles,),
        in_specs=[row(d), mod, mod, _resident((tm, LANES)), _resident((tm, LANES)),
                  _resident(w["wa"].shape), _resident(w["wkpe"].shape),
                  _resident(w["wuq"].shape), _resident(w["wukv"].shape),
                  _resident((1, Q_RANK)), _resident((1, KV_RANK))],
        out_specs=out_specs,
        compiler_params=_params("arbitrary"),
        name="mla_in",
    )(x, sh, sc, cr, sr, w["wa"], w["wkpe"], w["wuq"], w["wukv"], w["gq"], w["gkv"])


def _hg_in_kernel(x_ref, sh_ref, sc_ref, wh_ref, lb_ref, qh_ref, kh_ref, vh_ref, gate_ref, g_ref):
    x = x_ref[...]
    h = (_rms(x) * (1.0 + sc_ref[...]) + sh_ref[...]).astype(BF16)
    n = HG_WIDTH
    hq = _dot(h, wh_ref[0:n, :], _NT)
    qh_ref[...] = (hq * jax.nn.sigmoid(hq) * (HG_DIM ** -0.5)).astype(BF16)
    hf = _dot(h, wh_ref[n:2 * n, :], _NT)
    lb = lb_ref[...]
    f = lb + (1.0 - lb) * jax.nn.sigmoid(hf)
    kh_ref[...] = (1.0 - f).astype(BF16)
    g_ref[...] = jnp.log(f)
    vh_ref[...] = _dot(h, wh_ref[2 * n:3 * n, :], _NT).astype(BF16)
    hg = _dot(h, wh_ref[3 * n:4 * n, :], _NT)
    gate_ref[...] = (hg * jax.nn.sigmoid(hg)).astype(BF16)


def _hg_in(x, sh, sc, w, tm):
    t, d = x.shape
    n_tiles = t // tm
    tiles_per_group = n_tiles // sh.shape[0]
    r = sh.shape[1]
    row = lambda n: pl.BlockSpec((tm, n), lambda i: (i, 0))
    mod = pl.BlockSpec((None, r, d), lambda i: (i // tiles_per_group, 0, 0))
    half = jax.ShapeDtypeStruct((t, HG_WIDTH), BF16)
    return pl.pallas_call(
        _hg_in_kernel,
        out_shape=[half, half, half, half, jax.ShapeDtypeStruct((t, HG_WIDTH), F32)],
        grid=(n_tiles,),
        in_specs=[row(d), mod, mod, _resident(w["wh"].shape), _resident((1, HG_WIDTH))],
        out_specs=[row(HG_WIDTH)] * 5,
        compiler_params=_params("arbitrary"),
        name="hg_in",
    )(x, sh, sc, w["wh"], w["lb"])


def _attn_kernel(q_ref, qn_ref, k_ref, v_ref, o_ref, s0_ref, s1_ref, mx0_ref, mx1_ref, m_ref, acc_ref,
                 *, tk):
    qi = pl.program_id(2)
    diag_ok = (lax.broadcasted_iota(jnp.int32, (tk, tk), 1) // CHUNK
               <= lax.broadcasted_iota(jnp.int32, (tk, tk), 0) // CHUNK)

    def kv_rows(blk):
        return pl.ds(pl.multiple_of(blk * tk, tk), tk)

    def rows(r):
        return slice(r * tk, (r + 1) * tk)

    def lanes(x, n):
        return jnp.concatenate([x] * (n // LANES), axis=1)

    def row_max(s):
        return jnp.broadcast_to(jnp.max(s, axis=-1, keepdims=True), (tk, LANES))

    def produce(q, r, blk, s_ref, mx_ref):
        s = _dot(q[rows(r), :], k_ref[kv_rows(blk), :], _NT)
        s_ref[r] = s
        mx_ref[r] = row_max(s)

    def block_softmax(s, mx, blk):
        p = jnp.exp2(s - lanes(mx, tk)).astype(BF16)
        return mx, _dot(p, v_ref[kv_rows(blk), :])

    def merge(a, b):
        (ma, acc_a), (mb, acc_b) = a, b
        m = jnp.maximum(ma, mb)
        return m, (lanes(jnp.exp2(ma - m), HEAD_PACK) * acc_a
                   + lanes(jnp.exp2(mb - m), HEAD_PACK) * acc_b)

    def update(r, s, mx, blk):
        m_old = m_ref[r]
        m_new = jnp.maximum(m_old, mx)
        m_ref[r] = m_new
        p = jnp.exp2(s - lanes(m_new, tk)).astype(BF16)
        acc_ref[r] = (lanes(jnp.exp2(m_old - m_new), HEAD_PACK) * acc_ref[r]
                      + _dot(p, v_ref[kv_rows(blk), :]))

    @pl.when(qi == 0)
    def _():
        for r in range(2):
            produce(q_ref, r, 0, s0_ref, mx0_ref)

    for r in range(2):
        m_ref[r] = jnp.full((tk, LANES), -1e30, F32)
        acc_ref[r] = jnp.zeros((tk, HEAD_PACK), F32)

    def body(j, carry):
        blk = 2 * j
        for (cur, cmx), (nxt, nmx) in (((s0_ref, mx0_ref), (s1_ref, mx1_ref)),
                                       ((s1_ref, mx1_ref), (s0_ref, mx0_ref))):
            for r in range(2):
                produce(q_ref, r, blk + 1, nxt, nmx)
            for r in range(2):
                update(r, cur[r], cmx[r], blk)
            blk = blk + 1
        return carry

    lax.fori_loop(0, qi, body, 0)

    d0 = jnp.where(diag_ok, s0_ref[0], -1e30)
    d1 = jnp.where(diag_ok, _dot(q_ref[rows(1), :], k_ref[kv_rows(2 * qi + 1), :], _NT), -1e30)
    part_a = block_softmax(d0, row_max(d0), 2 * qi)
    part_b = block_softmax(s0_ref[1], mx0_ref[1], 2 * qi)
    part_c = block_softmax(d1, row_max(d1), 2 * qi + 1)
    finals = [merge((m_ref[0], acc_ref[0]), part_a),
              merge(merge((m_ref[1], acc_ref[1]), part_b), part_c)]
    for r in range(2):
        acc = finals[r][1]
        o_ref[rows(r), :] = (acc[:, :V_HEAD] / acc[:, V_HEAD:]).astype(BF16)
    for r in range(2):
        produce(qn_ref, r, 0, s0_ref, mx0_ref)


def _attention(q, kf, vp, batch, seq, tk):
    tq = 2 * tk
    nq = seq // tq
    return pl.pallas_call(
        functools.partial(_attn_kernel, tk=tk),
        out_shape=jax.ShapeDtypeStruct((batch * seq, N_HEADS * V_HEAD), BF16),
        grid=(batch, N_HEADS, nq),
        in_specs=[pl.BlockSpec((tq, HEAD_PACK), lambda b, h, i: (b * nq + i, h)),
                  pl.BlockSpec((tq, HEAD_PACK), lambda b, h, i: (b * nq + jnp.minimum(i + 1, nq - 1), h)),
                  pl.BlockSpec((seq, HEAD_PACK), lambda b, h, i: (b, h)),
                  pl.BlockSpec((seq, HEAD_PACK), lambda b, h, i: (b, h))],
        out_specs=pl.BlockSpec((tq, V_HEAD), lambda b, h, i: (b * nq + i, h)),
        scratch_shapes=[pltpu.VMEM((2, tk, tk), F32), pltpu.VMEM((2, tk, tk), F32),
                        pltpu.VMEM((2, tk, LANES), F32), pltpu.VMEM((2, tk, LANES), F32),
                        pltpu.VMEM((2, tk, LANES), F32), pltpu.VMEM((2, tk, HEAD_PACK), F32)],
        compiler_params=_params("arbitrary", "arbitrary", "arbitrary"),
        name="attn",
    )(q, q, kf, vp)


def _qlat_kernel(qn_ref, qp_ref, wuk_ref, ql_ref, qpe_ref):
    nb = ql_ref.shape[0]
    ql = _dot(qn_ref[...], wuk_ref[...], _NT)
    ql_ref[...] = ql.reshape(nb, -1, KV_RANK)
    qpe_ref[...] = qp_ref[...].astype(F32).reshape(nb, -1, LANES)


def _q_latent(q, wuk_t, nb, ln):
    t = q.shape[0]
    return pl.pallas_call(
        _qlat_kernel,
        out_shape=[jax.ShapeDtypeStruct((nb, N_HEADS, ln, KV_RANK), F32),
                   jax.ShapeDtypeStruct((nb, N_HEADS, ln, LANES), F32)],
        grid=(N_HEADS,),
        in_specs=[pl.BlockSpec((t, QK_NOPE), lambda h: (0, 2 * h)),
                  pl.BlockSpec((t, QK_NOPE), lambda h: (0, 2 * h + 1)),
                  pl.BlockSpec((None, KV_RANK, QK_NOPE), lambda h: (h, 0, 0))],
        out_specs=[pl.BlockSpec((nb, None, ln, KV_RANK), lambda h: (0, h, 0, 0)),
                   pl.BlockSpec((nb, None, ln, LANES), lambda h: (0, h, 0, 0))],
        compiler_params=_params("arbitrary"),
        name="q_latent",
    )(q, q, wuk_t)


def _dec_attn_kernel(ql_ref, qpe_ref, cckv_ref, ckpe_ref, nckv_ref, nkpe_ref, wuv_ref, o_ref, *, ln):
    rows = N_HEADS * ln
    ql = ql_ref[...].reshape(rows, KV_RANK).astype(BF16)
    qp = qpe_ref[...].reshape(rows, LANES)[:, :QK_ROPE].astype(BF16)
    ckc = cckv_ref[...].astype(BF16)
    kpc = ckpe_ref[...].astype(BF16)
    ckn = nckv_ref[...].astype(BF16)
    kpn = nkpe_ref[...].astype(BF16)
    s_c = _dot(ql, ckc, _NT) + _dot(qp, kpc)
    s_n = _dot(ql, ckn, _NT) + _dot(qp, kpn, _NT)
    m = jnp.maximum(jnp.max(s_c, axis=-1, keepdims=True), jnp.max(s_n, axis=-1, keepdims=True))
    p_c = jnp.exp2(s_c - m)
    p_n = jnp.exp2(s_n - m)
    l = jnp.sum(p_c, axis=-1, keepdims=True) + jnp.sum(p_n, axis=-1, keepdims=True)
    ctx = (_dot(p_c.astype(BF16), ckc) + _dot(p_n.astype(BF16), ckn)) / l
    ctx = ctx.astype(BF16)
    for hd in range(N_HEADS):
        o_ref[:, hd * V_HEAD:(hd + 1) * V_HEAD] = _dot(
            ctx[hd * ln:(hd + 1) * ln], wuv_ref[hd]).astype(BF16)


def _dec_attention(ql, qpe, cache_ckv, cache_kpe_t, ckv_new, kpe_new, wuv_t, ln):
    nb, past = cache_ckv.shape[0], cache_ckv.shape[1]
    return pl.pallas_call(
        functools.partial(_dec_attn_kernel, ln=ln),
        out_shape=jax.ShapeDtypeStruct((nb * ln, N_HEADS * V_HEAD), BF16),
        grid=(nb,),
        in_specs=[pl.BlockSpec((None, N_HEADS, ln, KV_RANK), lambda b: (b, 0, 0, 0)),
                  pl.BlockSpec((None, N_HEADS, ln, LANES), lambda b: (b, 0, 0, 0)),
                  pl.BlockSpec((None, past, KV_RANK), lambda b: (b, 0, 0)),
                  pl.BlockSpec((None, QK_ROPE, past), lambda b: (b, 0, 0)),
                  pl.BlockSpec((ln, KV_RANK), lambda b: (b, 0)),
                  pl.BlockSpec((ln, QK_ROPE), lambda b: (b, 0)),
                  _resident(wuv_t.shape)],
        out_specs=pl.BlockSpec((ln, N_HEADS * V_HEAD), lambda b: (b, 0)),
        compiler_params=_params("arbitrary"),
        name="dec_attn",
    )(ql, qpe, cache_ckv, cache_kpe_t, ckv_new, kpe_new, wuv_t)


def _gla_consts(ln, width):
    row = lax.broadcasted_iota(jnp.int32, (8, width), 0)
    t_idx = lax.broadcasted_iota(jnp.int32, (ln, ln), 0)
    s_idx = lax.broadcasted_iota(jnp.int32, (ln, ln), 1)
    level_mask = []
    for lev in range(int(math.log2(ln))):
        level_mask.append((((t_idx ^ s_idx) >> (lev + 1)) == 0)
                          & ((t_idx & (1 << lev)) != 0) & ((s_idx & (1 << lev)) == 0))
    return {
        "scan": [row >= sft for sft in (1, 2, 4)],
        "odd": (row & 1).astype(F32),
        "low4": (row & 4) == 0,
        "sign1": jnp.where((row & 2) != 0, 1.0, -1.0).astype(F32),
        "sign2": jnp.where((row & 4) != 0, 1.0, -1.0).astype(F32),
        "level_mask": level_mask,
    }


def _gla_chunk(qb, kb, vb, g, st_ref, loc_ref, cst, ln):
    nv = ln // 8
    width = g.shape[1]
    grp = lambda x: [x[8 * i:8 * i + 8, :] for i in range(nv)]
    cat = lambda xs: jnp.concatenate(xs, axis=0)
    head = lambda x, hd: x[:, hd * HG_DIM:(hd + 1) * HG_DIM]
    q, k = grp(qb.astype(F32)), grp(kb.astype(F32))
    gs = grp(g)

    loc = gs
    for sft, keep in zip((1, 2, 4), cst["scan"]):
        loc = [x + jnp.where(keep, pltpu.roll(x, sft, 0), 0.0) for x in loc]
    loc_ref[...] = cat(loc)
    bcast = lambda r: jnp.broadcast_to(loc_ref[r:r + 1, :], (8, width))
    tot = [bcast(8 * i + 7) for i in range(nv)]
    pin = [tot[0]]
    for i in range(1, nv):
        pin.append(pin[-1] + tot[i])
    b = [loc[0]] + [loc[i] + pin[i - 1] for i in range(1, nv)]
    b_last = pin[-1]

    exps = [[gs[i] * cst["odd"] for i in range(nv)],
            [(loc[i] - jnp.where(cst["low4"], bcast(8 * i + 1), bcast(8 * i + 5))) * cst["sign1"]
             for i in range(nv)],
            [(loc[i] - bcast(8 * i + 3)) * cst["sign2"] for i in range(nv)]]
    upper = [[True] * nv, [True] * nv, [True] * nv]
    lower = [[True] * nv, [True] * nv, [True] * nv]
    for lev in range(3, int(math.log2(ln))):
        m = 1 << (lev - 3)
        e, up, lo = [], [], []
        for i in range(nv):
            mid = (i // (2 * m)) * 2 * m + m - 1
            is_up = (i // m) % 2 == 1
            if is_up:
                e.append(loc[i] if mid == i - 1 else b[i] - pin[mid])
            else:
                e.append(tot[i] - loc[i] if mid == i else pin[mid] - b[i])
            up.append(is_up)
            lo.append(not is_up)
        exps.append(e)
        upper.append(up)
        lower.append(lo)

    a = [jnp.zeros((ln, ln), F32)] * HG_HEADS
    for lev, e in enumerate(exps):
        w = [jnp.exp(x) for x in e]
        qt = cat([q[i] * w[i] if upper[lev][i] else q[i] for i in range(nv)]).astype(BF16)
        kt = cat([k[i] * w[i] if lower[lev][i] else k[i] for i in range(nv)]).astype(BF16)
        a = [jnp.where(cst["level_mask"][lev], _dot(head(qt, hd), head(kt, hd), _NT), a[hd])
             for hd in range(HG_HEADS)]

    qd = cat([q[i] * jnp.exp(b[i]) for i in range(nv)]).astype(BF16)
    kd = cat([k[i] * jnp.exp(b_last - b[i]) for i in range(nv)]).astype(BF16)
    qk = cat([q[i] * k[i] for i in range(nv)])
    vf = vb.astype(F32)
    decay = jnp.exp(b_last[0:1, :])
    outs = []
    for hd in range(HG_HEADS):
        st = st_ref[hd]
        v_h = head(vb, hd)
        o = _dot(head(qd, hd), st.astype(BF16), _NT) + _dot(a[hd].astype(BF16), v_h)
        outs.append(o + jnp.sum(head(qk, hd), axis=-1, keepdims=True) * head(vf, hd))
        st_ref[hd] = st * head(decay, hd) + _dot(v_h, head(kd, hd), _TN)
    return outs


def _hgrn_kernel(*refs, ln, n_chunks, has_init):
    if has_init:
        qh_ref, kh_ref, vh_ref, gate_ref, g_ref, gw_ref, s0_ref, o_ref, sout_ref, st_ref, b_ref = refs
    else:
        qh_ref, kh_ref, vh_ref, gate_ref, g_ref, gw_ref, o_ref, sout_ref, st_ref, b_ref = refs
        s0_ref = None
    si = pl.program_id(1)

    @pl.when(si == 0)
    def _():
        for hd in range(HG_HEADS):
            st_ref[hd] = s0_ref[hd].T if has_init else jnp.zeros((HG_DIM, HG_DIM), F32)

    gw = gw_ref[...]
    cst = _gla_consts(ln, HG_WIDTH)

    def body(c, carry):
        rows = pl.ds(pl.multiple_of(c * ln, ln), ln)
        outs = _gla_chunk(qh_ref[rows, :], kh_ref[rows, :], vh_ref[rows, :], g_ref[rows, :],
                          st_ref, b_ref, cst, ln)
        for hd in range(HG_HEADS):
            lanes = slice(hd * HG_DIM, (hd + 1) * HG_DIM)
            o_ref[rows, lanes] = (_rms(outs[hd]) * gw * gate_ref[rows, lanes].astype(F32)).astype(BF16)
        return carry

    lax.fori_loop(0, n_chunks, body, 0, unroll=2 if n_chunks % 2 == 0 else 1)

    @pl.when(si == pl.num_programs(1) - 1)
    def _():
        for hd in range(HG_HEADS):
            sout_ref[hd] = st_ref[hd].T


def _hgrn(qh, kh, vh, gate, g, gw, state0, batch, seq, ln, ts):
    ns = seq // ts
    has_init = state0 is not None
    row = pl.BlockSpec((ts, HG_WIDTH), lambda b, s: (b * ns + s, 0))
    st_spec = pl.BlockSpec((None, HG_HEADS, HG_DIM, HG_DIM), lambda b, s: (b, 0, 0, 0))
    in_specs = [row] * 5 + [_resident((1, HG_DIM))]
    args = [qh, kh, vh, gate, g, gw]
    if has_init:
        in_specs.append(st_spec)
        args.append(state0)
    return pl.pallas_call(
        functools.partial(_hgrn_kernel, ln=ln, n_chunks=ts // ln, has_init=has_init),
        out_shape=[jax.ShapeDtypeStruct((batch * seq, HG_WIDTH), BF16),
                   jax.ShapeDtypeStruct((batch, HG_HEADS, HG_DIM, HG_DIM), F32)],
        grid=(batch, ns),
        in_specs=in_specs,
        out_specs=[row, st_spec],
        scratch_shapes=[pltpu.VMEM((HG_HEADS, HG_DIM, HG_DIM), F32),
                        pltpu.VMEM((ln, HG_WIDTH), F32)],
        compiler_params=_params("arbitrary", "arbitrary"),
        name="hgrn",
    )(*args)


def _out_proj_kernel(x_ref, gt_ref, om_ref, oh_ref, w_ref, o_ref):
    n = om_ref.shape[1]
    mix = _dot(om_ref[...], w_ref[0:n, :]) + _dot(oh_ref[...], w_ref[n:, :])
    o_ref[...] = x_ref[...] + gt_ref[...] * mix


def _out_proj(x, gt, o_mla, o_hg, w_out, tm):
    t, d = x.shape
    n_tiles = t // tm
    tiles_per_group = n_tiles // gt.shape[0]
    r = gt.shape[1]
    row = lambda n: pl.BlockSpec((tm, n), lambda i: (i, 0))
    return pl.pallas_call(
        _out_proj_kernel,
        out_shape=jax.ShapeDtypeStruct((t, d), F32),
        grid=(n_tiles,),
        in_specs=[row(d), pl.BlockSpec((None, r, d), lambda i: (i // tiles_per_group, 0, 0)),
                  row(o_mla.shape[1]), row(o_hg.shape[1]), _resident(w_out.shape)],
        out_specs=row(d),
        compiler_params=_params("arbitrary"),
        name="out_proj",
    )(x, gt, o_mla, o_hg, w_out)


def _ffn_kernel(x_ref, sh_ref, sc_ref, gt_ref, w1_ref, w2_ref, gf_ref, o_ref, h_ref):
    f = pl.program_id(1)

    @pl.when(f == 0)
    def _():
        h_ref[...] = (_rms(x_ref[...]) * (1.0 + sc_ref[...]) + sh_ref[...]).astype(BF16)
        o_ref[...] = jnp.zeros_like(o_ref)

    a = jnp.maximum(_dot(h_ref[...], w1_ref[...]), 0.0)
    o_ref[...] += _dot((a * a).astype(BF16), w2_ref[...])

    @pl.when(f == pl.num_programs(1) - 1)
    def _():
        x2 = x_ref[...] + gt_ref[...] * o_ref[...]
        o_ref[...] = _rms(x2) * gf_ref[...]


def _ffn(x, sh, sc, gt, w1, w2, gf, tm, tf):
    t, d = x.shape
    dff = w1.shape[1]
    n_tiles = t // tm
    tiles_per_group = n_tiles // sh.shape[0]
    r = sh.shape[1]
    row = pl.BlockSpec((tm, d), lambda i, f: (i, 0))
    mod = pl.BlockSpec((None, r, d), lambda i, f: (i // tiles_per_group, 0, 0))
    return pl.pallas_call(
        _ffn_kernel,
        out_shape=jax.ShapeDtypeStruct((t, d), F32),
        grid=(n_tiles, dff // tf),
        in_specs=[row, mod, mod, mod,
                  pl.BlockSpec((d, tf), lambda i, f: (0, f)),
                  pl.BlockSpec((tf, d), lambda i, f: (f, 0)),
                  pl.BlockSpec((1, d), lambda i, f: (0, 0))],
        out_specs=row,
        scratch_shapes=[pltpu.VMEM((tm, d), BF16)],
        compiler_params=_params("arbitrary", "arbitrary"),
        name="ffn",
    )(x, sh, sc, gt, w1, w2, gf)


def _rot_half_cols(w, axis=-1):
    lo, hi = jnp.split(w, 2, axis=axis)
    return jnp.concatenate([-hi, lo], axis=axis)


def _prep_weights(w_in, w_uq, g_q, g_kv, w_uk, w_uv, lb, g_hgrn, w_out, w_ff1, w_ff2, g_final):
    d = w_in.shape[0]
    o_kpe = Q_RANK + KV_RANK
    o_hg = o_kpe + QK_ROPE
    w_in_t = jnp.swapaxes(w_in, 0, 1)
    w_kpe_t = w_in_t[o_kpe:o_hg]
    q_scale = ATTN_SCALE * math.log2(math.e)
    wq = w_uq.reshape(Q_RANK, N_HEADS, QK_NOPE + QK_ROPE) * q_scale
    wq_pe = wq[..., QK_NOPE:]
    wuq = jnp.concatenate([wq[..., :QK_NOPE], wq_pe, _rot_half_cols(wq_pe)], axis=-1)
    return {
        "wa": w_in_t[:o_kpe].astype(BF16),
        "wkpe": jnp.concatenate([w_kpe_t, _rot_half_cols(w_kpe_t, axis=0)], axis=0).astype(BF16),
        "wh": w_in_t[o_hg:].astype(BF16),
        "wuq": wuq.reshape(Q_RANK, N_HEADS * HEAD_PACK).astype(BF16),
        "wukv": jnp.concatenate([w_uk.reshape(KV_RANK, -1), w_uv.reshape(KV_RANK, -1)], axis=-1).astype(BF16),
        "wuk_t": jnp.transpose(w_uk, (1, 0, 2)).astype(BF16),
        "wuv_t": jnp.transpose(w_uv, (1, 0, 2)).astype(BF16),
        "gq": g_q.reshape(1, -1), "gkv": g_kv.reshape(1, -1),
        "lb": lb.reshape(1, -1), "ghg": g_hgrn.reshape(1, -1),
        "wout": w_out.astype(BF16), "w1": w_ff1.astype(BF16), "w2": w_ff2.astype(BF16),
        "gfin": g_final.reshape(1, d),
    }


def _layer(x, mods, w, batch, seq, cache, past):
    sh1, sc1, gt1, sh2, sc2, gt2 = mods
    t = x.shape[0]
    tm = min(512, t)
    prompt = cache is None
    if prompt:
        q, ckv, kpe, kf, v = _mla_in(x, sh1, sc1, w, tm, True, past, seq)
        o_mla = _attention(q, kf, v, batch, seq, 512)
    else:
        cache_ckv, cache_kpe, state0 = cache
        q, ckv, kpe = _mla_in(x, sh1, sc1, w, tm, False, past, seq)
        ql, qpe = _q_latent(q, w["wuk_t"], batch, seq)
        o_mla = _dec_attention(ql, qpe, cache_ckv, cache_kpe, ckv, kpe, w["wuv_t"], seq)
    qh, kh, vh, gate, g = _hg_in(x, sh1, sc1, w, tm)
    if prompt:
        o_hg, s_fin = _hgrn(qh, kh, vh, gate, g, w["ghg"], None, batch, seq, CHUNK, 512)
    else:
        o_hg, s_fin = _hgrn(qh, kh, vh, gate, g, w["ghg"], state0, batch, seq, seq, seq)
    x1 = _out_proj(x, gt1, o_mla, o_hg, w["wout"], tm)
    y = _ffn(x1, sh2, sc2, gt2, w["w1"], w["w2"], w["gfin"], min(1024, t), 512)
    return y, ckv, kpe, s_fin


def kernel(x_prompt, x_sample, c_prompt, c_sample, cache_ckv, cache_kpe, state_hgrn, w_ada, b_ada, w_in, w_uq, g_q, g_kv, w_uk, w_uv, hg_lower_bounds, g_hgrn, w_out, w_ff1, w_ff2, g_final):
    depth = w_ada.shape[0]
    assert depth == 1, "single-layer step"
    bp, sp, d = x_prompt.shape
    bs, ss, _ = x_sample.shape
    past = cache_ckv.shape[2]

    lb_all = jnp.cumsum(jax.nn.softmax(hg_lower_bounds.astype(F32), axis=0), axis=0)
    w = _prep_weights(w_in[0], w_uq[0], g_q[0], g_kv[0], w_uk[0], w_uv[0], lb_all[0], g_hgrn[0],
                      w_out[0], w_ff1[0], w_ff2[0], g_final)

    c_all = jnp.concatenate([c_prompt, c_sample], axis=0)
    n_c = c_all.shape[0]
    c_pad = jnp.pad(c_all, ((0, -n_c % 8), (0, 0)))
    mod = _modulation(c_pad, w_ada[0], b_ada[0])[:n_c]
    mod_p = [m.reshape(bp, 1, d) for m in jnp.split(mod[:bp], 6, axis=-1)]
    mod_s = [jnp.repeat(m, ss, axis=0).reshape(1, bs * ss, d) for m in jnp.split(mod[bp:], 6, axis=-1)]

    yp, ckv_p, kpe_p, st_p = _layer(x_prompt.reshape(bp * sp, d), mod_p, w, bp, sp, None, 0)
    ys, ckv_s, kpe_s, st_s = _layer(x_sample.reshape(bs * ss, d), mod_s, w, bs, ss,
                                    (cache_ckv[0], jnp.swapaxes(cache_kpe[0], 1, 2), state_hgrn[0]), past)
    return (yp.reshape(bp, sp, d), ys.reshape(bs, ss, d),
            ckv_p.reshape(1, bp, sp, KV_RANK), kpe_p.reshape(1, bp, sp, QK_ROPE), st_p[None],
            ckv_s.reshape(1, bs, ss, KV_RANK), kpe_s.reshape(1, bs, ss, QK_ROPE), st_s[None])
```

```python
import functools
import math

import jax
import jax.numpy as jnp
from jax import lax
from jax.experimental import pallas as pl
from jax.experimental.pallas import tpu as pltpu

F32 = jnp.float32
BF16 = jnp.bfloat16

EPS = 1e-6
CHUNK = 64
N_HEADS = 8
QK_NOPE = 128
QK_ROPE = 64
V_HEAD = 128
Q_RANK = 512
KV_RANK = 512
ROPE_BASE = 10000.0
ATTN_SCALE = (QK_NOPE + QK_ROPE) ** -0.5
HG_HEADS = 8
HG_DIM = 128
HG_WIDTH = HG_HEADS * HG_DIM
HEAD_PACK = 2 * QK_NOPE
LANES = 128
VMEM_LIMIT = 56 * 1024 * 1024

_NT = (((1,), (1,)), ((), ()))
_TN = (((0,), (0,)), ((), ()))


def _rms(x):
    return x * lax.rsqrt(jnp.mean(x * x, axis=-1, keepdims=True) + EPS)


def _dot(a, b, dims=None):
    if dims is None:
        return jnp.dot(a, b, preferred_element_type=F32)
    return lax.dot_general(a, b, dims, preferred_element_type=F32)


def _resident(shape):
    return pl.BlockSpec(shape, lambda *_: (0,) * len(shape), pipeline_mode=pl.Buffered(1))


def _params(*sem):
    return pltpu.CompilerParams(dimension_semantics=sem, vmem_limit_bytes=VMEM_LIMIT)


def _mod_kernel(c_ref, w_ref, b_ref, o_ref):
    c = c_ref[...]
    a = (c * jax.nn.sigmoid(c)).astype(BF16)
    o_ref[...] = _dot(a, w_ref[...].astype(BF16)) + b_ref[...]


def _modulation(c_all, w_ada, b_ada):
    m, d = c_all.shape
    n = w_ada.shape[1]
    tn = 1024
    return pl.pallas_call(
        _mod_kernel,
        out_shape=jax.ShapeDtypeStruct((m, n), F32),
        grid=(n // tn,),
        in_specs=[pl.BlockSpec((m, d), lambda j: (0, 0)),
                  pl.BlockSpec((d, tn), lambda j: (0, j)),
                  pl.BlockSpec((1, tn), lambda j: (0, j))],
        out_specs=pl.BlockSpec((m, tn), lambda j: (0, j)),
        compiler_params=_params("arbitrary"),
        name="mod",
    )(c_all, w_ada, b_ada.reshape(1, n))


def _mla_in_kernel(x_ref, sh_ref, sc_ref, cr_ref, sr_ref, wa_ref, wkpe_ref, wuq_ref, wukv_ref,
                   gq_ref, gkv_ref, q_ref, ckv_ref, kpe_ref, *kv_refs, pos_base, pos_period):
    x = x_ref[...]
    tm = x.shape[0]
    h = (_rms(x) * (1.0 + sc_ref[...]) + sh_ref[...]).astype(BF16)

    half = QK_ROPE // 2
    lane = lax.broadcasted_iota(jnp.int32, (8, LANES), 1)
    base = (pos_base + lax.rem(pl.program_id(0) * tm, pos_period)).astype(F32)
    ang_b = base * jnp.exp(-math.log(ROPE_BASE) * lax.rem(lane, half).astype(F32) / half)
    cb, sb = jnp.cos(ang_b)[0:1], jnp.sin(ang_b)[0:1]
    cr, sr = cr_ref[...], sr_ref[...]
    cos = cr * cb - sr * sb
    sin = sr * cb + cr * sb

    def rope(t):
        return t * cos + pltpu.roll(t, QK_ROPE, 1) * sin

    a = _dot(h, wa_ref[...], _NT)
    cqn = (_rms(a[:, :Q_RANK]) * gq_ref[...]).astype(BF16)
    ckvn = _rms(a[:, Q_RANK:]) * gkv_ref[...]
    ckv_ref[...] = ckvn

    q = _dot(cqn, wuq_ref[...])
    for hd in range(N_HEADS):
        lo = hd * HEAD_PACK
        q_ref[:, lo:lo + QK_NOPE] = q[:, lo:lo + QK_NOPE].astype(BF16)
        q_ref[:, lo + QK_NOPE:lo + HEAD_PACK] = rope(q[:, lo + QK_NOPE:lo + HEAD_PACK]).astype(BF16)

    kp = rope(_dot(h, wkpe_ref[...], _NT))
    kpe_ref[...] = kp[:, :QK_ROPE]

    if kv_refs:
        kf_ref, v_ref = kv_refs
        kv = _dot(ckvn.astype(BF16), wukv_ref[...])
        kpb = kp.astype(BF16)
        for hd in range(N_HEADS):
            lo = hd * HEAD_PACK
            kf_ref[:, lo:lo + QK_NOPE] = kv[:, hd * QK_NOPE:(hd + 1) * QK_NOPE].astype(BF16)
            kf_ref[:, lo + QK_NOPE:lo + HEAD_PACK] = kpb
            vo = N_HEADS * QK_NOPE + hd * V_HEAD
            v_ref[:, lo:lo + V_HEAD] = kv[:, vo:vo + V_HEAD].astype(BF16)
            v_ref[:, lo + V_HEAD:lo + HEAD_PACK] = jnp.ones((kv.shape[0], HEAD_PACK - V_HEAD), BF16)


def _mla_in(x, sh, sc, w, tm, with_kv, pos_base, pos_period):
    t, d = x.shape
    n_tiles = t // tm
    tiles_per_group = n_tiles // sh.shape[0]
    r = sh.shape[1]
    row = lambda n: pl.BlockSpec((tm, n), lambda i: (i, 0))
    mod = pl.BlockSpec((None, r, d), lambda i: (i // tiles_per_group, 0, 0))
    packed = N_HEADS * HEAD_PACK
    half = QK_ROPE // 2
    lane = jnp.arange(LANES)
    inv_freq = jnp.exp(-math.log(ROPE_BASE) * (lane % half).astype(F32) / half)
    ang = (jnp.arange(tm) % min(tm, pos_period)).astype(F32)[:, None] * inv_freq[None, :]
    cr = jnp.where(lane < QK_ROPE, jnp.cos(ang), 0.0)
    sr = jnp.where(lane < QK_ROPE, jnp.sin(ang), 0.0)
    out_shape = [jax.ShapeDtypeStruct((t, packed), BF16),
                 jax.ShapeDtypeStruct((t, KV_RANK), F32),
                 jax.ShapeDtypeStruct((t, QK_ROPE), F32)]
    out_specs = [row(packed), row(KV_RANK), row(QK_ROPE)]
    if with_kv:
        out_shape += [jax.ShapeDtypeStruct((t, packed), BF16),
                      jax.ShapeDtypeStruct((t, packed), BF16)]
        out_specs += [row(packed), row(packed)]
    return pl.pallas_call(
        functools.partial(_mla_in_kernel, pos_base=pos_base, pos_period=pos_period),
        out_shape=out_shape,
        grid=(n_tiles,),
        in_specs=[row(d), mod, mod, _resident((tm, LANES)), _resident((tm, LANES)),
                  _resident(w["wa"].shape), _resident(w["wkpe"].shape),
                  _resident(w["wuq"].shape), _resident(w["wukv"].shape),
                  _resident((1, Q_RANK)), _resident((1, KV_RANK))],
        out_specs=out_specs,
        compiler_params=_params("arbitrary"),
        name="mla_in",
    )(x, sh, sc, cr, sr, w["wa"], w["wkpe"], w["wuq"], w["wukv"], w["gq"], w["gkv"])


def _hg_in_kernel(x_ref, sh_ref, sc_ref, wh_ref, lb_ref, qh_ref, kh_ref, vh_ref, gate_ref, g_ref):
    x = x_ref[...]
    h = (_rms(x) * (1.0 + sc_ref[...]) + sh_ref[...]).astype(BF16)
    n = HG_WIDTH
    hq = _dot(h, wh_ref[0:n, :], _NT)
    qh_ref[...] = (hq * jax.nn.sigmoid(hq) * (HG_DIM ** -0.5)).astype(BF16)
    hf = _dot(h, wh_ref[n:2 * n, :], _NT)
    lb = lb_ref[...]
    f = lb + (1.0 - lb) * jax.nn.sigmoid(hf)
    kh_ref[...] = (1.0 - f).astype(BF16)
    g_ref[...] = jnp.log2(f)
    vh_ref[...] = _dot(h, wh_ref[2 * n:3 * n, :], _NT).astype(BF16)
    hg = _dot(h, wh_ref[3 * n:4 * n, :], _NT)
    gate_ref[...] = (hg * jax.nn.sigmoid(hg)).astype(BF16)


def _hg_in(x, sh, sc, w, tm):
    t, d = x.shape
    n_tiles = t // tm
    tiles_per_group = n_tiles // sh.shape[0]
    r = sh.shape[1]
    row = lambda n: pl.BlockSpec((tm, n), lambda i: (i, 0))
    mod = pl.BlockSpec((None, r, d), lambda i: (i // tiles_per_group, 0, 0))
    half = jax.ShapeDtypeStruct((t, HG_WIDTH), BF16)
    return pl.pallas_call(
        _hg_in_kernel,
        out_shape=[half, half, half, half, jax.ShapeDtypeStruct((t, HG_WIDTH), F32)],
        grid=(n_tiles,),
        in_specs=[row(d), mod, mod, _resident(w["wh"].shape), _resident((1, HG_WIDTH))],
        out_specs=[row(HG_WIDTH)] * 5,
        compiler_params=_params("arbitrary"),
        name="hg_in",
    )(x, sh, sc, w["wh"], w["lb"])


def _attn_kernel(q_ref, qn_ref, k_ref, v_ref, o_ref, s0_ref, s1_ref, mx0_ref, mx1_ref, m_ref, acc_ref,
                 *, tk, nh):
    qi = pl.program_id(2)
    diag_ok = (lax.broadcasted_iota(jnp.int32, (tk, tk), 1) // CHUNK
               <= lax.broadcasted_iota(jnp.int32, (tk, tk), 0) // CHUNK)
    streams = [(hd, r) for hd in range(nh) for r in range(2)]
    sid = lambda hd, r: 2 * hd + r

    def kv_rows(blk):
        return pl.ds(pl.multiple_of(blk * tk, tk), tk)

    def rows(r):
        return slice(r * tk, (r + 1) * tk)

    def head(hd):
        return slice(hd * HEAD_PACK, (hd + 1) * HEAD_PACK)

    def lanes(x, n):
        return jnp.concatenate([x] * (n // LANES), axis=1)

    def row_max(s):
        return jnp.broadcast_to(jnp.max(s, axis=-1, keepdims=True), (tk, LANES))

    def scores(q, hd, r, blk):
        return _dot(q[rows(r), head(hd)], k_ref[kv_rows(blk), head(hd)], _NT)

    def produce(q, hd, r, blk, s_ref, mx_ref):
        s = scores(q, hd, r, blk)
        s_ref[sid(hd, r)] = s
        mx_ref[sid(hd, r)] = row_max(s)

    def block_softmax(s, mx, hd, blk):
        p = jnp.exp2(s - lanes(mx, tk)).astype(BF16)
        return mx, _dot(p, v_ref[kv_rows(blk), head(hd)])

    def merge(a, b):
        (ma, acc_a), (mb, acc_b) = a, b
        m = jnp.maximum(ma, mb)
        return m, (lanes(jnp.exp2(ma - m), HEAD_PACK) * acc_a
                   + lanes(jnp.exp2(mb - m), HEAD_PACK) * acc_b)

    def update(hd, r, s, mx, blk):
        i = sid(hd, r)
        m_old = m_ref[i]
        m_new = jnp.maximum(m_old, mx)
        m_ref[i] = m_new
        p = jnp.exp2(s - lanes(m_new, tk)).astype(BF16)
        acc_ref[i] = (lanes(jnp.exp2(m_old - m_new), HEAD_PACK) * acc_ref[i]
                      + _dot(p, v_ref[kv_rows(blk), head(hd)]))

    @pl.when(qi == 0)
    def _():
        for hd, r in streams:
            produce(q_ref, hd, r, 0, s0_ref, mx0_ref)

    for i in range(2 * nh):
        m_ref[i] = jnp.full((tk, LANES), -1e30, F32)
        acc_ref[i] = jnp.zeros((tk, HEAD_PACK), F32)

    def body(j, carry):
        blk = 2 * j
        for (cur, cmx), (nxt, nmx) in (((s0_ref, mx0_ref), (s1_ref, mx1_ref)),
                                       ((s1_ref, mx1_ref), (s0_ref, mx0_ref))):
            for hd, r in streams:
                produce(q_ref, hd, r, blk + 1, nxt, nmx)
            for hd, r in streams:
                update(hd, r, cur[sid(hd, r)], cmx[sid(hd, r)], blk)
            blk = blk + 1
        return carry

    lax.fori_loop(0, qi, body, 0)

    for hd in range(nh):
        top, bot = sid(hd, 0), sid(hd, 1)
        d0 = jnp.where(diag_ok, s0_ref[top], -1e30)
        d1 = jnp.where(diag_ok, scores(q_ref, hd, 1, 2 * qi + 1), -1e30)
        part_a = block_softmax(d0, row_max(d0), hd, 2 * qi)
        part_b = block_softmax(s0_ref[bot], mx0_ref[bot], hd, 2 * qi)
        part_c = block_softmax(d1, row_max(d1), hd, 2 * qi + 1)
        finals = [merge((m_ref[top], acc_ref[top]), part_a),
                  merge(merge((m_ref[bot], acc_ref[bot]), part_b), part_c)]
        for r in range(2):
            acc = finals[r][1]
            o_ref[rows(r), hd * V_HEAD:(hd + 1) * V_HEAD] = (acc[:, :V_HEAD] / acc[:, V_HEAD:]).astype(BF16)
    for hd, r in streams:
        produce(qn_ref, hd, r, 0, s0_ref, mx0_ref)


def _attention(q, kf, vp, batch, seq, tk, nh):
    tq = 2 * tk
    nq = seq // tq
    return pl.pallas_call(
        functools.partial(_attn_kernel, tk=tk, nh=nh),
        out_shape=jax.ShapeDtypeStruct((batch * seq, N_HEADS * V_HEAD), BF16),
        grid=(batch, N_HEADS // nh, nq),
        in_specs=[pl.BlockSpec((tq, nh * HEAD_PACK), lambda b, h, i: (b * nq + i, h)),
                  pl.BlockSpec((tq, nh * HEAD_PACK), lambda b, h, i: (b * nq + jnp.minimum(i + 1, nq - 1), h)),
                  pl.BlockSpec((seq, nh * HEAD_PACK), lambda b, h, i: (b, h)),
                  pl.BlockSpec((seq, nh * HEAD_PACK), lambda b, h, i: (b, h))],
        out_specs=pl.BlockSpec((tq, nh * V_HEAD), lambda b, h, i: (b * nq + i, h)),
        scratch_shapes=[pltpu.VMEM((2 * nh, tk, tk), F32), pltpu.VMEM((2 * nh, tk, tk), F32),
                        pltpu.VMEM((2 * nh, tk, LANES), F32), pltpu.VMEM((2 * nh, tk, LANES), F32),
                        pltpu.VMEM((2 * nh, tk, LANES), F32), pltpu.VMEM((2 * nh, tk, HEAD_PACK), F32)],
        compiler_params=_params("arbitrary", "arbitrary", "arbitrary"),
        name="attn",
    )(q, q, kf, vp)


def _qlat_kernel(qn_ref, qp_ref, wuk_ref, ql_ref, qpe_ref):
    nb = ql_ref.shape[0]
    ql = _dot(qn_ref[...], wuk_ref[...], _NT)
    ql_ref[...] = ql.reshape(nb, -1, KV_RANK)
    qpe_ref[...] = qp_ref[...].astype(F32).reshape(nb, -1, LANES)


def _q_latent(q, wuk_t, nb, ln):
    t = q.shape[0]
    return pl.pallas_call(
        _qlat_kernel,
        out_shape=[jax.ShapeDtypeStruct((nb, N_HEADS, ln, KV_RANK), F32),
                   jax.ShapeDtypeStruct((nb, N_HEADS, ln, LANES), F32)],
        grid=(N_HEADS,),
        in_specs=[pl.BlockSpec((t, QK_NOPE), lambda h: (0, 2 * h)),
                  pl.BlockSpec((t, QK_NOPE), lambda h: (0, 2 * h + 1)),
                  pl.BlockSpec((None, KV_RANK, QK_NOPE), lambda h: (h, 0, 0))],
        out_specs=[pl.BlockSpec((nb, None, ln, KV_RANK), lambda h: (0, h, 0, 0)),
                   pl.BlockSpec((nb, None, ln, LANES), lambda h: (0, h, 0, 0))],
        compiler_params=_params("arbitrary"),
        name="q_latent",
    )(q, q, wuk_t)


def _dec_attn_kernel(ql_ref, qpe_ref, cckv_ref, ckpe_ref, nckv_ref, nkpe_ref, wuv_ref, o_ref, *, ln):
    rows = N_HEADS * ln
    ql = ql_ref[...].reshape(rows, KV_RANK).astype(BF16)
    qp = qpe_ref[...].reshape(rows, LANES)[:, :QK_ROPE].astype(BF16)
    ckc = cckv_ref[...].astype(BF16)
    kpc = ckpe_ref[...].astype(BF16)
    ckn = nckv_ref[...].astype(BF16)
    kpn = nkpe_ref[...].astype(BF16)
    s_c = _dot(ql, ckc, _NT) + _dot(qp, kpc)
    s_n = _dot(ql, ckn, _NT) + _dot(qp, kpn, _NT)
    m = jnp.maximum(jnp.max(s_c, axis=-1, keepdims=True), jnp.max(s_n, axis=-1, keepdims=True))
    p_c = jnp.exp2(s_c - m)
    p_n = jnp.exp2(s_n - m)
    l = jnp.sum(p_c, axis=-1, keepdims=True) + jnp.sum(p_n, axis=-1, keepdims=True)
    ctx = (_dot(p_c.astype(BF16), ckc) + _dot(p_n.astype(BF16), ckn)) / l
    ctx = ctx.astype(BF16)
    for hd in range(N_HEADS):
        o_ref[:, hd * V_HEAD:(hd + 1) * V_HEAD] = _dot(
            ctx[hd * ln:(hd + 1) * ln], wuv_ref[hd]).astype(BF16)


def _dec_attention(ql, qpe, cache_ckv, cache_kpe_t, ckv_new, kpe_new, wuv_t, ln):
    nb, past = cache_ckv.shape[0], cache_ckv.shape[1]
    return pl.pallas_call(
        functools.partial(_dec_attn_kernel, ln=ln),
        out_shape=jax.ShapeDtypeStruct((nb * ln, N_HEADS * V_HEAD), BF16),
        grid=(nb,),
        in_specs=[pl.BlockSpec((None, N_HEADS, ln, KV_RANK), lambda b: (b, 0, 0, 0)),
                  pl.BlockSpec((None, N_HEADS, ln, LANES), lambda b: (b, 0, 0, 0)),
                  pl.BlockSpec((None, past, KV_RANK), lambda b: (b, 0, 0)),
                  pl.BlockSpec((None, QK_ROPE, past), lambda b: (b, 0, 0)),
                  pl.BlockSpec((ln, KV_RANK), lambda b: (b, 0)),
                  pl.BlockSpec((ln, QK_ROPE), lambda b: (b, 0)),
                  _resident(wuv_t.shape)],
        out_specs=pl.BlockSpec((ln, N_HEADS * V_HEAD), lambda b: (b, 0)),
        compiler_params=_params("arbitrary"),
        name="dec_attn",
    )(ql, qpe, cache_ckv, cache_kpe_t, ckv_new, kpe_new, wuv_t)


def _gla_consts(ln, width):
    row = lax.broadcasted_iota(jnp.int32, (8, width), 0)
    t_idx = lax.broadcasted_iota(jnp.int32, (ln, ln), 0)
    s_idx = lax.broadcasted_iota(jnp.int32, (ln, ln), 1)
    level_mask = []
    for lev in range(int(math.log2(ln))):
        level_mask.append((((t_idx ^ s_idx) >> (lev + 1)) == 0)
                          & ((t_idx & (1 << lev)) != 0) & ((s_idx & (1 << lev)) == 0))
    return {
        "scan": [row >= sft for sft in (1, 2, 4)],
        "odd": (row & 1).astype(F32),
        "low4": (row & 4) == 0,
        "sign1": jnp.where((row & 2) != 0, 1.0, -1.0).astype(F32),
        "sign2": jnp.where((row & 4) != 0, 1.0, -1.0).astype(F32),
        "level_mask": level_mask,
    }


def _gla_chunk(qb, kb, vb, g, st_ref, loc_ref, cst, ln):
    nv = ln // 8
    width = g.shape[1]
    grp = lambda x: [x[8 * i:8 * i + 8, :] for i in range(nv)]
    cat = lambda xs: jnp.concatenate(xs, axis=0)
    head = lambda x, hd: x[:, hd * HG_DIM:(hd + 1) * HG_DIM]
    q, k = grp(qb.astype(F32)), grp(kb.astype(F32))
    gs = grp(g)

    loc = gs
    for sft, keep in zip((1, 2, 4), cst["scan"]):
        loc = [x + jnp.where(keep, pltpu.roll(x, sft, 0), 0.0) for x in loc]
    loc_ref[...] = cat(loc)
    bcast = lambda r: jnp.broadcast_to(loc_ref[r:r + 1, :], (8, width))
    tot = [bcast(8 * i + 7) for i in range(nv)]
    pin = [tot[0]]
    for i in range(1, nv):
        pin.append(pin[-1] + tot[i])
    b = [loc[0]] + [loc[i] + pin[i - 1] for i in range(1, nv)]
    b_last = pin[-1]

    exps = [[gs[i] * cst["odd"] for i in range(nv)],
            [(loc[i] - jnp.where(cst["low4"], bcast(8 * i + 1), bcast(8 * i + 5))) * cst["sign1"]
             for i in range(nv)],
            [(loc[i] - bcast(8 * i + 3)) * cst["sign2"] for i in range(nv)]]
    upper = [[True] * nv, [True] * nv, [True] * nv]
    lower = [[True] * nv, [True] * nv, [True] * nv]
    for lev in range(3, int(math.log2(ln))):
        m = 1 << (lev - 3)
        e, up, lo = [], [], []
        for i in range(nv):
            mid = (i // (2 * m)) * 2 * m + m - 1
            is_up = (i // m) % 2 == 1
            if is_up:
                e.append(loc[i] if mid == i - 1 else b[i] - pin[mid])
            else:
                e.append(tot[i] - loc[i] if mid == i else pin[mid] - b[i])
            up.append(is_up)
            lo.append(not is_up)
        exps.append(e)
        upper.append(up)
        lower.append(lo)

    a = [jnp.zeros((ln, ln), F32)] * HG_HEADS
    for lev, e in enumerate(exps):
        w = [jnp.exp2(x) for x in e]
        qt = cat([q[i] * w[i] if upper[lev][i] else q[i] for i in range(nv)]).astype(BF16)
        kt = cat([k[i] * w[i] if lower[lev][i] else k[i] for i in range(nv)]).astype(BF16)
        a = [jnp.where(cst["level_mask"][lev], _dot(head(qt, hd), head(kt, hd), _NT), a[hd])
             for hd in range(HG_HEADS)]

    qd = cat([q[i] * jnp.exp2(b[i]) for i in range(nv)]).astype(BF16)
    kd = cat([k[i] * jnp.exp2(b_last - b[i]) for i in range(nv)]).astype(BF16)
    qk = cat([q[i] * k[i] for i in range(nv)])
    vf = vb.astype(F32)
    decay = jnp.exp2(b_last[0:1, :])
    outs = []
    for hd in range(HG_HEADS):
        st = st_ref[hd]
        v_h = head(vb, hd)
        o = _dot(head(qd, hd), st.astype(BF16), _NT) + _dot(a[hd].astype(BF16), v_h)
        outs.append(o + jnp.sum(head(qk, hd), axis=-1, keepdims=True) * head(vf, hd))
        st_ref[hd] = st * head(decay, hd) + _dot(v_h, head(kd, hd), _TN)
    return outs


def _hgrn_kernel(*refs, ln, n_chunks, has_init):
    if has_init:
        qh_ref, kh_ref, vh_ref, gate_ref, g_ref, gw_ref, s0_ref, o_ref, sout_ref, st_ref, b_ref = refs
    else:
        qh_ref, kh_ref, vh_ref, gate_ref, g_ref, gw_ref, o_ref, sout_ref, st_ref, b_ref = refs
        s0_ref = None
    si = pl.program_id(1)

    @pl.when(si == 0)
    def _():
        for hd in range(HG_HEADS):
            st_ref[hd] = s0_ref[hd].T if has_init else jnp.zeros((HG_DIM, HG_DIM), F32)

    gw = gw_ref[...]
    cst = _gla_consts(ln, HG_WIDTH)

    def body(c, carry):
        rows = pl.ds(pl.multiple_of(c * ln, ln), ln)
        outs = _gla_chunk(qh_ref[rows, :], kh_ref[rows, :], vh_ref[rows, :], g_ref[rows, :],
                          st_ref, b_ref, cst, ln)
        for hd in range(HG_HEADS):
            lanes = slice(hd * HG_DIM, (hd + 1) * HG_DIM)
            o_ref[rows, lanes] = (_rms(outs[hd]) * gw * gate_ref[rows, lanes].astype(F32)).astype(BF16)
        return carry

    lax.fori_loop(0, n_chunks, body, 0, unroll=2 if n_chunks % 2 == 0 else 1)

    @pl.when(si == pl.num_programs(1) - 1)
    def _():
        for hd in range(HG_HEADS):
            sout_ref[hd] = st_ref[hd].T


def _hgrn(qh, kh, vh, gate, g, gw, state0, batch, seq, ln, ts):
    ns = seq // ts
    has_init = state0 is not None
    row = pl.BlockSpec((ts, HG_WIDTH), lambda b, s: (b * ns + s, 0))
    st_spec = pl.BlockSpec((None, HG_HEADS, HG_DIM, HG_DIM), lambda b, s: (b, 0, 0, 0))
    in_specs = [row] * 5 + [_resident((1, HG_DIM))]
    args = [qh, kh, vh, gate, g, gw]
    if has_init:
        in_specs.append(st_spec)
        args.append(state0)
    return pl.pallas_call(
        functools.partial(_hgrn_kernel, ln=ln, n_chunks=ts // ln, has_init=has_init),
        out_shape=[jax.ShapeDtypeStruct((batch * seq, HG_WIDTH), BF16),
                   jax.ShapeDtypeStruct((batch, HG_HEADS, HG_DIM, HG_DIM), F32)],
        grid=(batch, ns),
        in_specs=in_specs,
        out_specs=[row, st_spec],
        scratch_shapes=[pltpu.VMEM((HG_HEADS, HG_DIM, HG_DIM), F32),
                        pltpu.VMEM((ln, HG_WIDTH), F32)],
        compiler_params=_params("arbitrary", "arbitrary"),
        name="hgrn",
    )(*args)


def _out_proj_kernel(x_ref, gt_ref, om_ref, oh_ref, w_ref, o_ref):
    n = om_ref.shape[1]
    mix = _dot(om_ref[...], w_ref[0:n, :]) + _dot(oh_ref[...], w_ref[n:, :])
    o_ref[...] = x_ref[...] + gt_ref[...] * mix


def _out_proj(x, gt, o_mla, o_hg, w_out, tm):
    t, d = x.shape
    n_tiles = t // tm
    tiles_per_group = n_tiles // gt.shape[0]
    r = gt.shape[1]
    row = lambda n: pl.BlockSpec((tm, n), lambda i: (i, 0))
    return pl.pallas_call(
        _out_proj_kernel,
        out_shape=jax.ShapeDtypeStruct((t, d), F32),
        grid=(n_tiles,),
        in_specs=[row(d), pl.BlockSpec((None, r, d), lambda i: (i // tiles_per_group, 0, 0)),
                  row(o_mla.shape[1]), row(o_hg.shape[1]), _resident(w_out.shape)],
        out_specs=row(d),
        compiler_params=_params("arbitrary"),
        name="out_proj",
    )(x, gt, o_mla, o_hg, w_out)


def _ffn_kernel(x_ref, sh_ref, sc_ref, gt_ref, w1_ref, w2_ref, gf_ref, o_ref, h_ref):
    f = pl.program_id(1)
    last = pl.num_programs(1) - 1

    def mlp(h):
        a = jnp.maximum(_dot(h, w1_ref[...]), 0.0)
        return _dot((a * a).astype(BF16), w2_ref[...])

    @pl.when(f == 0)
    def _():
        h = (_rms(x_ref[...]) * (1.0 + sc_ref[...]) + sh_ref[...]).astype(BF16)
        h_ref[...] = h
        o_ref[...] = mlp(h)

    @pl.when(jnp.logical_and(f > 0, f < last))
    def _():
        o_ref[...] += mlp(h_ref[...])

    @pl.when(f == last)
    def _():
        x2 = x_ref[...] + gt_ref[...] * (o_ref[...] + mlp(h_ref[...]))
        o_ref[...] = _rms(x2) * gf_ref[...]


def _ffn(x, sh, sc, gt, w1, w2, gf, tm, tf):
    t, d = x.shape
    dff = w1.shape[1]
    n_tiles = t // tm
    tiles_per_group = n_tiles // sh.shape[0]
    r = sh.shape[1]
    row = pl.BlockSpec((tm, d), lambda i, f: (i, 0))
    mod = pl.BlockSpec((None, r, d), lambda i, f: (i // tiles_per_group, 0, 0))
    return pl.pallas_call(
        _ffn_kernel,
        out_shape=jax.ShapeDtypeStruct((t, d), F32),
        grid=(n_tiles, dff // tf),
        in_specs=[row, mod, mod, mod,
                  pl.BlockSpec((d, tf), lambda i, f: (0, f)),
                  pl.BlockSpec((tf, d), lambda i, f: (f, 0)),
                  pl.BlockSpec((1, d), lambda i, f: (0, 0))],
        out_specs=row,
        scratch_shapes=[pltpu.VMEM((tm, d), BF16)],
        compiler_params=_params("arbitrary", "arbitrary"),
        name="ffn",
    )(x, sh, sc, gt, w1, w2, gf)


def _rot_half_cols(w, axis=-1):
    lo, hi = jnp.split(w, 2, axis=axis)
    return jnp.concatenate([-hi, lo], axis=axis)


def _prep_weights(w_in, w_uq, g_q, g_kv, w_uk, w_uv, lb, g_hgrn, w_out, w_ff1, w_ff2, g_final):
    d = w_in.shape[0]
    o_kpe = Q_RANK + KV_RANK
    o_hg = o_kpe + QK_ROPE
    w_in_t = jnp.swapaxes(w_in, 0, 1)
    w_kpe_t = w_in_t[o_kpe:o_hg]
    q_scale = ATTN_SCALE * math.log2(math.e)
    wq = w_uq.reshape(Q_RANK, N_HEADS, QK_NOPE + QK_ROPE) * q_scale
    wq_pe = wq[..., QK_NOPE:]
    wuq = jnp.concatenate([wq[..., :QK_NOPE], wq_pe, _rot_half_cols(wq_pe)], axis=-1)
    return {
        "wa": w_in_t[:o_kpe].astype(BF16),
        "wkpe": jnp.concatenate([w_kpe_t, _rot_half_cols(w_kpe_t, axis=0)], axis=0).astype(BF16),
        "wh": w_in_t[o_hg:].astype(BF16),
        "wuq": wuq.reshape(Q_RANK, N_HEADS * HEAD_PACK).astype(BF16),
        "wukv": jnp.concatenate([w_uk.reshape(KV_RANK, -1), w_uv.reshape(KV_RANK, -1)], axis=-1).astype(BF16),
        "wuk_t": jnp.transpose(w_uk, (1, 0, 2)).astype(BF16),
        "wuv_t": jnp.transpose(w_uv, (1, 0, 2)).astype(BF16),
        "gq": g_q.reshape(1, -1), "gkv": g_kv.reshape(1, -1),
        "lb": lb.reshape(1, -1), "ghg": g_hgrn.reshape(1, -1),
        "wout": w_out.astype(BF16), "w1": w_ff1.astype(BF16), "w2": w_ff2.astype(BF16),
        "gfin": g_final.reshape(1, d),
    }


def _layer(x, mods, w, batch, seq, cache, past):
    sh1, sc1, gt1, sh2, sc2, gt2 = mods
    t = x.shape[0]
    tm = min(512, t)
    prompt = cache is None
    if prompt:
        q, ckv, kpe, kf, v = _mla_in(x, sh1, sc1, w, tm, True, past, seq)
        o_mla = _attention(q, kf, v, batch, seq, 512, 2)
    else:
        cache_ckv, cache_kpe, state0 = cache
        q, ckv, kpe = _mla_in(x, sh1, sc1, w, tm, False, past, seq)
        ql, qpe = _q_latent(q, w["wuk_t"], batch, seq)
        o_mla = _dec_attention(ql, qpe, cache_ckv, cache_kpe, ckv, kpe, w["wuv_t"], seq)
    qh, kh, vh, gate, g = _hg_in(x, sh1, sc1, w, tm)
    if prompt:
        o_hg, s_fin = _hgrn(qh, kh, vh, gate, g, w["ghg"], None, batch, seq, CHUNK, 1024)
    else:
        o_hg, s_fin = _hgrn(qh, kh, vh, gate, g, w["ghg"], state0, batch, seq, seq, seq)
    x1 = _out_proj(x, gt1, o_mla, o_hg, w["wout"], tm)
    y = _ffn(x1, sh2, sc2, gt2, w["w1"], w["w2"], w["gfin"], min(1024, t), 512)
    return y, ckv, kpe, s_fin


def kernel(x_prompt, x_sample, c_prompt, c_sample, cache_ckv, cache_kpe, state_hgrn, w_ada, b_ada, w_in, w_uq, g_q, g_kv, w_uk, w_uv, hg_lower_bounds, g_hgrn, w_out, w_ff1, w_ff2, g_final):
    depth = w_ada.shape[0]
    assert depth == 1, "single-layer step"
    bp, sp, d = x_prompt.shape
    bs, ss, _ = x_sample.shape
    past = cache_ckv.shape[2]

    lb_all = jnp.cumsum(jax.nn.softmax(hg_lower_bounds.astype(F32), axis=0), axis=0)
    w = _prep_weights(w_in[0], w_uq[0], g_q[0], g_kv[0], w_uk[0], w_uv[0], lb_all[0], g_hgrn[0],
                      w_out[0], w_ff1[0], w_ff2[0], g_final)

    c_all = jnp.concatenate([c_prompt, c_sample], axis=0)
    n_c = c_all.shape[0]
    c_pad = jnp.pad(c_all, ((0, -n_c % 8), (0, 0)))
    mod = _modulation(c_pad, w_ada[0], b_ada[0])[:n_c]
    mod_p = [m.reshape(bp, 1, d) for m in jnp.split(mod[:bp], 6, axis=-1)]
    mod_s = [jnp.repeat(m, ss, axis=0).reshape(1, bs * ss, d) for m in jnp.split(mod[bp:], 6, axis=-1)]

    yp, ckv_p, kpe_p, st_p = _layer(x_prompt.reshape(bp * sp, d), mod_p, w, bp, sp, None, 0)
    ys, ckv_s, kpe_s, st_s = _layer(x_sample.reshape(bs * ss, d), mod_s, w, bs, ss,
                                    (cache_ckv[0], jnp.swapaxes(cache_kpe[0], 1, 2), state_hgrn[0]), past)
    return (yp.reshape(bp, sp, d), ys.reshape(bs, ss, d),
            ckv_p.reshape(1, bp, sp, KV_RANK), kpe_p.reshape(1, bp, sp, QK_ROPE), st_p[None],
            ckv_s.reshape(1, bs, ss, KV_RANK), kpe_s.reshape(1, bs, ss, QK_ROPE), st_s[None])
```

```python
import functools
import math

import jax
import jax.numpy as jnp
from jax import lax
from jax.experimental import pallas as pl
from jax.experimental.pallas import tpu as pltpu

F32 = jnp.float32
BF16 = jnp.bfloat16

EPS = 1e-6
CHUNK = 64
N_HEADS = 8
QK_NOPE = 128
QK_ROPE = 64
V_HEAD = 128
Q_RANK = 512
KV_RANK = 512
ROPE_BASE = 10000.0
ATTN_SCALE = (QK_NOPE + QK_ROPE) ** -0.5
HG_HEADS = 8
HG_DIM = 128
HG_WIDTH = HG_HEADS * HG_DIM
HEAD_PACK = 2 * QK_NOPE
LANES = 128
VMEM_LIMIT = 56 * 1024 * 1024
FFN_VMEM_LIMIT = 62 * 1024 * 1024

_NT = (((1,), (1,)), ((), ()))
_TN = (((0,), (0,)), ((), ()))


def _rms(x):
    return x * lax.rsqrt(jnp.mean(x * x, axis=-1, keepdims=True) + EPS)


def _dot(a, b, dims=None):
    if dims is None:
        return jnp.dot(a, b, preferred_element_type=F32)
    return lax.dot_general(a, b, dims, preferred_element_type=F32)


def _resident(shape):
    return pl.BlockSpec(shape, lambda *_: (0,) * len(shape), pipeline_mode=pl.Buffered(1))


def _params(*sem, vmem_limit=VMEM_LIMIT):
    return pltpu.CompilerParams(dimension_semantics=sem, vmem_limit_bytes=vmem_limit)


def _mod_kernel(c_ref, w_ref, b_ref, o_ref):
    c = c_ref[...]
    a = (c * jax.nn.sigmoid(c)).astype(BF16)
    o_ref[...] = _dot(a, w_ref[...].astype(BF16)) + b_ref[...]


def _modulation(c_all, w_ada, b_ada):
    m, d = c_all.shape
    n = w_ada.shape[1]
    tn = 1024
    return pl.pallas_call(
        _mod_kernel,
        out_shape=jax.ShapeDtypeStruct((m, n), F32),
        grid=(n // tn,),
        in_specs=[pl.BlockSpec((m, d), lambda j: (0, 0)),
                  pl.BlockSpec((d, tn), lambda j: (0, j)),
                  pl.BlockSpec((1, tn), lambda j: (0, j))],
        out_specs=pl.BlockSpec((m, tn), lambda j: (0, j)),
        compiler_params=_params("arbitrary"),
        name="mod",
    )(c_all, w_ada, b_ada.reshape(1, n))


def _mla_in_kernel(x_ref, sh_ref, sc_ref, cr_ref, sr_ref, wa_ref, wkpe_ref, wuq_ref, wukv_ref,
                   gq_ref, gkv_ref, q_ref, ckv_ref, kpe_ref, *kv_refs, pos_base, pos_period):
    x = x_ref[...]
    tm = x.shape[0]
    h = (_rms(x) * (1.0 + sc_ref[...]) + sh_ref[...]).astype(BF16)

    half = QK_ROPE // 2
    lane = lax.broadcasted_iota(jnp.int32, (8, LANES), 1)
    base = (pos_base + lax.rem(pl.program_id(0) * tm, pos_period)).astype(F32)
    ang_b = base * jnp.exp(-math.log(ROPE_BASE) * lax.rem(lane, half).astype(F32) / half)
    cb, sb = jnp.cos(ang_b)[0:1], jnp.sin(ang_b)[0:1]
    cr, sr = cr_ref[...], sr_ref[...]
    cos = cr * cb - sr * sb
    sin = sr * cb + cr * sb

    def rope(t):
        return t * cos + pltpu.roll(t, QK_ROPE, 1) * sin

    a = _dot(h, wa_ref[...], _NT)
    cqn = (_rms(a[:, :Q_RANK]) * gq_ref[...]).astype(BF16)
    ckvn = _rms(a[:, Q_RANK:]) * gkv_ref[...]
    ckv_ref[...] = ckvn

    q = _dot(cqn, wuq_ref[...])
    for hd in range(N_HEADS):
        lo = hd * HEAD_PACK
        q_ref[:, lo:lo + QK_NOPE] = q[:, lo:lo + QK_NOPE].astype(BF16)
        q_ref[:, lo + QK_NOPE:lo + HEAD_PACK] = rope(q[:, lo + QK_NOPE:lo + HEAD_PACK]).astype(BF16)

    kp = rope(_dot(h, wkpe_ref[...], _NT))
    kpe_ref[...] = kp[:, :QK_ROPE]

    if kv_refs:
        kf_ref, v_ref = kv_refs
        kv = _dot(ckvn.astype(BF16), wukv_ref[...])
        kpb = kp.astype(BF16)
        for hd in range(N_HEADS):
            lo = hd * HEAD_PACK
            kf_ref[:, lo:lo + QK_NOPE] = kv[:, hd * QK_NOPE:(hd + 1) * QK_NOPE].astype(BF16)
            kf_ref[:, lo + QK_NOPE:lo + HEAD_PACK] = kpb
            vo = N_HEADS * QK_NOPE + hd * V_HEAD
            v_ref[:, lo:lo + V_HEAD] = kv[:, vo:vo + V_HEAD].astype(BF16)
            v_ref[:, lo + V_HEAD:lo + HEAD_PACK] = jnp.ones((kv.shape[0], HEAD_PACK - V_HEAD), BF16)


def _mla_in(x, sh, sc, w, tm, with_kv, pos_base, pos_period):
    t, d = x.shape
    n_tiles = t // tm
    tiles_per_group = n_tiles // sh.shape[0]
    r = sh.shape[1]
    row = lambda n: pl.BlockSpec((tm, n), lambda i: (i, 0))
    mod = pl.BlockSpec((None, r, d), lambda i: (i // tiles_per_group, 0, 0))
    packed = N_HEADS * HEAD_PACK
    half = QK_ROPE // 2
    lane = jnp.arange(LANES)
    inv_freq = jnp.exp(-math.log(ROPE_BASE) * (lane % half).astype(F32) / half)
    ang = (jnp.arange(tm) % min(tm, pos_period)).astype(F32)[:, None] * inv_freq[None, :]
    cr = jnp.where(lane < QK_ROPE, jnp.cos(ang), 0.0)
    sr = jnp.where(lane < QK_ROPE, jnp.sin(ang), 0.0)
    out_shape = [jax.ShapeDtypeStruct((t, packed), BF16),
                 jax.ShapeDtypeStruct((t, KV_RANK), F32),
                 jax.ShapeDtypeStruct((t, QK_ROPE), F32)]
    out_specs = [row(packed), row(KV_RANK), row(QK_ROPE)]
    if with_kv:
        out_shape += [jax.ShapeDtypeStruct((t, packed), BF16),
                      jax.ShapeDtypeStruct((t, packed), BF16)]
        out_specs += [row(packed), row(packed)]
    return pl.pallas_call(
        functools.partial(_mla_in_kernel, pos_base=pos_base, pos_period=pos_period),
        out_shape=out_shape,
        grid=(n_tiles,),
        in_specs=[row(d), mod, mod, _resident((tm, LANES)), _resident((tm, LANES)),
                  _resident(w["wa"].shape), _resident(w["wkpe"].shape),
                  _resident(w["wuq"].shape), _resident(w["wukv"].shape),
                  _resident((1, Q_RANK)), _resident((1, KV_RANK))],
        out_specs=out_specs,
        compiler_params=_params("arbitrary"),
        name="mla_in",
    )(x, sh, sc, cr, sr, w["wa"], w["wkpe"], w["wuq"], w["wukv"], w["gq"], w["gkv"])


def _hg_in_kernel(x_ref, sh_ref, sc_ref, wh_ref, lb_ref, *refs):
    qh_ref, kh_ref, vh_ref, gate_ref, g_ref = refs[-5:]
    if len(refs) > 5:
        w1_ref, w2_ref, w1b_ref, w2b_ref = refs[:4]
        w1b_ref[...] = w1_ref[...].astype(BF16)
        w2b_ref[...] = w2_ref[...].astype(BF16)
    x = x_ref[...]
    h = (_rms(x) * (1.0 + sc_ref[...]) + sh_ref[...]).astype(BF16)
    n = HG_WIDTH
    hq = _dot(h, wh_ref[0:n, :], _NT)
    qh_ref[...] = (hq * jax.nn.sigmoid(hq) * (HG_DIM ** -0.5)).astype(BF16)
    hf = _dot(h, wh_ref[n:2 * n, :], _NT)
    lb = lb_ref[...]
    f = lb + (1.0 - lb) * jax.nn.sigmoid(hf)
    kh_ref[...] = (1.0 - f).astype(BF16)
    g_ref[...] = jnp.log2(f)
    vh_ref[...] = _dot(h, wh_ref[2 * n:3 * n, :], _NT).astype(BF16)
    hg = _dot(h, wh_ref[3 * n:4 * n, :], _NT)
    gate_ref[...] = (hg * jax.nn.sigmoid(hg)).astype(BF16)


def _hg_in(x, sh, sc, w, tm, cast=()):
    t, d = x.shape
    n_tiles = t // tm
    tiles_per_group = n_tiles // sh.shape[0]
    r = sh.shape[1]
    row = lambda n: pl.BlockSpec((tm, n), lambda i: (i, 0))
    mod = pl.BlockSpec((None, r, d), lambda i: (i // tiles_per_group, 0, 0))
    half = jax.ShapeDtypeStruct((t, HG_WIDTH), BF16)
    cast_specs = [pl.BlockSpec((m.shape[0] // n_tiles, m.shape[1]), lambda i: (i, 0)) for m in cast]
    return pl.pallas_call(
        _hg_in_kernel,
        out_shape=[jax.ShapeDtypeStruct(m.shape, BF16) for m in cast]
        + [half, half, half, half, jax.ShapeDtypeStruct((t, HG_WIDTH), F32)],
        grid=(n_tiles,),
        in_specs=[row(d), mod, mod, _resident(w["wh"].shape), _resident((1, HG_WIDTH))] + cast_specs,
        out_specs=cast_specs + [row(HG_WIDTH)] * 5,
        compiler_params=_params("arbitrary"),
        name="hg_in",
    )(x, sh, sc, w["wh"], w["lb"], *cast)


def _attn_kernel(q_ref, qn_ref, k_ref, v_ref, o_ref, s0_ref, s1_ref, mx0_ref, mx1_ref, m_ref, acc_ref,
                 *, tk, nh):
    qi = pl.program_id(2)
    diag_ok = (lax.broadcasted_iota(jnp.int32, (tk, tk), 1) // CHUNK
               <= lax.broadcasted_iota(jnp.int32, (tk, tk), 0) // CHUNK)
    streams = [(hd, r) for hd in range(nh) for r in range(2)]
    sid = lambda hd, r: 2 * hd + r

    def kv_rows(blk):
        return pl.ds(pl.multiple_of(blk * tk, tk), tk)

    def rows(r):
        return slice(r * tk, (r + 1) * tk)

    def head(hd):
        return slice(hd * HEAD_PACK, (hd + 1) * HEAD_PACK)

    def lanes(x, n):
        return jnp.concatenate([x] * (n // LANES), axis=1)

    def row_max(s):
        return jnp.broadcast_to(jnp.max(s, axis=-1, keepdims=True), (tk, LANES))

    def scores(q, hd, r, blk):
        return _dot(q[rows(r), head(hd)], k_ref[kv_rows(blk), head(hd)], _NT)

    def produce(q, hd, r, blk, s_ref, mx_ref):
        s = scores(q, hd, r, blk)
        s_ref[sid(hd, r)] = s
        mx_ref[sid(hd, r)] = row_max(s)

    def block_softmax(s, mx, hd, blk):
        p = jnp.exp2(s - lanes(mx, tk)).astype(BF16)
        return mx, _dot(p, v_ref[kv_rows(blk), head(hd)])

    def merge(a, b):
        (ma, acc_a), (mb, acc_b) = a, b
        m = jnp.maximum(ma, mb)
        return m, (lanes(jnp.exp2(ma - m), HEAD_PACK) * acc_a
                   + lanes(jnp.exp2(mb - m), HEAD_PACK) * acc_b)

    def update(hd, r, s, mx, blk):
        i = sid(hd, r)
        m_old = m_ref[i]
        m_new = jnp.maximum(m_old, mx)
        m_ref[i] = m_new
        p = jnp.exp2(s - lanes(m_new, tk)).astype(BF16)
        acc_ref[i] = (lanes(jnp.exp2(m_old - m_new), HEAD_PACK) * acc_ref[i]
                      + _dot(p, v_ref[kv_rows(blk), head(hd)]))

    @pl.when(qi == 0)
    def _():
        for hd, r in streams:
            produce(q_ref, hd, r, 0, s0_ref, mx0_ref)

    for i in range(2 * nh):
        m_ref[i] = jnp.full((tk, LANES), -1e30, F32)
        acc_ref[i] = jnp.zeros((tk, HEAD_PACK), F32)

    def body(j, carry):
        blk = 2 * j
        for (cur, cmx), (nxt, nmx) in (((s0_ref, mx0_ref), (s1_ref, mx1_ref)),
                                       ((s1_ref, mx1_ref), (s0_ref, mx0_ref))):
            for hd, r in streams:
                produce(q_ref, hd, r, blk + 1, nxt, nmx)
            for hd, r in streams:
                update(hd, r, cur[sid(hd, r)], cmx[sid(hd, r)], blk)
            blk = blk + 1
        return carry

    lax.fori_loop(0, qi, body, 0)

    for hd in range(nh):
        top, bot = sid(hd, 0), sid(hd, 1)
        d0 = jnp.where(diag_ok, s0_ref[top], -1e30)
        d1 = jnp.where(diag_ok, scores(q_ref, hd, 1, 2 * qi + 1), -1e30)
        part_a = block_softmax(d0, row_max(d0), hd, 2 * qi)
        part_b = block_softmax(s0_ref[bot], mx0_ref[bot], hd, 2 * qi)
        part_c = block_softmax(d1, row_max(d1), hd, 2 * qi + 1)
        finals = [merge((m_ref[top], acc_ref[top]), part_a),
                  merge(merge((m_ref[bot], acc_ref[bot]), part_b), part_c)]
        for r in range(2):
            acc = finals[r][1]
            o_ref[rows(r), hd * V_HEAD:(hd + 1) * V_HEAD] = (acc[:, :V_HEAD] / acc[:, V_HEAD:]).astype(BF16)
    for hd, r in streams:
        produce(qn_ref, hd, r, 0, s0_ref, mx0_ref)


def _attention(q, kf, vp, batch, seq, tk, nh):
    tq = 2 * tk
    nq = seq // tq
    return pl.pallas_call(
        functools.partial(_attn_kernel, tk=tk, nh=nh),
        out_shape=jax.ShapeDtypeStruct((batch * seq, N_HEADS * V_HEAD), BF16),
        grid=(batch, N_HEADS // nh, nq),
        in_specs=[pl.BlockSpec((tq, nh * HEAD_PACK), lambda b, h, i: (b * nq + i, h)),
                  pl.BlockSpec((tq, nh * HEAD_PACK), lambda b, h, i: (b * nq + jnp.minimum(i + 1, nq - 1), h)),
                  pl.BlockSpec((seq, nh * HEAD_PACK), lambda b, h, i: (b, h)),
                  pl.BlockSpec((seq, nh * HEAD_PACK), lambda b, h, i: (b, h))],
        out_specs=pl.BlockSpec((tq, nh * V_HEAD), lambda b, h, i: (b * nq + i, h)),
        scratch_shapes=[pltpu.VMEM((2 * nh, tk, tk), F32), pltpu.VMEM((2 * nh, tk, tk), F32),
                        pltpu.VMEM((2 * nh, tk, LANES), F32), pltpu.VMEM((2 * nh, tk, LANES), F32),
                        pltpu.VMEM((2 * nh, tk, LANES), F32), pltpu.VMEM((2 * nh, tk, HEAD_PACK), F32)],
        compiler_params=_params("arbitrary", "arbitrary", "arbitrary"),
        name="attn",
    )(q, q, kf, vp)


def _qlat_kernel(qn_ref, qp_ref, wuk_ref, ql_ref, qpe_ref):
    nb = ql_ref.shape[0]
    ql = _dot(qn_ref[...], wuk_ref[...], _NT)
    ql_ref[...] = ql.reshape(nb, -1, KV_RANK)
    qpe_ref[...] = qp_ref[...].astype(F32).reshape(nb, -1, LANES)


def _q_latent(q, wuk_t, nb, ln):
    t = q.shape[0]
    return pl.pallas_call(
        _qlat_kernel,
        out_shape=[jax.ShapeDtypeStruct((nb, N_HEADS, ln, KV_RANK), F32),
                   jax.ShapeDtypeStruct((nb, N_HEADS, ln, LANES), F32)],
        grid=(N_HEADS,),
        in_specs=[pl.BlockSpec((t, QK_NOPE), lambda h: (0, 2 * h)),
                  pl.BlockSpec((t, QK_NOPE), lambda h: (0, 2 * h + 1)),
                  pl.BlockSpec((None, KV_RANK, QK_NOPE), lambda h: (h, 0, 0))],
        out_specs=[pl.BlockSpec((nb, None, ln, KV_RANK), lambda h: (0, h, 0, 0)),
                   pl.BlockSpec((nb, None, ln, LANES), lambda h: (0, h, 0, 0))],
        compiler_params=_params("arbitrary"),
        name="q_latent",
    )(q, q, wuk_t)


def _dec_attn_kernel(ql_ref, qpe_ref, cckv_ref, ckpe_ref, nckv_ref, nkpe_ref, wuv_ref, o_ref, *, ln):
    rows = N_HEADS * ln
    ql = ql_ref[...].reshape(rows, KV_RANK).astype(BF16)
    qp = qpe_ref[...].reshape(rows, LANES)[:, :QK_ROPE].astype(BF16)
    ckc = cckv_ref[...].astype(BF16)
    kpc = ckpe_ref[...].astype(BF16)
    ckn = nckv_ref[...].astype(BF16)
    kpn = nkpe_ref[...].astype(BF16)
    s_c = _dot(ql, ckc, _NT) + _dot(qp, kpc)
    s_n = _dot(ql, ckn, _NT) + _dot(qp, kpn, _NT)
    m = jnp.maximum(jnp.max(s_c, axis=-1, keepdims=True), jnp.max(s_n, axis=-1, keepdims=True))
    p_c = jnp.exp2(s_c - m)
    p_n = jnp.exp2(s_n - m)
    l = jnp.sum(p_c, axis=-1, keepdims=True) + jnp.sum(p_n, axis=-1, keepdims=True)
    ctx = (_dot(p_c.astype(BF16), ckc) + _dot(p_n.astype(BF16), ckn)) / l
    ctx = ctx.astype(BF16)
    for hd in range(N_HEADS):
        o_ref[:, hd * V_HEAD:(hd + 1) * V_HEAD] = _dot(
            ctx[hd * ln:(hd + 1) * ln], wuv_ref[hd]).astype(BF16)


def _dec_attention(ql, qpe, cache_ckv, cache_kpe_t, ckv_new, kpe_new, wuv_t, ln):
    nb, past = cache_ckv.shape[0], cache_ckv.shape[1]
    return pl.pallas_call(
        functools.partial(_dec_attn_kernel, ln=ln),
        out_shape=jax.ShapeDtypeStruct((nb * ln, N_HEADS * V_HEAD), BF16),
        grid=(nb,),
        in_specs=[pl.BlockSpec((None, N_HEADS, ln, KV_RANK), lambda b: (b, 0, 0, 0)),
                  pl.BlockSpec((None, N_HEADS, ln, LANES), lambda b: (b, 0, 0, 0)),
                  pl.BlockSpec((None, past, KV_RANK), lambda b: (b, 0, 0)),
                  pl.BlockSpec((None, QK_ROPE, past), lambda b: (b, 0, 0)),
                  pl.BlockSpec((ln, KV_RANK), lambda b: (b, 0)),
                  pl.BlockSpec((ln, QK_ROPE), lambda b: (b, 0)),
                  _resident(wuv_t.shape)],
        out_specs=pl.BlockSpec((ln, N_HEADS * V_HEAD), lambda b: (b, 0)),
        compiler_params=_params("arbitrary"),
        name="dec_attn",
    )(ql, qpe, cache_ckv, cache_kpe_t, ckv_new, kpe_new, wuv_t)


def _gla_consts(ln, width):
    row = lax.broadcasted_iota(jnp.int32, (8, width), 0)
    t_idx = lax.broadcasted_iota(jnp.int32, (ln, ln), 0)
    s_idx = lax.broadcasted_iota(jnp.int32, (ln, ln), 1)
    level_mask = []
    for lev in range(int(math.log2(ln))):
        level_mask.append((((t_idx ^ s_idx) >> (lev + 1)) == 0)
                          & ((t_idx & (1 << lev)) != 0) & ((s_idx & (1 << lev)) == 0))
    return {
        "scan": [row >= sft for sft in (1, 2, 4)],
        "odd": (row & 1).astype(F32),
        "low4": (row & 4) == 0,
        "sign1": jnp.where((row & 2) != 0, 1.0, -1.0).astype(F32),
        "sign2": jnp.where((row & 4) != 0, 1.0, -1.0).astype(F32),
        "level_mask": level_mask,
    }


def _gla_chunk(qb, kb, vb, g, st_ref, loc_ref, cst, ln):
    nv = ln // 8
    width = g.shape[1]
    grp = lambda x: [x[8 * i:8 * i + 8, :] for i in range(nv)]
    cat = lambda xs: jnp.concatenate(xs, axis=0)
    head = lambda x, hd: x[:, hd * HG_DIM:(hd + 1) * HG_DIM]
    q, k = grp(qb.astype(F32)), grp(kb.astype(F32))
    gs = grp(g)

    loc = gs
    for sft, keep in zip((1, 2, 4), cst["scan"]):
        loc = [x + jnp.where(keep, pltpu.roll(x, sft, 0), 0.0) for x in loc]
    loc_ref[...] = cat(loc)
    bcast = lambda r: jnp.broadcast_to(loc_ref[r:r + 1, :], (8, width))
    tot = [bcast(8 * i + 7) for i in range(nv)]
    pin = [tot[0]]
    for i in range(1, nv):
        pin.append(pin[-1] + tot[i])
    b = [loc[0]] + [loc[i] + pin[i - 1] for i in range(1, nv)]
    b_last = pin[-1]

    exps = [[gs[i] * cst["odd"] for i in range(nv)],
            [(loc[i] - jnp.where(cst["low4"], bcast(8 * i + 1), bcast(8 * i + 5))) * cst["sign1"]
             for i in range(nv)],
            [(loc[i] - bcast(8 * i + 3)) * cst["sign2"] for i in range(nv)]]
    upper = [[True] * nv, [True] * nv, [True] * nv]
    lower = [[True] * nv, [True] * nv, [True] * nv]
    for lev in range(3, int(math.log2(ln))):
        m = 1 << (lev - 3)
        e, up, lo = [], [], []
        for i in range(nv):
            mid = (i // (2 * m)) * 2 * m + m - 1
            is_up = (i // m) % 2 == 1
            if is_up:
                e.append(loc[i] if mid == i - 1 else b[i] - pin[mid])
            else:
                e.append(tot[i] - loc[i] if mid == i else pin[mid] - b[i])
            up.append(is_up)
            lo.append(not is_up)
        exps.append(e)
        upper.append(up)
        lower.append(lo)

    a = [jnp.zeros((ln, ln), F32)] * HG_HEADS
    for lev, e in enumerate(exps):
        w = [jnp.exp2(x) for x in e]
        qt = cat([q[i] * w[i] if upper[lev][i] else q[i] for i in range(nv)]).astype(BF16)
        kt = cat([k[i] * w[i] if lower[lev][i] else k[i] for i in range(nv)]).astype(BF16)
        a = [jnp.where(cst["level_mask"][lev], _dot(head(qt, hd), head(kt, hd), _NT), a[hd])
             for hd in range(HG_HEADS)]

    qd = cat([q[i] * jnp.exp2(b[i]) for i in range(nv)]).astype(BF16)
    kd = cat([k[i] * jnp.exp2(b_last - b[i]) for i in range(nv)]).astype(BF16)
    qk = cat([q[i] * k[i] for i in range(nv)])
    vf = vb.astype(F32)
    decay = jnp.exp2(b_last[0:1, :])
    outs = []
    for hd in range(HG_HEADS):
        st = st_ref[hd]
        v_h = head(vb, hd)
        o = _dot(head(qd, hd), st.astype(BF16), _NT) + _dot(a[hd].astype(BF16), v_h)
        outs.append(o + jnp.sum(head(qk, hd), axis=-1, keepdims=True) * head(vf, hd))
        st_ref[hd] = st * head(decay, hd) + _dot(v_h, head(kd, hd), _TN)
    return outs


def _hgrn_kernel(*refs, ln, n_chunks, has_init):
    if has_init:
        qh_ref, kh_ref, vh_ref, gate_ref, g_ref, gw_ref, s0_ref, o_ref, sout_ref, st_ref, b_ref = refs
    else:
        qh_ref, kh_ref, vh_ref, gate_ref, g_ref, gw_ref, o_ref, sout_ref, st_ref, b_ref = refs
        s0_ref = None
    si = pl.program_id(1)

    @pl.when(si == 0)
    def _():
        for hd in range(HG_HEADS):
            st_ref[hd] = s0_ref[hd].T if has_init else jnp.zeros((HG_DIM, HG_DIM), F32)

    gw = gw_ref[...]
    cst = _gla_consts(ln, HG_WIDTH)

    def body(c, carry):
        rows = pl.ds(pl.multiple_of(c * ln, ln), ln)
        outs = _gla_chunk(qh_ref[rows, :], kh_ref[rows, :], vh_ref[rows, :], g_ref[rows, :],
                          st_ref, b_ref, cst, ln)
        for hd in range(HG_HEADS):
            lanes = slice(hd * HG_DIM, (hd + 1) * HG_DIM)
            o_ref[rows, lanes] = (_rms(outs[hd]) * gw * gate_ref[rows, lanes].astype(F32)).astype(BF16)
        return carry

    lax.fori_loop(0, n_chunks, body, 0, unroll=2 if n_chunks % 2 == 0 else 1)

    @pl.when(si == pl.num_programs(1) - 1)
    def _():
        for hd in range(HG_HEADS):
            sout_ref[hd] = st_ref[hd].T


def _hgrn(qh, kh, vh, gate, g, gw, state0, batch, seq, ln, ts):
    ns = seq // ts
    has_init = state0 is not None
    row = pl.BlockSpec((ts, HG_WIDTH), lambda b, s: (b * ns + s, 0))
    st_spec = pl.BlockSpec((None, HG_HEADS, HG_DIM, HG_DIM), lambda b, s: (b, 0, 0, 0))
    in_specs = [row] * 5 + [_resident((1, HG_DIM))]
    args = [qh, kh, vh, gate, g, gw]
    if has_init:
        in_specs.append(st_spec)
        args.append(state0)
    return pl.pallas_call(
        functools.partial(_hgrn_kernel, ln=ln, n_chunks=ts // ln, has_init=has_init),
        out_shape=[jax.ShapeDtypeStruct((batch * seq, HG_WIDTH), BF16),
                   jax.ShapeDtypeStruct((batch, HG_HEADS, HG_DIM, HG_DIM), F32)],
        grid=(batch, ns),
        in_specs=in_specs,
        out_specs=[row, st_spec],
        scratch_shapes=[pltpu.VMEM((HG_HEADS, HG_DIM, HG_DIM), F32),
                        pltpu.VMEM((ln, HG_WIDTH), F32)],
        compiler_params=_params("arbitrary", "arbitrary"),
        name="hgrn",
    )(*args)


def _out_proj_kernel(x_ref, gt_ref, om_ref, oh_ref, w_ref, o_ref):
    n = om_ref.shape[1]
    mix = _dot(om_ref[...], w_ref[0:n, :]) + _dot(oh_ref[...], w_ref[n:, :])
    o_ref[...] = x_ref[...] + gt_ref[...] * mix


def _out_proj(x, gt, o_mla, o_hg, w_out, tm):
    t, d = x.shape
    n_tiles = t // tm
    tiles_per_group = n_tiles // gt.shape[0]
    r = gt.shape[1]
    row = lambda n: pl.BlockSpec((tm, n), lambda i: (i, 0))
    return pl.pallas_call(
        _out_proj_kernel,
        out_shape=jax.ShapeDtypeStruct((t, d), F32),
        grid=(n_tiles,),
        in_specs=[row(d), pl.BlockSpec((None, r, d), lambda i: (i // tiles_per_group, 0, 0)),
                  row(o_mla.shape[1]), row(o_hg.shape[1]), _resident(w_out.shape)],
        out_specs=row(d),
        compiler_params=_params("arbitrary"),
        name="out_proj",
    )(x, gt, o_mla, o_hg, w_out)


def _ffn_kernel(x_ref, sh_ref, sc_ref, gt_ref, w1_ref, w2_ref, gf_ref, o_ref, h_ref):
    f = pl.program_id(1)
    last = pl.num_programs(1) - 1

    def mlp(h):
        a = jnp.maximum(_dot(h, w1_ref[...]), 0.0)
        return _dot((a * a).astype(BF16), w2_ref[...])

    @pl.when(f == 0)
    def _():
        h = (_rms(x_ref[...]) * (1.0 + sc_ref[...]) + sh_ref[...]).astype(BF16)
        h_ref[...] = h
        o_ref[...] = mlp(h)

    @pl.when(jnp.logical_and(f > 0, f < last))
    def _():
        o_ref[...] += mlp(h_ref[...])

    @pl.when(f == last)
    def _():
        x2 = x_ref[...] + gt_ref[...] * (o_ref[...] + mlp(h_ref[...]))
        o_ref[...] = _rms(x2) * gf_ref[...]


def _ffn(x, sh, sc, gt, w1, w2, gf, tm, tf):
    t, d = x.shape
    dff = w1.shape[1]
    n_tiles = t // tm
    tiles_per_group = n_tiles // sh.shape[0]
    r = sh.shape[1]
    row = pl.BlockSpec((tm, d), lambda i, f: (i, 0))
    mod = pl.BlockSpec((None, r, d), lambda i, f: (i // tiles_per_group, 0, 0))
    return pl.pallas_call(
        _ffn_kernel,
        out_shape=jax.ShapeDtypeStruct((t, d), F32),
        grid=(n_tiles, dff // tf),
        in_specs=[row, mod, mod, mod,
                  pl.BlockSpec((d, tf), lambda i, f: (0, f)),
                  pl.BlockSpec((tf, d), lambda i, f: (f, 0)),
                  pl.BlockSpec((1, d), lambda i, f: (0, 0))],
        out_specs=row,
        scratch_shapes=[pltpu.VMEM((tm, d), BF16)],
        compiler_params=_params("arbitrary", "arbitrary", vmem_limit=FFN_VMEM_LIMIT),
        name="ffn",
    )(x, sh, sc, gt, w1, w2, gf)


def _rot_half_cols(w, axis=-1):
    lo, hi = jnp.split(w, 2, axis=axis)
    return jnp.concatenate([-hi, lo], axis=axis)


def _prep_weights(w_in, w_uq, g_q, g_kv, w_uk, w_uv, lb, g_hgrn, w_out, w_ff1, w_ff2, g_final):
    d = w_in.shape[0]
    o_kpe = Q_RANK + KV_RANK
    o_hg = o_kpe + QK_ROPE
    w_in_t = jnp.swapaxes(w_in, 0, 1)
    w_kpe_t = w_in_t[o_kpe:o_hg]
    q_scale = ATTN_SCALE * math.log2(math.e)
    wq = w_uq.reshape(Q_RANK, N_HEADS, QK_NOPE + QK_ROPE) * q_scale
    wq_pe = wq[..., QK_NOPE:]
    wuq = jnp.concatenate([wq[..., :QK_NOPE], wq_pe, _rot_half_cols(wq_pe)], axis=-1)
    return {
        "wa": w_in_t[:o_kpe].astype(BF16),
        "wkpe": jnp.concatenate([w_kpe_t, _rot_half_cols(w_kpe_t, axis=0)], axis=0).astype(BF16),
        "wh": w_in_t[o_hg:].astype(BF16),
        "wuq": wuq.reshape(Q_RANK, N_HEADS * HEAD_PACK).astype(BF16),
        "wukv": jnp.concatenate([w_uk.reshape(KV_RANK, -1), w_uv.reshape(KV_RANK, -1)], axis=-1).astype(BF16),
        "wuk_t": jnp.transpose(w_uk, (1, 0, 2)).astype(BF16),
        "wuv_t": jnp.transpose(w_uv, (1, 0, 2)).astype(BF16),
        "gq": g_q.reshape(1, -1), "gkv": g_kv.reshape(1, -1),
        "lb": lb.reshape(1, -1), "ghg": g_hgrn.reshape(1, -1),
        "wout": w_out.astype(BF16), "w1_f32": w_ff1, "w2_f32": w_ff2,
        "gfin": g_final.reshape(1, d),
    }


def _layer(x, mods, w, batch, seq, cache, past):
    sh1, sc1, gt1, sh2, sc2, gt2 = mods
    t = x.shape[0]
    tm = min(512, t)
    prompt = cache is None
    if prompt:
        q, ckv, kpe, kf, v = _mla_in(x, sh1, sc1, w, tm, True, past, seq)
        o_mla = _attention(q, kf, v, batch, seq, 512, 2)
    else:
        cache_ckv, cache_kpe, state0 = cache
        q, ckv, kpe = _mla_in(x, sh1, sc1, w, tm, False, past, seq)
        ql, qpe = _q_latent(q, w["wuk_t"], batch, seq)
        o_mla = _dec_attention(ql, qpe, cache_ckv, cache_kpe, ckv, kpe, w["wuv_t"], seq)
    if prompt:
        w["w1"], w["w2"], qh, kh, vh, gate, g = _hg_in(x, sh1, sc1, w, tm, (w["w1_f32"], w["w2_f32"]))
        o_hg, s_fin = _hgrn(qh, kh, vh, gate, g, w["ghg"], None, batch, seq, CHUNK, 1024)
    else:
        qh, kh, vh, gate, g = _hg_in(x, sh1, sc1, w, tm)
        o_hg, s_fin = _hgrn(qh, kh, vh, gate, g, w["ghg"], state0, batch, seq, seq, seq)
    x1 = _out_proj(x, gt1, o_mla, o_hg, w["wout"], tm)
    y = _ffn(x1, sh2, sc2, gt2, w["w1"], w["w2"], w["gfin"], min(1024, t), 1024)
    return y, ckv, kpe, s_fin


def kernel(x_prompt, x_sample, c_prompt, c_sample, cache_ckv, cache_kpe, state_hgrn, w_ada, b_ada, w_in, w_uq, g_q, g_kv, w_uk, w_uv, hg_lower_bounds, g_hgrn, w_out, w_ff1, w_ff2, g_final):
    depth = w_ada.shape[0]
    assert depth == 1, "single-layer step"
    bp, sp, d = x_prompt.shape
    bs, ss, _ = x_sample.shape
    past = cache_ckv.shape[2]

    lb_all = jnp.cumsum(jax.nn.softmax(hg_lower_bounds.astype(F32), axis=0), axis=0)
    w = _prep_weights(w_in[0], w_uq[0], g_q[0], g_kv[0], w_uk[0], w_uv[0], lb_all[0], g_hgrn[0],
                      w_out[0], w_ff1[0], w_ff2[0], g_final)

    c_all = jnp.concatenate([c_prompt, c_sample], axis=0)
    n_c = c_all.shape[0]
    c_pad = jnp.pad(c_all, ((0, -n_c % 8), (0, 0)))
    mod = _modulation(c_pad, w_ada[0], b_ada[0])[:n_c]
    mod_p = [m.reshape(bp, 1, d) for m in jnp.split(mod[:bp], 6, axis=-1)]
    mod_s = [jnp.repeat(m, ss, axis=0).reshape(1, bs * ss, d) for m in jnp.split(mod[bp:], 6, axis=-1)]

    yp, ckv_p, kpe_p, st_p = _layer(x_prompt.reshape(bp * sp, d), mod_p, w, bp, sp, None, 0)
    ys, ckv_s, kpe_s, st_s = _layer(x_sample.reshape(bs * ss, d), mod_s, w, bs, ss,
                                    (cache_ckv[0], jnp.swapaxes(cache_kpe[0], 1, 2), state_hgrn[0]), past)
    return (yp.reshape(bp, sp, d), ys.reshape(bs, ss, d),
            ckv_p.reshape(1, bp, sp, KV_RANK), kpe_p.reshape(1, bp, sp, QK_ROPE), st_p[None],
            ckv_s.reshape(1, bs, ss, KV_RANK), kpe_s.reshape(1, bs, ss, QK_ROPE), st_s[None])
```

```python
import functools
import math

import jax
import jax.numpy as jnp
from jax import lax
from jax.experimental import pallas as pl
from jax.experimental.pallas import tpu as pltpu

F32 = jnp.float32
BF16 = jnp.bfloat16

EPS = 1e-6
CHUNK = 64
N_HEADS = 8
QK_NOPE = 128
QK_ROPE = 64
V_HEAD = 128
Q_RANK = 512
KV_RANK = 512
ROPE_BASE = 10000.0
ATTN_SCALE = (QK_NOPE + QK_ROPE) ** -0.5
HG_HEADS = 8
HG_DIM = 128
HG_WIDTH = HG_HEADS * HG_DIM
HEAD_PACK = 2 * QK_NOPE
LANES = 128
VMEM_LIMIT = 56 * 1024 * 1024
FFN_VMEM_LIMIT = 62 * 1024 * 1024

_NT = (((1,), (1,)), ((), ()))
_TN = (((0,), (0,)), ((), ()))


def _rms(x):
    return x * lax.rsqrt(jnp.mean(x * x, axis=-1, keepdims=True) + EPS)


def _dot(a, b, dims=None):
    if dims is None:
        return jnp.dot(a, b, preferred_element_type=F32)
    return lax.dot_general(a, b, dims, preferred_element_type=F32)


def _resident(shape):
    return pl.BlockSpec(shape, lambda *_: (0,) * len(shape), pipeline_mode=pl.Buffered(1))


def _params(*sem, vmem_limit=VMEM_LIMIT):
    return pltpu.CompilerParams(dimension_semantics=sem, vmem_limit_bytes=vmem_limit)


def _mod_kernel(c_ref, w_ref, b_ref, o_ref):
    c = c_ref[...]
    a = (c * jax.nn.sigmoid(c)).astype(BF16)
    o_ref[...] = _dot(a, w_ref[...].astype(BF16)) + b_ref[...]


def _modulation(c_all, w_ada, b_ada):
    m, d = c_all.shape
    n = w_ada.shape[1]
    tn = 1024
    return pl.pallas_call(
        _mod_kernel,
        out_shape=jax.ShapeDtypeStruct((m, n), F32),
        grid=(n // tn,),
        in_specs=[pl.BlockSpec((m, d), lambda j: (0, 0)),
                  pl.BlockSpec((d, tn), lambda j: (0, j)),
                  pl.BlockSpec((1, tn), lambda j: (0, j))],
        out_specs=pl.BlockSpec((m, tn), lambda j: (0, j)),
        compiler_params=_params("arbitrary"),
        name="mod",
    )(c_all, w_ada, b_ada.reshape(1, n))


def _mla_in_kernel(x_ref, sh_ref, sc_ref, cr_ref, sr_ref, wa_ref, wuq_ref, wukv_ref, gq_ref, gkv_ref,
                   *refs, pos_base, pos_period, n_cast, with_kv):
    cast_refs, (q_ref, ckv_ref, kpe_ref), refs = refs[:n_cast], refs[n_cast:n_cast + 3], refs[n_cast + 3:]
    if n_cast:
        rows = cast_refs[0].shape[0]
        for j, src in enumerate(cast_refs):
            refs[-1][j * rows:(j + 1) * rows, :] = src[...].astype(BF16)
    x = x_ref[...]
    tm = x.shape[0]
    h = (_rms(x) * (1.0 + sc_ref[...]) + sh_ref[...]).astype(BF16)

    half = QK_ROPE // 2
    lane = lax.broadcasted_iota(jnp.int32, (8, LANES), 1)
    base = (pos_base + lax.rem(pl.program_id(0) * tm, pos_period)).astype(F32)
    ang_b = base * jnp.exp(-math.log(ROPE_BASE) * lax.rem(lane, half).astype(F32) / half)
    cb, sb = jnp.cos(ang_b)[0:1], jnp.sin(ang_b)[0:1]
    cr, sr = cr_ref[...], sr_ref[...]
    cos = cr * cb - sr * sb
    sin = sr * cb + cr * sb

    def rope(t):
        return t * cos + pltpu.roll(t, QK_ROPE, 1) * sin

    a = _dot(h, wa_ref[...], _NT)
    cqn = (_rms(a[:, :Q_RANK]) * gq_ref[...]).astype(BF16)
    ckvn = _rms(a[:, Q_RANK:Q_RANK + KV_RANK]) * gkv_ref[...]
    ckv_ref[...] = ckvn

    q = _dot(cqn, wuq_ref[...])
    for hd in range(N_HEADS):
        lo = hd * HEAD_PACK
        q_ref[:, lo:lo + QK_NOPE] = q[:, lo:lo + QK_NOPE].astype(BF16)
        q_ref[:, lo + QK_NOPE:lo + HEAD_PACK] = rope(q[:, lo + QK_NOPE:lo + HEAD_PACK]).astype(BF16)

    kp = rope(a[:, Q_RANK + KV_RANK:])
    kpe_ref[...] = kp[:, :QK_ROPE]

    if with_kv:
        kf_ref, v_ref = refs[:2]
        kv = _dot(ckvn.astype(BF16), wukv_ref[...])
        kpb = kp.astype(BF16)
        for hd in range(N_HEADS):
            lo = hd * HEAD_PACK
            kf_ref[:, lo:lo + QK_NOPE] = kv[:, hd * QK_NOPE:(hd + 1) * QK_NOPE].astype(BF16)
            kf_ref[:, lo + QK_NOPE:lo + HEAD_PACK] = kpb
            vo = N_HEADS * QK_NOPE + hd * V_HEAD
            v_ref[:, lo:lo + V_HEAD] = kv[:, vo:vo + V_HEAD].astype(BF16)
            v_ref[:, lo + V_HEAD:lo + HEAD_PACK] = jnp.ones((kv.shape[0], HEAD_PACK - V_HEAD), BF16)


def _mla_in(x, sh, sc, w, tm, with_kv, pos_base, pos_period, cast_src=None):
    t, d = x.shape
    n_tiles = t // tm
    tiles_per_group = n_tiles // sh.shape[0]
    r = sh.shape[1]
    row = lambda n: pl.BlockSpec((tm, n), lambda i: (i, 0))
    mod = pl.BlockSpec((None, r, d), lambda i: (i // tiles_per_group, 0, 0))
    packed = N_HEADS * HEAD_PACK
    half = QK_ROPE // 2
    lane = jnp.arange(LANES)
    inv_freq = jnp.exp(-math.log(ROPE_BASE) * (lane % half).astype(F32) / half)
    ang = (jnp.arange(tm) % min(tm, pos_period)).astype(F32)[:, None] * inv_freq[None, :]
    cr = jnp.where(lane < QK_ROPE, jnp.cos(ang), 0.0)
    sr = jnp.where(lane < QK_ROPE, jnp.sin(ang), 0.0)
    out_shape = [jax.ShapeDtypeStruct((t, packed), BF16),
                 jax.ShapeDtypeStruct((t, KV_RANK), F32),
                 jax.ShapeDtypeStruct((t, QK_ROPE), F32)]
    out_specs = [row(packed), row(KV_RANK), row(QK_ROPE)]
    if with_kv:
        out_shape += [jax.ShapeDtypeStruct((t, packed), BF16),
                      jax.ShapeDtypeStruct((t, packed), BF16)]
        out_specs += [row(packed), row(packed)]
    cast_args, cast_specs = [], []
    if cast_src is not None:
        src, first, n_rows = cast_src
        blk = n_rows // n_tiles
        sub = math.gcd(first, blk)
        for j in range(blk // sub):
            cast_args.append(src)
            cast_specs.append(pl.BlockSpec((sub, d), lambda i, j=j: (first // sub + i * (blk // sub) + j, 0)))
        out_shape.append(jax.ShapeDtypeStruct((n_rows, d), BF16))
        out_specs.append(pl.BlockSpec((blk, d), lambda i: (i, 0)))
    return pl.pallas_call(
        functools.partial(_mla_in_kernel, pos_base=pos_base, pos_period=pos_period,
                          n_cast=len(cast_args), with_kv=with_kv),
        out_shape=out_shape,
        grid=(n_tiles,),
        in_specs=[row(d), mod, mod, _resident((tm, LANES)), _resident((tm, LANES)),
                  _resident(w["wa"].shape), _resident(w["wuq"].shape), _resident(w["wukv"].shape),
                  _resident((1, Q_RANK)), _resident((1, KV_RANK))] + cast_specs,
        out_specs=out_specs,
        compiler_params=_params("arbitrary"),
        name="mla_in",
    )(x, sh, sc, cr, sr, w["wa"], w["wuq"], w["wukv"], w["gq"], w["gkv"], *cast_args)


def _hg_in_kernel(x_ref, sh_ref, sc_ref, wh_ref, lb_ref, *refs):
    qh_ref, kh_ref, vh_ref, gate_ref, g_ref = refs[-5:]
    n_cast = (len(refs) - 5) // 2
    for src, dst in zip(refs[:n_cast], refs[n_cast:2 * n_cast]):
        dst[...] = src[...].astype(BF16)
    x = x_ref[...]
    h = (_rms(x) * (1.0 + sc_ref[...]) + sh_ref[...]).astype(BF16)
    n = HG_WIDTH
    hf = _dot(h, wh_ref[n:2 * n, :], _NT)
    lb = lb_ref[...]
    f = lb + (1.0 - lb) * jax.nn.sigmoid(hf)
    kh_ref[...] = (1.0 - f).astype(BF16)
    g_ref[...] = jnp.log2(f)
    hq = _dot(h, wh_ref[0:n, :], _NT)
    qh_ref[...] = (hq * jax.nn.sigmoid(hq) * (HG_DIM ** -0.5)).astype(BF16)
    hg = _dot(h, wh_ref[3 * n:4 * n, :], _NT)
    gate_ref[...] = (hg * jax.nn.sigmoid(hg)).astype(BF16)
    vh_ref[...] = _dot(h, wh_ref[2 * n:3 * n, :], _NT).astype(BF16)


def _hg_in(x, sh, sc, w, tm, cast=()):
    t, d = x.shape
    n_tiles = t // tm
    tiles_per_group = n_tiles // sh.shape[0]
    r = sh.shape[1]
    row = lambda n: pl.BlockSpec((tm, n), lambda i: (i, 0))
    mod = pl.BlockSpec((None, r, d), lambda i: (i // tiles_per_group, 0, 0))
    half = jax.ShapeDtypeStruct((t, HG_WIDTH), BF16)
    cast_specs = [pl.BlockSpec((m.shape[0] // n_tiles, m.shape[1]), lambda i: (i, 0)) for m in cast]
    return pl.pallas_call(
        _hg_in_kernel,
        out_shape=[jax.ShapeDtypeStruct(m.shape, BF16) for m in cast]
        + [half, half, half, half, jax.ShapeDtypeStruct((t, HG_WIDTH), F32)],
        grid=(n_tiles,),
        in_specs=[row(d), mod, mod, _resident(w["wh"].shape), _resident((1, HG_WIDTH))] + cast_specs,
        out_specs=cast_specs + [row(HG_WIDTH)] * 5,
        compiler_params=_params("arbitrary"),
        name="hg_in",
    )(x, sh, sc, w["wh"], w["lb"], *cast)


def _attn_kernel(q_ref, qn_ref, k_ref, v_ref, o_ref, s0_ref, s1_ref, mx0_ref, mx1_ref, m_ref, acc_ref,
                 *, tk, nh):
    qi = pl.program_id(2)
    diag_ok = (lax.broadcasted_iota(jnp.int32, (tk, tk), 1) // CHUNK
               <= lax.broadcasted_iota(jnp.int32, (tk, tk), 0) // CHUNK)
    streams = [(hd, r) for hd in range(nh) for r in range(2)]
    sid = lambda hd, r: 2 * hd + r

    def kv_rows(blk):
        return pl.ds(pl.multiple_of(blk * tk, tk), tk)

    def rows(r):
        return slice(r * tk, (r + 1) * tk)

    def head(hd):
        return slice(hd * HEAD_PACK, (hd + 1) * HEAD_PACK)

    def lanes(x, n):
        return jnp.concatenate([x] * (n // LANES), axis=1)

    def row_max(s):
        return jnp.broadcast_to(jnp.max(s, axis=-1, keepdims=True), (tk, LANES))

    def scores(q, hd, r, blk):
        return _dot(q[rows(r), head(hd)], k_ref[kv_rows(blk), head(hd)], _NT)

    def produce(q, hd, r, blk, s_ref, mx_ref):
        s = scores(q, hd, r, blk)
        s_ref[sid(hd, r)] = s
        mx_ref[sid(hd, r)] = row_max(s)

    def block_softmax(s, mx, hd, blk):
        p = jnp.exp2(s - lanes(mx, tk)).astype(BF16)
        return mx, _dot(p, v_ref[kv_rows(blk), head(hd)])

    def merge(a, b):
        (ma, acc_a), (mb, acc_b) = a, b
        m = jnp.maximum(ma, mb)
        return m, (lanes(jnp.exp2(ma - m), HEAD_PACK) * acc_a
                   + lanes(jnp.exp2(mb - m), HEAD_PACK) * acc_b)

    def update(hd, r, s, mx, blk):
        i = sid(hd, r)
        m_old = m_ref[i]
        m_new = jnp.maximum(m_old, mx)
        m_ref[i] = m_new
        p = jnp.exp2(s - lanes(m_new, tk)).astype(BF16)
        acc_ref[i] = (lanes(jnp.exp2(m_old - m_new), HEAD_PACK) * acc_ref[i]
                      + _dot(p, v_ref[kv_rows(blk), head(hd)]))

    @pl.when(qi == 0)
    def _():
        for hd, r in streams:
            produce(q_ref, hd, r, 0, s0_ref, mx0_ref)

    for i in range(2 * nh):
        m_ref[i] = jnp.full((tk, LANES), -1e30, F32)
        acc_ref[i] = jnp.zeros((tk, HEAD_PACK), F32)

    def body(j, carry):
        blk = 2 * j
        for (cur, cmx), (nxt, nmx) in (((s0_ref, mx0_ref), (s1_ref, mx1_ref)),
                                       ((s1_ref, mx1_ref), (s0_ref, mx0_ref))):
            for hd, r in streams:
                produce(q_ref, hd, r, blk + 1, nxt, nmx)
            for hd, r in streams:
                update(hd, r, cur[sid(hd, r)], cmx[sid(hd, r)], blk)
            blk = blk + 1
        return carry

    lax.fori_loop(0, qi, body, 0)

    for hd in range(nh):
        top, bot = sid(hd, 0), sid(hd, 1)
        d0 = jnp.where(diag_ok, s0_ref[top], -1e30)
        d1 = jnp.where(diag_ok, scores(q_ref, hd, 1, 2 * qi + 1), -1e30)
        part_a = block_softmax(d0, row_max(d0), hd, 2 * qi)
        part_b = block_softmax(s0_ref[bot], mx0_ref[bot], hd, 2 * qi)
        part_c = block_softmax(d1, row_max(d1), hd, 2 * qi + 1)
        finals = [merge((m_ref[top], acc_ref[top]), part_a),
                  merge(merge((m_ref[bot], acc_ref[bot]), part_b), part_c)]
        for r in range(2):
            acc = finals[r][1]
            o_ref[rows(r), hd * V_HEAD:(hd + 1) * V_HEAD] = (acc[:, :V_HEAD] / acc[:, V_HEAD:]).astype(BF16)
    for hd, r in streams:
        produce(qn_ref, hd, r, 0, s0_ref, mx0_ref)


def _attention(q, kf, vp, batch, seq, tk, nh):
    tq = 2 * tk
    nq = seq // tq
    return pl.pallas_call(
        functools.partial(_attn_kernel, tk=tk, nh=nh),
        out_shape=jax.ShapeDtypeStruct((batch * seq, N_HEADS * V_HEAD), BF16),
        grid=(batch, N_HEADS // nh, nq),
        in_specs=[pl.BlockSpec((tq, nh * HEAD_PACK), lambda b, h, i: (b * nq + i, h)),
                  pl.BlockSpec((tq, nh * HEAD_PACK), lambda b, h, i: (b * nq + jnp.minimum(i + 1, nq - 1), h)),
                  pl.BlockSpec((seq, nh * HEAD_PACK), lambda b, h, i: (b, h)),
                  pl.BlockSpec((seq, nh * HEAD_PACK), lambda b, h, i: (b, h))],
        out_specs=pl.BlockSpec((tq, nh * V_HEAD), lambda b, h, i: (b * nq + i, h)),
        scratch_shapes=[pltpu.VMEM((2 * nh, tk, tk), F32), pltpu.VMEM((2 * nh, tk, tk), F32),
                        pltpu.VMEM((2 * nh, tk, LANES), F32), pltpu.VMEM((2 * nh, tk, LANES), F32),
                        pltpu.VMEM((2 * nh, tk, LANES), F32), pltpu.VMEM((2 * nh, tk, HEAD_PACK), F32)],
        compiler_params=_params("arbitrary", "arbitrary", "arbitrary"),
        name="attn",
    )(q, q, kf, vp)


def _qlat_kernel(qn_ref, qp_ref, wuk_ref, ql_ref, qpe_ref):
    nb = ql_ref.shape[0]
    ql = _dot(qn_ref[...], wuk_ref[...], _NT)
    ql_ref[...] = ql.reshape(nb, -1, KV_RANK)
    qpe_ref[...] = qp_ref[...].astype(F32).reshape(nb, -1, LANES)


def _q_latent(q, wuk_t, nb, ln):
    t = q.shape[0]
    return pl.pallas_call(
        _qlat_kernel,
        out_shape=[jax.ShapeDtypeStruct((nb, N_HEADS, ln, KV_RANK), F32),
                   jax.ShapeDtypeStruct((nb, N_HEADS, ln, LANES), F32)],
        grid=(N_HEADS,),
        in_specs=[pl.BlockSpec((t, QK_NOPE), lambda h: (0, 2 * h)),
                  pl.BlockSpec((t, QK_NOPE), lambda h: (0, 2 * h + 1)),
                  pl.BlockSpec((None, KV_RANK, QK_NOPE), lambda h: (h, 0, 0))],
        out_specs=[pl.BlockSpec((nb, None, ln, KV_RANK), lambda h: (0, h, 0, 0)),
                   pl.BlockSpec((nb, None, ln, LANES), lambda h: (0, h, 0, 0))],
        compiler_params=_params("arbitrary"),
        name="q_latent",
    )(q, q, wuk_t)


def _dec_attn_kernel(ql_ref, qpe_ref, cckv_ref, ckpe_ref, nckv_ref, nkpe_ref, wuv_ref, o_ref, *, ln):
    rows = N_HEADS * ln
    for b in range(ql_ref.shape[0]):
        new = slice(b * ln, (b + 1) * ln)
        ql = ql_ref[b].reshape(rows, KV_RANK).astype(BF16)
        qp = qpe_ref[b].reshape(rows, LANES)[:, :QK_ROPE].astype(BF16)
        ckc = cckv_ref[b].astype(BF16)
        kpc = ckpe_ref[b].astype(BF16)
        ckn = nckv_ref[new, :].astype(BF16)
        kpn = nkpe_ref[new, :].astype(BF16)
        s_c = _dot(ql, ckc, _NT) + _dot(qp, kpc)
        s_n = _dot(ql, ckn, _NT) + _dot(qp, kpn, _NT)
        m = jnp.maximum(jnp.max(s_c, axis=-1, keepdims=True), jnp.max(s_n, axis=-1, keepdims=True))
        p_c = jnp.exp2(s_c - m)
        p_n = jnp.exp2(s_n - m)
        l = jnp.sum(p_c, axis=-1, keepdims=True) + jnp.sum(p_n, axis=-1, keepdims=True)
        ctx = (_dot(p_c.astype(BF16), ckc) + _dot(p_n.astype(BF16), ckn)) / l
        ctx = ctx.astype(BF16)
        for hd in range(N_HEADS):
            o_ref[new, hd * V_HEAD:(hd + 1) * V_HEAD] = _dot(
                ctx[hd * ln:(hd + 1) * ln], wuv_ref[hd]).astype(BF16)


def _dec_attention(ql, qpe, cache_ckv, cache_kpe_t, ckv_new, kpe_new, wuv_t, ln):
    nb, past = cache_ckv.shape[0], cache_ckv.shape[1]
    bs = 2 if nb % 2 == 0 else 1
    return pl.pallas_call(
        functools.partial(_dec_attn_kernel, ln=ln),
        out_shape=jax.ShapeDtypeStruct((nb * ln, N_HEADS * V_HEAD), BF16),
        grid=(nb // bs,),
        in_specs=[pl.BlockSpec((bs, N_HEADS, ln, KV_RANK), lambda b: (b, 0, 0, 0)),
                  pl.BlockSpec((bs, N_HEADS, ln, LANES), lambda b: (b, 0, 0, 0)),
                  pl.BlockSpec((bs, past, KV_RANK), lambda b: (b, 0, 0)),
                  pl.BlockSpec((bs, QK_ROPE, past), lambda b: (b, 0, 0)),
                  pl.BlockSpec((bs * ln, KV_RANK), lambda b: (b, 0)),
                  pl.BlockSpec((bs * ln, QK_ROPE), lambda b: (b, 0)),
                  _resident(wuv_t.shape)],
        out_specs=pl.BlockSpec((bs * ln, N_HEADS * V_HEAD), lambda b: (b, 0)),
        compiler_params=_params("arbitrary"),
        name="dec_attn",
    )(ql, qpe, cache_ckv, cache_kpe_t, ckv_new, kpe_new, wuv_t)


def _gla_consts(ln, width):
    row = lax.broadcasted_iota(jnp.int32, (8, width), 0)
    t_idx = lax.broadcasted_iota(jnp.int32, (ln, ln), 0)
    s_idx = lax.broadcasted_iota(jnp.int32, (ln, ln), 1)
    level_mask = []
    for lev in range(int(math.log2(ln))):
        level_mask.append((((t_idx ^ s_idx) >> (lev + 1)) == 0)
                          & ((t_idx & (1 << lev)) != 0) & ((s_idx & (1 << lev)) == 0))
    return {
        "scan": [row >= sft for sft in (1, 2, 4)],
        "odd": (row & 1).astype(F32),
        "low4": (row & 4) == 0,
        "sign1": jnp.where((row & 2) != 0, 1.0, -1.0).astype(F32),
        "sign2": jnp.where((row & 4) != 0, 1.0, -1.0).astype(F32),
        "level_mask": level_mask,
    }


def _gla_chunk(qb, kb, vb, g, st_ref, loc_ref, cst, ln):
    nv = ln // 8
    width = g.shape[1]
    grp = lambda x: [x[8 * i:8 * i + 8, :] for i in range(nv)]
    cat = lambda xs: jnp.concatenate(xs, axis=0)
    head = lambda x, hd: x[:, hd * HG_DIM:(hd + 1) * HG_DIM]
    q, k = grp(qb.astype(F32)), grp(kb.astype(F32))
    gs = grp(g)

    loc = gs
    for sft, keep in zip((1, 2, 4), cst["scan"]):
        loc = [x + jnp.where(keep, pltpu.roll(x, sft, 0), 0.0) for x in loc]
    loc_ref[...] = cat(loc)
    bcast = lambda r: jnp.broadcast_to(loc_ref[r:r + 1, :], (8, width))
    tot = [bcast(8 * i + 7) for i in range(nv)]
    pin = [tot[0]]
    for i in range(1, nv):
        pin.append(pin[-1] + tot[i])
    b = [loc[0]] + [loc[i] + pin[i - 1] for i in range(1, nv)]
    b_last = pin[-1]

    exps = [[gs[i] * cst["odd"] for i in range(nv)],
            [(loc[i] - jnp.where(cst["low4"], bcast(8 * i + 1), bcast(8 * i + 5))) * cst["sign1"]
             for i in range(nv)],
            [(loc[i] - bcast(8 * i + 3)) * cst["sign2"] for i in range(nv)]]
    upper = [[True] * nv, [True] * nv, [True] * nv]
    lower = [[True] * nv, [True] * nv, [True] * nv]
    for lev in range(3, int(math.log2(ln))):
        m = 1 << (lev - 3)
        e, up, lo = [], [], []
        for i in range(nv):
            mid = (i // (2 * m)) * 2 * m + m - 1
            is_up = (i // m) % 2 == 1
            if is_up:
                e.append(loc[i] if mid == i - 1 else b[i] - pin[mid])
            else:
                e.append(tot[i] - loc[i] if mid == i else pin[mid] - b[i])
            up.append(is_up)
            lo.append(not is_up)
        exps.append(e)
        upper.append(up)
        lower.append(lo)

    a = [jnp.zeros((ln, ln), F32)] * HG_HEADS
    for lev, e in enumerate(exps):
        w = [jnp.exp2(x) for x in e]
        qt = cat([q[i] * w[i] if upper[lev][i] else q[i] for i in range(nv)]).astype(BF16)
        kt = cat([k[i] * w[i] if lower[lev][i] else k[i] for i in range(nv)]).astype(BF16)
        a = [jnp.where(cst["level_mask"][lev], _dot(head(qt, hd), head(kt, hd), _NT), a[hd])
             for hd in range(HG_HEADS)]

    qd = cat([q[i] * jnp.exp2(b[i]) for i in range(nv)]).astype(BF16)
    kd = cat([k[i] * jnp.exp2(b_last - b[i]) for i in range(nv)]).astype(BF16)
    qk = cat([q[i] * k[i] for i in range(nv)])
    vf = vb.astype(F32)
    decay = jnp.exp2(b_last[0:1, :])
    outs = []
    for hd in range(HG_HEADS):
        st = st_ref[hd]
        v_h = head(vb, hd)
        o = _dot(head(qd, hd), st.astype(BF16), _NT) + _dot(a[hd].astype(BF16), v_h)
        outs.append(o + jnp.sum(head(qk, hd), axis=-1, keepdims=True) * head(vf, hd))
        st_ref[hd] = st * head(decay, hd) + _dot(v_h, head(kd, hd), _TN)
    return outs


def _hgrn_kernel(*refs, ln, n_chunks, has_init):
    if has_init:
        qh_ref, kh_ref, vh_ref, gate_ref, g_ref, gw_ref, s0_ref, o_ref, sout_ref, st_ref, b_ref = refs
    else:
        qh_ref, kh_ref, vh_ref, gate_ref, g_ref, gw_ref, o_ref, sout_ref, st_ref, b_ref = refs
        s0_ref = None
    si = pl.program_id(1)

    @pl.when(si == 0)
    def _():
        for hd in range(HG_HEADS):
            st_ref[hd] = s0_ref[hd].T if has_init else jnp.zeros((HG_DIM, HG_DIM), F32)

    gw = gw_ref[...]
    cst = _gla_consts(ln, HG_WIDTH)

    def body(c, carry):
        rows = pl.ds(pl.multiple_of(c * ln, ln), ln)
        outs = _gla_chunk(qh_ref[rows, :], kh_ref[rows, :], vh_ref[rows, :], g_ref[rows, :],
                          st_ref, b_ref, cst, ln)
        for hd in range(HG_HEADS):
            lanes = slice(hd * HG_DIM, (hd + 1) * HG_DIM)
            o_ref[rows, lanes] = (_rms(outs[hd]) * gw * gate_ref[rows, lanes].astype(F32)).astype(BF16)
        return carry

    lax.fori_loop(0, n_chunks, body, 0, unroll=2 if n_chunks % 2 == 0 else 1)

    @pl.when(si == pl.num_programs(1) - 1)
    def _():
        for hd in range(HG_HEADS):
            sout_ref[hd] = st_ref[hd].T


def _hgrn(qh, kh, vh, gate, g, gw, state0, batch, seq, ln, ts):
    ns = seq // ts
    has_init = state0 is not None
    row = pl.BlockSpec((ts, HG_WIDTH), lambda b, s: (b * ns + s, 0))
    st_spec = pl.BlockSpec((None, HG_HEADS, HG_DIM, HG_DIM), lambda b, s: (b, 0, 0, 0))
    in_specs = [row] * 5 + [_resident((1, HG_DIM))]
    args = [qh, kh, vh, gate, g, gw]
    if has_init:
        in_specs.append(st_spec)
        args.append(state0)
    return pl.pallas_call(
        functools.partial(_hgrn_kernel, ln=ln, n_chunks=ts // ln, has_init=has_init),
        out_shape=[jax.ShapeDtypeStruct((batch * seq, HG_WIDTH), BF16),
                   jax.ShapeDtypeStruct((batch, HG_HEADS, HG_DIM, HG_DIM), F32)],
        grid=(batch, ns),
        in_specs=in_specs,
        out_specs=[row, st_spec],
        scratch_shapes=[pltpu.VMEM((HG_HEADS, HG_DIM, HG_DIM), F32),
                        pltpu.VMEM((ln, HG_WIDTH), F32)],
        compiler_params=_params("arbitrary", "arbitrary"),
        name="hgrn",
    )(*args)


def _out_proj_kernel(x_ref, gt_ref, om_ref, oh_ref, w_ref, o_ref):
    n = om_ref.shape[1]
    mix = _dot(om_ref[...], w_ref[0:n, :]) + _dot(oh_ref[...], w_ref[n:, :])
    o_ref[...] = x_ref[...] + gt_ref[...] * mix


def _out_proj(x, gt, o_mla, o_hg, w_out, tm):
    t, d = x.shape
    n_tiles = t // tm
    tiles_per_group = n_tiles // gt.shape[0]
    r = gt.shape[1]
    row = lambda n: pl.BlockSpec((tm, n), lambda i: (i, 0))
    return pl.pallas_call(
        _out_proj_kernel,
        out_shape=jax.ShapeDtypeStruct((t, d), F32),
        grid=(n_tiles,),
        in_specs=[row(d), pl.BlockSpec((None, r, d), lambda i: (i // tiles_per_group, 0, 0)),
                  row(o_mla.shape[1]), row(o_hg.shape[1]), _resident(w_out.shape)],
        out_specs=row(d),
        compiler_params=_params("arbitrary"),
        name="out_proj",
    )(x, gt, o_mla, o_hg, w_out)


def _ffn_kernel(x_ref, sh_ref, sc_ref, gt_ref, w1_ref, w2_ref, gf_ref, o_ref, h_ref):
    f = pl.program_id(1)
    last = pl.num_programs(1) - 1

    def mlp(h):
        a = jnp.maximum(_dot(h, w1_ref[...]), 0.0)
        return _dot((a * a).astype(BF16), w2_ref[...])

    @pl.when(f == 0)
    def _():
        h = (_rms(x_ref[...]) * (1.0 + sc_ref[...]) + sh_ref[...]).astype(BF16)
        h_ref[...] = h
        o_ref[...] = mlp(h)

    @pl.when(jnp.logical_and(f > 0, f < last))
    def _():
        o_ref[...] += mlp(h_ref[...])

    @pl.when(f == last)
    def _():
        x2 = x_ref[...] + gt_ref[...] * (o_ref[...] + mlp(h_ref[...]))
        o_ref[...] = _rms(x2) * gf_ref[...]


def _ffn(x, sh, sc, gt, w1, w2, gf, tm, tf):
    t, d = x.shape
    dff = w1.shape[1]
    n_tiles = t // tm
    tiles_per_group = n_tiles // sh.shape[0]
    r = sh.shape[1]
    row = pl.BlockSpec((tm, d), lambda i, f: (i, 0))
    mod = pl.BlockSpec((None, r, d), lambda i, f: (i // tiles_per_group, 0, 0))
    return pl.pallas_call(
        _ffn_kernel,
        out_shape=jax.ShapeDtypeStruct((t, d), F32),
        grid=(n_tiles, dff // tf),
        in_specs=[row, mod, mod, mod,
                  pl.BlockSpec((d, tf), lambda i, f: (0, f)),
                  pl.BlockSpec((tf, d), lambda i, f: (f, 0)),
                  pl.BlockSpec((1, d), lambda i, f: (0, 0))],
        out_specs=row,
        scratch_shapes=[pltpu.VMEM((tm, d), BF16)],
        compiler_params=_params("arbitrary", "arbitrary", vmem_limit=FFN_VMEM_LIMIT),
        name="ffn",
    )(x, sh, sc, gt, w1, w2, gf)


def _rot_half_cols(w, axis=-1):
    lo, hi = jnp.split(w, 2, axis=axis)
    return jnp.concatenate([-hi, lo], axis=axis)


def _prep_weights(w_in, w_uq, g_q, g_kv, w_uk, w_uv, lb, g_hgrn, w_out, w_ff1, w_ff2, g_final):
    d = w_in.shape[0]
    o_kpe = Q_RANK + KV_RANK
    o_hg = o_kpe + QK_ROPE
    w_in_t = jnp.swapaxes(w_in, 0, 1)
    w_kpe_t = w_in_t[o_kpe:o_hg]
    q_scale = ATTN_SCALE * math.log2(math.e)
    wq = w_uq.reshape(Q_RANK, N_HEADS, QK_NOPE + QK_ROPE) * q_scale
    wq_pe = wq[..., QK_NOPE:]
    wuq = jnp.concatenate([wq[..., :QK_NOPE], wq_pe, _rot_half_cols(wq_pe)], axis=-1)
    return {
        "wa": jnp.concatenate([w_in_t[:o_hg], _rot_half_cols(w_kpe_t, axis=0)], axis=0).astype(BF16),
        "wh_src": (w_in_t, o_hg, w_in_t.shape[0] - o_hg),
        "wuq": wuq.reshape(Q_RANK, N_HEADS * HEAD_PACK).astype(BF16),
        "wukv": jnp.concatenate([w_uk.reshape(KV_RANK, -1), w_uv.reshape(KV_RANK, -1)], axis=-1).astype(BF16),
        "wuk_t": jnp.transpose(w_uk, (1, 0, 2)).astype(BF16),
        "wuv_t": jnp.transpose(w_uv, (1, 0, 2)).astype(BF16),
        "gq": g_q.reshape(1, -1), "gkv": g_kv.reshape(1, -1),
        "lb": lb.reshape(1, -1), "ghg": g_hgrn.reshape(1, -1),
        "wout_f32": w_out, "w1_f32": w_ff1, "w2_f32": w_ff2,
        "gfin": g_final.reshape(1, d),
    }


def _layer(x, mods, w, batch, seq, cache, past):
    sh1, sc1, gt1, sh2, sc2, gt2 = mods
    t = x.shape[0]
    tm = min(512, t)
    prompt = cache is None
    if prompt:
        q, ckv, kpe, kf, v, w["wh"] = _mla_in(x, sh1, sc1, w, tm, True, past, seq, w["wh_src"])
        o_mla = _attention(q, kf, v, batch, seq, 512, 2)
    else:
        cache_ckv, cache_kpe, state0 = cache
        q, ckv, kpe = _mla_in(x, sh1, sc1, w, tm, False, past, seq)
        ql, qpe = _q_latent(q, w["wuk_t"], batch, seq)
        o_mla = _dec_attention(ql, qpe, cache_ckv, cache_kpe, ckv, kpe, w["wuv_t"], seq)
    if prompt:
        w["w1"], w["w2"], w["wout"], qh, kh, vh, gate, g = _hg_in(
            x, sh1, sc1, w, tm, (w["w1_f32"], w["w2_f32"], w["wout_f32"]))
        o_hg, s_fin = _hgrn(qh, kh, vh, gate, g, w["ghg"], None, batch, seq, CHUNK, 1024)
    else:
        qh, kh, vh, gate, g = _hg_in(x, sh1, sc1, w, tm)
        o_hg, s_fin = _hgrn(qh, kh, vh, gate, g, w["ghg"], state0, batch, seq, seq, seq)
    x1 = _out_proj(x, gt1, o_mla, o_hg, w["wout"], tm)
    y = _ffn(x1, sh2, sc2, gt2, w["w1"], w["w2"], w["gfin"], min(1024, t), 1024)
    return y, ckv, kpe, s_fin


def kernel(x_prompt, x_sample, c_prompt, c_sample, cache_ckv, cache_kpe, state_hgrn, w_ada, b_ada, w_in, w_uq, g_q, g_kv, w_uk, w_uv, hg_lower_bounds, g_hgrn, w_out, w_ff1, w_ff2, g_final):
    depth = w_ada.shape[0]
    assert depth == 1, "single-layer step"
    bp, sp, d = x_prompt.shape
    bs, ss, _ = x_sample.shape
    past = cache_ckv.shape[2]

    lb_all = jnp.cumsum(jax.nn.softmax(hg_lower_bounds.astype(F32), axis=0), axis=0)
    w = _prep_weights(w_in[0], w_uq[0], g_q[0], g_kv[0], w_uk[0], w_uv[0], lb_all[0], g_hgrn[0],
                      w_out[0], w_ff1[0], w_ff2[0], g_final)

    c_all = jnp.concatenate([c_prompt, c_sample], axis=0)
    n_c = c_all.shape[0]
    c_pad = jnp.pad(c_all, ((0, -n_c % 8), (0, 0)))
    mod = _modulation(c_pad, w_ada[0], b_ada[0])[:n_c]
    mod_p = [m.reshape(bp, 1, d) for m in jnp.split(mod[:bp], 6, axis=-1)]
    mod_s = [jnp.repeat(m, ss, axis=0).reshape(1, bs * ss, d) for m in jnp.split(mod[bp:], 6, axis=-1)]

    yp, ckv_p, kpe_p, st_p = _layer(x_prompt.reshape(bp * sp, d), mod_p, w, bp, sp, None, 0)
    ys, ckv_s, kpe_s, st_s = _layer(x_sample.reshape(bs * ss, d), mod_s, w, bs, ss,
                                    (cache_ckv[0], jnp.swapaxes(cache_kpe[0], 1, 2), state_hgrn[0]), past)
    return (yp.reshape(bp, sp, d), ys.reshape(bs, ss, d),
            ckv_p.reshape(1, bp, sp, KV_RANK), kpe_p.reshape(1, bp, sp, QK_ROPE), st_p[None],
            ckv_s.reshape(1, bs, ss, KV_RANK), kpe_s.reshape(1, bs, ss, QK_ROPE), st_s[None])
```

```python
import functools
import math

import jax
import jax.numpy as jnp
from jax import lax
from jax.experimental import pallas as pl
from jax.experimental.pallas import tpu as pltpu

F32 = jnp.float32
BF16 = jnp.bfloat16

EPS = 1e-6
CHUNK = 64
N_HEADS = 8
QK_NOPE = 128
QK_ROPE = 64
V_HEAD = 128
Q_RANK = 512
KV_RANK = 512
ROPE_BASE = 10000.0
ATTN_SCALE = (QK_NOPE + QK_ROPE) ** -0.5
HG_HEADS = 8
HG_DIM = 128
HG_WIDTH = HG_HEADS * HG_DIM
HEAD_PACK = 2 * QK_NOPE
LANES = 128
VMEM_LIMIT = 56 * 1024 * 1024
FFN_VMEM_LIMIT = 62 * 1024 * 1024

_NT = (((1,), (1,)), ((), ()))
_TN = (((0,), (0,)), ((), ()))


def _rms(x):
    return x * lax.rsqrt(jnp.mean(x * x, axis=-1, keepdims=True) + EPS)


def _dot(a, b, dims=None):
    if dims is None:
        return jnp.dot(a, b, preferred_element_type=F32)
    return lax.dot_general(a, b, dims, preferred_element_type=F32)


def _resident(shape):
    return pl.BlockSpec(shape, lambda *_: (0,) * len(shape), pipeline_mode=pl.Buffered(1))


def _params(*sem, vmem_limit=VMEM_LIMIT):
    return pltpu.CompilerParams(dimension_semantics=sem, vmem_limit_bytes=vmem_limit)


def _mod_kernel(c_ref, w_ref, b_ref, o_ref):
    c = c_ref[...]
    a = (c * jax.nn.sigmoid(c)).astype(BF16)
    o_ref[...] = _dot(a, w_ref[...].astype(BF16)) + b_ref[...]


def _modulation(c_all, w_ada, b_ada):
    m, d = c_all.shape
    n = w_ada.shape[1]
    tn = 1024
    return pl.pallas_call(
        _mod_kernel,
        out_shape=jax.ShapeDtypeStruct((m, n), F32),
        grid=(n // tn,),
        in_specs=[pl.BlockSpec((m, d), lambda j: (0, 0)),
                  pl.BlockSpec((d, tn), lambda j: (0, j)),
                  pl.BlockSpec((1, tn), lambda j: (0, j))],
        out_specs=pl.BlockSpec((m, tn), lambda j: (0, j)),
        compiler_params=_params("arbitrary"),
        name="mod",
    )(c_all, w_ada, b_ada.reshape(1, n))


def _mla_in_kernel(x_ref, sh_ref, sc_ref, cr_ref, sr_ref, wa_ref, wuq_ref, wukv_ref, gq_ref, gkv_ref,
                   *refs, pos_base, pos_period, n_cast, with_kv):
    cast_refs, (q_ref, ckv_ref, kpe_ref), refs = refs[:n_cast], refs[n_cast:n_cast + 3], refs[n_cast + 3:]
    if n_cast:
        rows = cast_refs[0].shape[0]
        for j, src in enumerate(cast_refs):
            refs[-1][j * rows:(j + 1) * rows, :] = src[...].astype(BF16)
    x = x_ref[...]
    tm = x.shape[0]
    h = (_rms(x) * (1.0 + sc_ref[...]) + sh_ref[...]).astype(BF16)

    half = QK_ROPE // 2
    lane = lax.broadcasted_iota(jnp.int32, (8, LANES), 1)
    base = (pos_base + lax.rem(pl.program_id(0) * tm, pos_period)).astype(F32)
    ang_b = base * jnp.exp(-math.log(ROPE_BASE) * lax.rem(lane, half).astype(F32) / half)
    cb, sb = jnp.cos(ang_b)[0:1], jnp.sin(ang_b)[0:1]
    cr, sr = cr_ref[...], sr_ref[...]
    cos = cr * cb - sr * sb
    sin = sr * cb + cr * sb

    def rope(t):
        return t * cos + pltpu.roll(t, QK_ROPE, 1) * sin

    a = _dot(h, wa_ref[...], _NT)
    cqn = (_rms(a[:, :Q_RANK]) * gq_ref[...]).astype(BF16)
    ckvn = _rms(a[:, Q_RANK:Q_RANK + KV_RANK]) * gkv_ref[...]
    ckv_ref[...] = ckvn

    q = _dot(cqn, wuq_ref[...])
    for hd in range(N_HEADS):
        lo = hd * HEAD_PACK
        q_ref[:, lo:lo + QK_NOPE] = q[:, lo:lo + QK_NOPE].astype(BF16)
        q_ref[:, lo + QK_NOPE:lo + HEAD_PACK] = rope(q[:, lo + QK_NOPE:lo + HEAD_PACK]).astype(BF16)

    kp = rope(a[:, Q_RANK + KV_RANK:])
    kpe_ref[...] = kp[:, :QK_ROPE]

    if with_kv:
        kf_ref, v_ref = refs[:2]
        kv = _dot(ckvn.astype(BF16), wukv_ref[...])
        kpb = kp.astype(BF16)
        for hd in range(N_HEADS):
            lo = hd * HEAD_PACK
            kf_ref[:, lo:lo + QK_NOPE] = kv[:, hd * QK_NOPE:(hd + 1) * QK_NOPE].astype(BF16)
            kf_ref[:, lo + QK_NOPE:lo + HEAD_PACK] = kpb
            vo = N_HEADS * QK_NOPE + hd * V_HEAD
            v_ref[:, lo:lo + V_HEAD] = kv[:, vo:vo + V_HEAD].astype(BF16)
            v_ref[:, lo + V_HEAD:lo + HEAD_PACK] = jnp.ones((kv.shape[0], HEAD_PACK - V_HEAD), BF16)


def _mla_in(x, sh, sc, w, tm, with_kv, pos_base, pos_period, cast_src=None):
    t, d = x.shape
    n_tiles = t // tm
    tiles_per_group = n_tiles // sh.shape[0]
    r = sh.shape[1]
    row = lambda n: pl.BlockSpec((tm, n), lambda i: (i, 0))
    mod = pl.BlockSpec((None, r, d), lambda i: (i // tiles_per_group, 0, 0))
    packed = N_HEADS * HEAD_PACK
    half = QK_ROPE // 2
    lane = jnp.arange(LANES)
    inv_freq = jnp.exp(-math.log(ROPE_BASE) * (lane % half).astype(F32) / half)
    ang = (jnp.arange(tm) % min(tm, pos_period)).astype(F32)[:, None] * inv_freq[None, :]
    cr = jnp.where(lane < QK_ROPE, jnp.cos(ang), 0.0)
    sr = jnp.where(lane < QK_ROPE, jnp.sin(ang), 0.0)
    out_shape = [jax.ShapeDtypeStruct((t, packed), BF16),
                 jax.ShapeDtypeStruct((t, KV_RANK), F32),
                 jax.ShapeDtypeStruct((t, QK_ROPE), F32)]
    out_specs = [row(packed), row(KV_RANK), row(QK_ROPE)]
    if with_kv:
        out_shape += [jax.ShapeDtypeStruct((t, packed), BF16),
                      jax.ShapeDtypeStruct((t, packed), BF16)]
        out_specs += [row(packed), row(packed)]
    cast_args, cast_specs = [], []
    if cast_src is not None:
        src, first, n_rows = cast_src
        blk = n_rows // n_tiles
        sub = math.gcd(first, blk)
        for j in range(blk // sub):
            cast_args.append(src)
            cast_specs.append(pl.BlockSpec((sub, d), lambda i, j=j: (first // sub + i * (blk // sub) + j, 0)))
        out_shape.append(jax.ShapeDtypeStruct((n_rows, d), BF16))
        out_specs.append(pl.BlockSpec((blk, d), lambda i: (i, 0)))
    return pl.pallas_call(
        functools.partial(_mla_in_kernel, pos_base=pos_base, pos_period=pos_period,
                          n_cast=len(cast_args), with_kv=with_kv),
        out_shape=out_shape,
        grid=(n_tiles,),
        in_specs=[row(d), mod, mod, _resident((tm, LANES)), _resident((tm, LANES)),
                  _resident(w["wa"].shape), _resident(w["wuq"].shape), _resident(w["wukv"].shape),
                  _resident((1, Q_RANK)), _resident((1, KV_RANK))] + cast_specs,
        out_specs=out_specs,
        compiler_params=_params("arbitrary"),
        name="mla_in",
    )(x, sh, sc, cr, sr, w["wa"], w["wuq"], w["wukv"], w["gq"], w["gkv"], *cast_args)


def _hg_in_kernel(x_ref, sh_ref, sc_ref, wh_ref, lb_ref, *refs):
    qh_ref, kh_ref, vh_ref, gate_ref, g_ref = refs[-5:]
    n_cast = (len(refs) - 5) // 2
    for src, dst in zip(refs[:n_cast], refs[n_cast:2 * n_cast]):
        dst[...] = src[...].astype(BF16)
    x = x_ref[...]
    h = (_rms(x) * (1.0 + sc_ref[...]) + sh_ref[...]).astype(BF16)
    n = HG_WIDTH
    hf = _dot(h, wh_ref[n:2 * n, :], _NT)
    lb = lb_ref[...]
    f = lb + (1.0 - lb) * jax.nn.sigmoid(hf)
    kh_ref[...] = (1.0 - f).astype(BF16)
    g_ref[...] = jnp.log2(f)
    hq = _dot(h, wh_ref[0:n, :], _NT)
    qh_ref[...] = (hq * jax.nn.sigmoid(hq) * (HG_DIM ** -0.5)).astype(BF16)
    hg = _dot(h, wh_ref[3 * n:4 * n, :], _NT)
    gate_ref[...] = (hg * jax.nn.sigmoid(hg)).astype(BF16)
    vh_ref[...] = _dot(h, wh_ref[2 * n:3 * n, :], _NT).astype(BF16)


def _hg_in(x, sh, sc, w, tm, cast=()):
    t, d = x.shape
    n_tiles = t // tm
    tiles_per_group = n_tiles // sh.shape[0]
    r = sh.shape[1]
    row = lambda n: pl.BlockSpec((tm, n), lambda i: (i, 0))
    mod = pl.BlockSpec((None, r, d), lambda i: (i // tiles_per_group, 0, 0))
    half = jax.ShapeDtypeStruct((t, HG_WIDTH), BF16)
    cast_specs = [pl.BlockSpec((m.shape[0] // n_tiles, m.shape[1]), lambda i: (i, 0)) for m in cast]
    return pl.pallas_call(
        _hg_in_kernel,
        out_shape=[jax.ShapeDtypeStruct(m.shape, BF16) for m in cast]
        + [half, half, half, half, jax.ShapeDtypeStruct((t, HG_WIDTH), F32)],
        grid=(n_tiles,),
        in_specs=[row(d), mod, mod, _resident(w["wh"].shape), _resident((1, HG_WIDTH))] + cast_specs,
        out_specs=cast_specs + [row(HG_WIDTH)] * 5,
        compiler_params=_params("arbitrary"),
        name="hg_in",
    )(x, sh, sc, w["wh"], w["lb"], *cast)


def _attn_kernel(q_ref, qn_ref, k_ref, v_ref, o_ref, s0_ref, s1_ref, mx0_ref, mx1_ref, m_ref, acc_ref,
                 *, tk, nh):
    qi = pl.program_id(2)
    diag_ok = (lax.broadcasted_iota(jnp.int32, (tk, tk), 1) // CHUNK
               <= lax.broadcasted_iota(jnp.int32, (tk, tk), 0) // CHUNK)
    streams = [(hd, r) for hd in range(nh) for r in range(2)]
    sid = lambda hd, r: 2 * hd + r

    def kv_rows(blk):
        return pl.ds(pl.multiple_of(blk * tk, tk), tk)

    def rows(r):
        return slice(r * tk, (r + 1) * tk)

    def head(hd):
        return slice(hd * HEAD_PACK, (hd + 1) * HEAD_PACK)

    def lanes(x, n):
        return jnp.concatenate([x] * (n // LANES), axis=1)

    def row_max(s):
        return jnp.broadcast_to(jnp.max(s, axis=-1, keepdims=True), (tk, LANES))

    def scores(q, hd, r, blk):
        return _dot(q[rows(r), head(hd)], k_ref[kv_rows(blk), head(hd)], _NT)

    def produce(q, hd, r, blk, s_ref, mx_ref):
        s = scores(q, hd, r, blk)
        s_ref[sid(hd, r)] = s
        mx_ref[sid(hd, r)] = row_max(s)

    def block_softmax(s, mx, hd, blk):
        p = jnp.exp2(s - lanes(mx, tk)).astype(BF16)
        return mx, _dot(p, v_ref[kv_rows(blk), head(hd)])

    def merge(a, b):
        (ma, acc_a), (mb, acc_b) = a, b
        m = jnp.maximum(ma, mb)
        return m, (lanes(jnp.exp2(ma - m), HEAD_PACK) * acc_a
                   + lanes(jnp.exp2(mb - m), HEAD_PACK) * acc_b)

    def update(hd, r, s, mx, blk):
        i = sid(hd, r)
        m_old = m_ref[i]
        m_new = jnp.maximum(m_old, mx)
        m_ref[i] = m_new
        p = jnp.exp2(s - lanes(m_new, tk)).astype(BF16)
        acc_ref[i] = (lanes(jnp.exp2(m_old - m_new), HEAD_PACK) * acc_ref[i]
                      + _dot(p, v_ref[kv_rows(blk), head(hd)]))

    @pl.when(qi == 0)
    def _():
        for hd, r in streams:
            produce(q_ref, hd, r, 0, s0_ref, mx0_ref)

    for i in range(2 * nh):
        m_ref[i] = jnp.full((tk, LANES), -1e30, F32)
        acc_ref[i] = jnp.zeros((tk, HEAD_PACK), F32)

    def body(j, carry):
        blk = 2 * j
        for (cur, cmx), (nxt, nmx) in (((s0_ref, mx0_ref), (s1_ref, mx1_ref)),
                                       ((s1_ref, mx1_ref), (s0_ref, mx0_ref))):
            for hd, r in streams:
                produce(q_ref, hd, r, blk + 1, nxt, nmx)
            for hd, r in streams:
                update(hd, r, cur[sid(hd, r)], cmx[sid(hd, r)], blk)
            blk = blk + 1
        return carry

    lax.fori_loop(0, qi, body, 0)

    for hd in range(nh):
        top, bot = sid(hd, 0), sid(hd, 1)
        d0 = jnp.where(diag_ok, s0_ref[top], -1e30)
        d1 = jnp.where(diag_ok, scores(q_ref, hd, 1, 2 * qi + 1), -1e30)
        part_a = block_softmax(d0, row_max(d0), hd, 2 * qi)
        part_b = block_softmax(s0_ref[bot], mx0_ref[bot], hd, 2 * qi)
        part_c = block_softmax(d1, row_max(d1), hd, 2 * qi + 1)
        finals = [merge((m_ref[top], acc_ref[top]), part_a),
                  merge(merge((m_ref[bot], acc_ref[bot]), part_b), part_c)]
        for r in range(2):
            acc = finals[r][1]
            o_ref[rows(r), hd * V_HEAD:(hd + 1) * V_HEAD] = (acc[:, :V_HEAD] / acc[:, V_HEAD:]).astype(BF16)
    for hd, r in streams:
        produce(qn_ref, hd, r, 0, s0_ref, mx0_ref)


def _attention(q, kf, vp, batch, seq, tk, nh):
    tq = 2 * tk
    nq = seq // tq
    return pl.pallas_call(
        functools.partial(_attn_kernel, tk=tk, nh=nh),
        out_shape=jax.ShapeDtypeStruct((batch * seq, N_HEADS * V_HEAD), BF16),
        grid=(batch, N_HEADS // nh, nq),
        in_specs=[pl.BlockSpec((tq, nh * HEAD_PACK), lambda b, h, i: (b * nq + i, h)),
                  pl.BlockSpec((tq, nh * HEAD_PACK), lambda b, h, i: (b * nq + jnp.minimum(i + 1, nq - 1), h)),
                  pl.BlockSpec((seq, nh * HEAD_PACK), lambda b, h, i: (b, h)),
                  pl.BlockSpec((seq, nh * HEAD_PACK), lambda b, h, i: (b, h))],
        out_specs=pl.BlockSpec((tq, nh * V_HEAD), lambda b, h, i: (b * nq + i, h)),
        scratch_shapes=[pltpu.VMEM((2 * nh, tk, tk), F32), pltpu.VMEM((2 * nh, tk, tk), F32),
                        pltpu.VMEM((2 * nh, tk, LANES), F32), pltpu.VMEM((2 * nh, tk, LANES), F32),
                        pltpu.VMEM((2 * nh, tk, LANES), F32), pltpu.VMEM((2 * nh, tk, HEAD_PACK), F32)],
        compiler_params=_params("arbitrary", "arbitrary", "arbitrary"),
        name="attn",
    )(q, q, kf, vp)


def _qlat_kernel(qn_ref, qp_ref, wuk_ref, ql_ref, qpe_ref):
    nb = ql_ref.shape[0]
    ql = _dot(qn_ref[...], wuk_ref[...], _NT)
    ql_ref[...] = ql.reshape(nb, -1, KV_RANK)
    qpe_ref[...] = qp_ref[...].astype(F32).reshape(nb, -1, LANES)


def _q_latent(q, wuk_t, nb, ln):
    t = q.shape[0]
    return pl.pallas_call(
        _qlat_kernel,
        out_shape=[jax.ShapeDtypeStruct((nb, N_HEADS, ln, KV_RANK), F32),
                   jax.ShapeDtypeStruct((nb, N_HEADS, ln, LANES), F32)],
        grid=(N_HEADS,),
        in_specs=[pl.BlockSpec((t, QK_NOPE), lambda h: (0, 2 * h)),
                  pl.BlockSpec((t, QK_NOPE), lambda h: (0, 2 * h + 1)),
                  pl.BlockSpec((None, KV_RANK, QK_NOPE), lambda h: (h, 0, 0))],
        out_specs=[pl.BlockSpec((nb, None, ln, KV_RANK), lambda h: (0, h, 0, 0)),
                   pl.BlockSpec((nb, None, ln, LANES), lambda h: (0, h, 0, 0))],
        compiler_params=_params("arbitrary"),
        name="q_latent",
    )(q, q, wuk_t)


def _dec_attn_kernel(ql_ref, qpe_ref, cckv_ref, ckpe_ref, nckv_ref, nkpe_ref, wuv_ref, o_ref, *, ln):
    rows = N_HEADS * ln
    for b in range(ql_ref.shape[0]):
        new = slice(b * ln, (b + 1) * ln)
        ql = ql_ref[b].reshape(rows, KV_RANK).astype(BF16)
        qp = qpe_ref[b].reshape(rows, LANES)[:, :QK_ROPE].astype(BF16)
        ckc = cckv_ref[b].astype(BF16)
        kpc = ckpe_ref[b].astype(BF16)
        ckn = nckv_ref[new, :].astype(BF16)
        kpn = nkpe_ref[new, :].astype(BF16)
        s_c = _dot(ql, ckc, _NT) + _dot(qp, kpc)
        s_n = _dot(ql, ckn, _NT) + _dot(qp, kpn, _NT)
        m = jnp.maximum(jnp.max(s_c, axis=-1, keepdims=True), jnp.max(s_n, axis=-1, keepdims=True))
        p_c = jnp.exp2(s_c - m)
        p_n = jnp.exp2(s_n - m)
        l = jnp.sum(p_c, axis=-1, keepdims=True) + jnp.sum(p_n, axis=-1, keepdims=True)
        ctx = (_dot(p_c.astype(BF16), ckc) + _dot(p_n.astype(BF16), ckn)) / l
        ctx = ctx.astype(BF16)
        for hd in range(N_HEADS):
            o_ref[new, hd * V_HEAD:(hd + 1) * V_HEAD] = _dot(
                ctx[hd * ln:(hd + 1) * ln], wuv_ref[hd]).astype(BF16)


def _dec_attention(ql, qpe, cache_ckv, cache_kpe_t, ckv_new, kpe_new, wuv_t, ln):
    nb, past = cache_ckv.shape[0], cache_ckv.shape[1]
    bs = 2 if nb % 2 == 0 else 1
    return pl.pallas_call(
        functools.partial(_dec_attn_kernel, ln=ln),
        out_shape=jax.ShapeDtypeStruct((nb * ln, N_HEADS * V_HEAD), BF16),
        grid=(nb // bs,),
        in_specs=[pl.BlockSpec((bs, N_HEADS, ln, KV_RANK), lambda b: (b, 0, 0, 0)),
                  pl.BlockSpec((bs, N_HEADS, ln, LANES), lambda b: (b, 0, 0, 0)),
                  pl.BlockSpec((bs, past, KV_RANK), lambda b: (b, 0, 0)),
                  pl.BlockSpec((bs, QK_ROPE, past), lambda b: (b, 0, 0)),
                  pl.BlockSpec((bs * ln, KV_RANK), lambda b: (b, 0)),
                  pl.BlockSpec((bs * ln, QK_ROPE), lambda b: (b, 0)),
                  _resident(wuv_t.shape)],
        out_specs=pl.BlockSpec((bs * ln, N_HEADS * V_HEAD), lambda b: (b, 0)),
        compiler_params=_params("arbitrary"),
        name="dec_attn",
    )(ql, qpe, cache_ckv, cache_kpe_t, ckv_new, kpe_new, wuv_t)


def _gla_consts(ln, width):
    row = lax.broadcasted_iota(jnp.int32, (8, width), 0)
    t_idx = lax.broadcasted_iota(jnp.int32, (ln, ln), 0)
    s_idx = lax.broadcasted_iota(jnp.int32, (ln, ln), 1)
    level_mask = []
    for lev in range(int(math.log2(ln))):
        level_mask.append((((t_idx ^ s_idx) >> (lev + 1)) == 0)
                          & ((t_idx & (1 << lev)) != 0) & ((s_idx & (1 << lev)) == 0))
    return {
        "scan": [row >= sft for sft in (1, 2, 4)],
        "odd": (row & 1).astype(F32),
        "low4": (row & 4) == 0,
        "sign1": jnp.where((row & 2) != 0, 1.0, -1.0).astype(F32),
        "sign2": jnp.where((row & 4) != 0, 1.0, -1.0).astype(F32),
        "level_mask": level_mask,
    }


def _gla_chunk(qb, kb, vb, g, st_ref, loc_ref, cst, ln):
    nv = ln // 8
    width = g.shape[1]
    grp = lambda x: [x[8 * i:8 * i + 8, :] for i in range(nv)]
    cat = lambda xs: jnp.concatenate(xs, axis=0)
    head = lambda x, hd: x[:, hd * HG_DIM:(hd + 1) * HG_DIM]
    q, k = grp(qb.astype(F32)), grp(kb.astype(F32))
    gs = grp(g)

    loc = gs
    for sft, keep in zip((1, 2, 4), cst["scan"]):
        loc = [x + jnp.where(keep, pltpu.roll(x, sft, 0), 0.0) for x in loc]
    loc_ref[...] = cat(loc)
    bcast = lambda r: jnp.broadcast_to(loc_ref[r:r + 1, :], (8, width))
    tot = [bcast(8 * i + 7) for i in range(nv)]
    pin = [tot[0]]
    for i in range(1, nv):
        pin.append(pin[-1] + tot[i])
    b = [loc[0]] + [loc[i] + pin[i - 1] for i in range(1, nv)]
    b_last = pin[-1]

    exps = [[gs[i] * cst["odd"] for i in range(nv)],
            [(loc[i] - jnp.where(cst["low4"], bcast(8 * i + 1), bcast(8 * i + 5))) * cst["sign1"]
             for i in range(nv)],
            [(loc[i] - bcast(8 * i + 3)) * cst["sign2"] for i in range(nv)]]
    upper = [[True] * nv, [True] * nv, [True] * nv]
    lower = [[True] * nv, [True] * nv, [True] * nv]
    for lev in range(3, int(math.log2(ln))):
        m = 1 << (lev - 3)
        e, up, lo = [], [], []
        for i in range(nv):
            mid = (i // (2 * m)) * 2 * m + m - 1
            is_up = (i // m) % 2 == 1
            if is_up:
                e.append(loc[i] if mid == i - 1 else b[i] - pin[mid])
            else:
                e.append(tot[i] - loc[i] if mid == i else pin[mid] - b[i])
            up.append(is_up)
            lo.append(not is_up)
        exps.append(e)
        upper.append(up)
        lower.append(lo)

    a = [jnp.zeros((ln, ln), F32)] * HG_HEADS
    for lev, e in enumerate(exps):
        w = [jnp.exp2(x) for x in e]
        qt = cat([q[i] * w[i] if upper[lev][i] else q[i] for i in range(nv)]).astype(BF16)
        kt = cat([k[i] * w[i] if lower[lev][i] else k[i] for i in range(nv)]).astype(BF16)
        a = [jnp.where(cst["level_mask"][lev], _dot(head(qt, hd), head(kt, hd), _NT), a[hd])
             for hd in range(HG_HEADS)]

    qd = cat([q[i] * jnp.exp2(b[i]) for i in range(nv)]).astype(BF16)
    kd = cat([k[i] * jnp.exp2(b_last - b[i]) for i in range(nv)]).astype(BF16)
    qk = cat([q[i] * k[i] for i in range(nv)])
    vf = vb.astype(F32)
    decay = jnp.exp2(b_last[0:1, :])
    outs = []
    for hd in range(HG_HEADS):
        st = st_ref[hd]
        v_h = head(vb, hd)
        o = _dot(head(qd, hd), st.astype(BF16), _NT) + _dot(a[hd].astype(BF16), v_h)
        outs.append(o + jnp.sum(head(qk, hd), axis=-1, keepdims=True) * head(vf, hd))
        st_ref[hd] = st * head(decay, hd) + _dot(v_h, head(kd, hd), _TN)
    return outs


def _hgrn_kernel(*refs, ln, n_chunks, has_init):
    if has_init:
        qh_ref, kh_ref, vh_ref, gate_ref, g_ref, gw_ref, s0_ref, o_ref, sout_ref, st_ref, b_ref = refs
    else:
        qh_ref, kh_ref, vh_ref, gate_ref, g_ref, gw_ref, o_ref, sout_ref, st_ref, b_ref = refs
        s0_ref = None
    si = pl.program_id(1)

    @pl.when(si == 0)
    def _():
        for hd in range(HG_HEADS):
            st_ref[hd] = s0_ref[hd].T if has_init else jnp.zeros((HG_DIM, HG_DIM), F32)

    gw = gw_ref[...]
    cst = _gla_consts(ln, HG_WIDTH)

    def body(c, carry):
        rows = pl.ds(pl.multiple_of(c * ln, ln), ln)
        outs = _gla_chunk(qh_ref[rows, :], kh_ref[rows, :], vh_ref[rows, :], g_ref[rows, :],
                          st_ref, b_ref, cst, ln)
        for hd in range(HG_HEADS):
            lanes = slice(hd * HG_DIM, (hd + 1) * HG_DIM)
            o_ref[rows, lanes] = (_rms(outs[hd]) * gw * gate_ref[rows, lanes].astype(F32)).astype(BF16)
        return carry

    lax.fori_loop(0, n_chunks, body, 0, unroll=8 if n_chunks % 8 == 0 else 1)

    @pl.when(si == pl.num_programs(1) - 1)
    def _():
        for hd in range(HG_HEADS):
            sout_ref[hd] = st_ref[hd].T


def _hgrn(qh, kh, vh, gate, g, gw, state0, batch, seq, ln, ts):
    ns = seq // ts
    has_init = state0 is not None
    row = pl.BlockSpec((ts, HG_WIDTH), lambda b, s: (b * ns + s, 0))
    st_spec = pl.BlockSpec((None, HG_HEADS, HG_DIM, HG_DIM), lambda b, s: (b, 0, 0, 0))
    in_specs = [row] * 5 + [_resident((1, HG_DIM))]
    args = [qh, kh, vh, gate, g, gw]
    if has_init:
        in_specs.append(st_spec)
        args.append(state0)
    return pl.pallas_call(
        functools.partial(_hgrn_kernel, ln=ln, n_chunks=ts // ln, has_init=has_init),
        out_shape=[jax.ShapeDtypeStruct((batch * seq, HG_WIDTH), BF16),
                   jax.ShapeDtypeStruct((batch, HG_HEADS, HG_DIM, HG_DIM), F32)],
        grid=(batch, ns),
        in_specs=in_specs,
        out_specs=[row, st_spec],
        scratch_shapes=[pltpu.VMEM((HG_HEADS, HG_DIM, HG_DIM), F32),
                        pltpu.VMEM((ln, HG_WIDTH), F32)],
        compiler_params=_params("arbitrary", "arbitrary"),
        name="hgrn",
    )(*args)


def _out_proj_kernel(x_ref, gt_ref, om_ref, oh_ref, w_ref, o_ref):
    n = om_ref.shape[1]
    mix = _dot(om_ref[...], w_ref[0:n, :]) + _dot(oh_ref[...], w_ref[n:, :])
    o_ref[...] = x_ref[...] + gt_ref[...] * mix


def _out_proj(x, gt, o_mla, o_hg, w_out, tm):
    t, d = x.shape
    n_tiles = t // tm
    tiles_per_group = n_tiles // gt.shape[0]
    r = gt.shape[1]
    row = lambda n: pl.BlockSpec((tm, n), lambda i: (i, 0))
    return pl.pallas_call(
        _out_proj_kernel,
        out_shape=jax.ShapeDtypeStruct((t, d), F32),
        grid=(n_tiles,),
        in_specs=[row(d), pl.BlockSpec((None, r, d), lambda i: (i // tiles_per_group, 0, 0)),
                  row(o_mla.shape[1]), row(o_hg.shape[1]), _resident(w_out.shape)],
        out_specs=row(d),
        compiler_params=_params("arbitrary"),
        name="out_proj",
    )(x, gt, o_mla, o_hg, w_out)


def _ffn_kernel(x_ref, sh_ref, sc_ref, gt_ref, w1_ref, w2_ref, gf_ref, o_ref, h_ref):
    f = pl.program_id(1)
    last = pl.num_programs(1) - 1

    def mlp(h):
        a = jnp.maximum(_dot(h, w1_ref[...]), 0.0)
        return _dot((a * a).astype(BF16), w2_ref[...])

    @pl.when(f == 0)
    def _():
        h = (_rms(x_ref[...]) * (1.0 + sc_ref[...]) + sh_ref[...]).astype(BF16)
        h_ref[...] = h
        o_ref[...] = mlp(h)

    @pl.when(jnp.logical_and(f > 0, f < last))
    def _():
        o_ref[...] += mlp(h_ref[...])

    @pl.when(f == last)
    def _():
        x2 = x_ref[...] + gt_ref[...] * (o_ref[...] + mlp(h_ref[...]))
        o_ref[...] = _rms(x2) * gf_ref[...]


def _ffn(x, sh, sc, gt, w1, w2, gf, tm, tf):
    t, d = x.shape
    dff = w1.shape[1]
    n_tiles = t // tm
    tiles_per_group = n_tiles // sh.shape[0]
    r = sh.shape[1]
    row = pl.BlockSpec((tm, d), lambda i, f: (i, 0))
    mod = pl.BlockSpec((None, r, d), lambda i, f: (i // tiles_per_group, 0, 0))
    return pl.pallas_call(
        _ffn_kernel,
        out_shape=jax.ShapeDtypeStruct((t, d), F32),
        grid=(n_tiles, dff // tf),
        in_specs=[row, mod, mod, mod,
                  pl.BlockSpec((d, tf), lambda i, f: (0, f)),
                  pl.BlockSpec((tf, d), lambda i, f: (f, 0)),
                  pl.BlockSpec((1, d), lambda i, f: (0, 0))],
        out_specs=row,
        scratch_shapes=[pltpu.VMEM((tm, d), BF16)],
        compiler_params=_params("arbitrary", "arbitrary", vmem_limit=FFN_VMEM_LIMIT),
        name="ffn",
    )(x, sh, sc, gt, w1, w2, gf)


def _rot_half_cols(w, axis=-1):
    lo, hi = jnp.split(w, 2, axis=axis)
    return jnp.concatenate([-hi, lo], axis=axis)


def _prep_weights(w_in, w_uq, g_q, g_kv, w_uk, w_uv, lb, g_hgrn, w_out, w_ff1, w_ff2, g_final):
    d = w_in.shape[0]
    o_kpe = Q_RANK + KV_RANK
    o_hg = o_kpe + QK_ROPE
    w_in_t = jnp.swapaxes(w_in, 0, 1)
    w_mla_t = lax.optimization_barrier(w_in_t[:o_hg])
    w_kpe_t = w_mla_t[o_kpe:]
    q_scale = ATTN_SCALE * math.log2(math.e)
    wq = w_uq.reshape(Q_RANK, N_HEADS, QK_NOPE + QK_ROPE) * q_scale
    wq_pe = wq[..., QK_NOPE:]
    wuq = jnp.concatenate([wq[..., :QK_NOPE], wq_pe, _rot_half_cols(wq_pe)], axis=-1)
    return {
        "wa": jnp.concatenate([w_mla_t, _rot_half_cols(w_kpe_t, axis=0)], axis=0).astype(BF16),
        "wh_src": (w_in_t, o_hg, w_in_t.shape[0] - o_hg),
        "wuq": wuq.reshape(Q_RANK, N_HEADS * HEAD_PACK).astype(BF16),
        "wukv": jnp.concatenate([w_uk.reshape(KV_RANK, -1), w_uv.reshape(KV_RANK, -1)], axis=-1).astype(BF16),
        "wuk_t": jnp.transpose(w_uk, (1, 0, 2)).astype(BF16),
        "wuv_t": jnp.transpose(w_uv, (1, 0, 2)).astype(BF16),
        "gq": g_q.reshape(1, -1), "gkv": g_kv.reshape(1, -1),
        "lb": lb.reshape(1, -1), "ghg": g_hgrn.reshape(1, -1),
        "wout_f32": w_out, "w1_f32": w_ff1, "w2_f32": w_ff2,
        "gfin": g_final.reshape(1, d),
    }


def _layer(x, mods, w, batch, seq, cache, past):
    sh1, sc1, gt1, sh2, sc2, gt2 = mods
    t = x.shape[0]
    tm = min(512, t)
    prompt = cache is None
    if prompt:
        q, ckv, kpe, kf, v, w["wh"] = _mla_in(x, sh1, sc1, w, tm, True, past, seq, w["wh_src"])
        o_mla = _attention(q, kf, v, batch, seq, 512, 2)
    else:
        cache_ckv, cache_kpe, state0 = cache
        q, ckv, kpe = _mla_in(x, sh1, sc1, w, tm, False, past, seq)
        ql, qpe = _q_latent(q, w["wuk_t"], batch, seq)
        o_mla = _dec_attention(ql, qpe, cache_ckv, cache_kpe, ckv, kpe, w["wuv_t"], seq)
    if prompt:
        w["w1"], w["w2"], w["wout"], qh, kh, vh, gate, g = _hg_in(
            x, sh1, sc1, w, tm, (w["w1_f32"], w["w2_f32"], w["wout_f32"]))
        o_hg, s_fin = _hgrn(qh, kh, vh, gate, g, w["ghg"], None, batch, seq, CHUNK, 1024)
    else:
        qh, kh, vh, gate, g = _hg_in(x, sh1, sc1, w, tm)
        o_hg, s_fin = _hgrn(qh, kh, vh, gate, g, w["ghg"], state0, batch, seq, seq, seq)
    x1 = _out_proj(x, gt1, o_mla, o_hg, w["wout"], tm)
    y = _ffn(x1, sh2, sc2, gt2, w["w1"], w["w2"], w["gfin"], min(1024, t), 1024)
    return y, ckv, kpe, s_fin


def kernel(x_prompt, x_sample, c_prompt, c_sample, cache_ckv, cache_kpe, state_hgrn, w_ada, b_ada, w_in, w_uq, g_q, g_kv, w_uk, w_uv, hg_lower_bounds, g_hgrn, w_out, w_ff1, w_ff2, g_final):
    depth = w_ada.shape[0]
    assert depth == 1, "single-layer step"
    bp, sp, d = x_prompt.shape
    bs, ss, _ = x_sample.shape
    past = cache_ckv.shape[2]

    lb_all = jnp.cumsum(jax.nn.softmax(hg_lower_bounds.astype(F32), axis=0), axis=0)
    w = _prep_weights(w_in[0], w_uq[0], g_q[0], g_kv[0], w_uk[0], w_uv[0], lb_all[0], g_hgrn[0],
                      w_out[0], w_ff1[0], w_ff2[0], g_final)

    c_all = jnp.concatenate([c_prompt, c_sample], axis=0)
    n_c = c_all.shape[0]
    c_pad = jnp.pad(c_all, ((0, -n_c % 8), (0, 0)))
    mod = _modulation(c_pad, w_ada[0], b_ada[0])[:n_c]
    mod_p = [m.reshape(bp, 1, d) for m in jnp.split(mod[:bp], 6, axis=-1)]
    mod_s = [jnp.repeat(m, ss, axis=0).reshape(1, bs * ss, d) for m in jnp.split(mod[bp:], 6, axis=-1)]

    yp, ckv_p, kpe_p, st_p = _layer(x_prompt.reshape(bp * sp, d), mod_p, w, bp, sp, None, 0)
    ys, ckv_s, kpe_s, st_s = _layer(x_sample.reshape(bs * ss, d), mod_s, w, bs, ss,
                                    (cache_ckv[0], jnp.swapaxes(cache_kpe[0], 1, 2), state_hgrn[0]), past)
    return (yp.reshape(bp, sp, d), ys.reshape(bs, ss, d),
            ckv_p.reshape(1, bp, sp, KV_RANK), kpe_p.reshape(1, bp, sp, QK_ROPE), st_p[None],
            ckv_s.reshape(1, bs, ss, KV_RANK), kpe_s.reshape(1, bs, ss, QK_ROPE), st_s[None])
```

```python
import functools
import math

import jax
import jax.numpy as jnp
from jax import lax
from jax.experimental import pallas as pl
from jax.experimental.pallas import tpu as pltpu

F32 = jnp.float32
BF16 = jnp.bfloat16

EPS = 1e-6
CHUNK = 64
N_HEADS = 8
QK_NOPE = 128
QK_ROPE = 64
V_HEAD = 128
Q_RANK = 512
KV_RANK = 512
ROPE_BASE = 10000.0
ATTN_SCALE = (QK_NOPE + QK_ROPE) ** -0.5
HG_HEADS = 8
HG_DIM = 128
HG_WIDTH = HG_HEADS * HG_DIM
HEAD_PACK = 2 * QK_NOPE
LANES = 128
VMEM_LIMIT = 56 * 1024 * 1024
FFN_VMEM_LIMIT = 62 * 1024 * 1024

_NT = (((1,), (1,)), ((), ()))
_TN = (((0,), (0,)), ((), ()))


def _rms(x):
    return x * lax.rsqrt(jnp.mean(x * x, axis=-1, keepdims=True) + EPS)


def _dot(a, b, dims=None):
    if dims is None:
        return jnp.dot(a, b, preferred_element_type=F32)
    return lax.dot_general(a, b, dims, preferred_element_type=F32)


def _resident(shape):
    return pl.BlockSpec(shape, lambda *_: (0,) * len(shape), pipeline_mode=pl.Buffered(1))


def _params(*sem, vmem_limit=VMEM_LIMIT):
    return pltpu.CompilerParams(dimension_semantics=sem, vmem_limit_bytes=vmem_limit)


def _mod_kernel(c_ref, w_ref, b_ref, o_ref):
    c = c_ref[...]
    a = (c * jax.nn.sigmoid(c)).astype(BF16)
    o_ref[...] = _dot(a, w_ref[...].astype(BF16)) + b_ref[...]


def _modulation(c_all, w_ada, b_ada):
    m, d = c_all.shape
    n = w_ada.shape[1]
    tn = 1024
    return pl.pallas_call(
        _mod_kernel,
        out_shape=jax.ShapeDtypeStruct((m, n), F32),
        grid=(n // tn,),
        in_specs=[pl.BlockSpec((m, d), lambda j: (0, 0)),
                  pl.BlockSpec((d, tn), lambda j: (0, j)),
                  pl.BlockSpec((1, tn), lambda j: (0, j))],
        out_specs=pl.BlockSpec((m, tn), lambda j: (0, j)),
        compiler_params=_params("arbitrary"),
        name="mod",
    )(c_all, w_ada, b_ada.reshape(1, n))


def _mla_in_kernel(x_ref, sh_ref, sc_ref, cr_ref, sr_ref, wa_ref, wuq_ref, wukv_ref, gq_ref, gkv_ref,
                   *refs, pos_base, pos_period, n_cast, with_kv):
    cast_refs, (q_ref, ckv_ref, kpe_ref), refs = refs[:n_cast], refs[n_cast:n_cast + 3], refs[n_cast + 3:]
    if n_cast:
        rows = cast_refs[0].shape[0]
        for j, src in enumerate(cast_refs):
            refs[-1][j * rows:(j + 1) * rows, :] = src[...].astype(BF16)
    x = x_ref[...]
    tm = x.shape[0]
    h = (_rms(x) * (1.0 + sc_ref[...]) + sh_ref[...]).astype(BF16)

    half = QK_ROPE // 2
    lane = lax.broadcasted_iota(jnp.int32, (8, LANES), 1)
    base = (pos_base + lax.rem(pl.program_id(0) * tm, pos_period)).astype(F32)
    ang_b = base * jnp.exp(-math.log(ROPE_BASE) * lax.rem(lane, half).astype(F32) / half)
    cb, sb = jnp.cos(ang_b)[0:1], jnp.sin(ang_b)[0:1]
    cr, sr = cr_ref[...], sr_ref[...]
    cos = cr * cb - sr * sb
    sin = sr * cb + cr * sb

    def rope(t):
        return t * cos + pltpu.roll(t, QK_ROPE, 1) * sin

    a = _dot(h, wa_ref[...], _NT)
    cqn = (_rms(a[:, :Q_RANK]) * gq_ref[...]).astype(BF16)
    ckvn = _rms(a[:, Q_RANK:Q_RANK + KV_RANK]) * gkv_ref[...]
    ckv_ref[...] = ckvn

    q = _dot(cqn, wuq_ref[...])
    for hd in range(N_HEADS):
        lo = hd * HEAD_PACK
        q_ref[:, lo:lo + QK_NOPE] = q[:, lo:lo + QK_NOPE].astype(BF16)
        q_ref[:, lo + QK_NOPE:lo + HEAD_PACK] = rope(q[:, lo + QK_NOPE:lo + HEAD_PACK]).astype(BF16)

    kp = rope(a[:, Q_RANK + KV_RANK:])
    kpe_ref[...] = kp[:, :QK_ROPE]

    if with_kv:
        kf_ref, v_ref = refs[:2]
        kv = _dot(ckvn.astype(BF16), wukv_ref[...])
        kpb = kp.astype(BF16)
        for hd in range(N_HEADS):
            lo = hd * HEAD_PACK
            kf_ref[:, lo:lo + QK_NOPE] = kv[:, hd * QK_NOPE:(hd + 1) * QK_NOPE].astype(BF16)
            kf_ref[:, lo + QK_NOPE:lo + HEAD_PACK] = kpb
            vo = N_HEADS * QK_NOPE + hd * V_HEAD
            v_ref[:, lo:lo + V_HEAD] = kv[:, vo:vo + V_HEAD].astype(BF16)
            v_ref[:, lo + V_HEAD:lo + HEAD_PACK] = jnp.ones((kv.shape[0], HEAD_PACK - V_HEAD), BF16)


def _mla_in(x, sh, sc, w, tm, with_kv, pos_base, pos_period, cast_src=None):
    t, d = x.shape
    n_tiles = t // tm
    tiles_per_group = n_tiles // sh.shape[0]
    r = sh.shape[1]
    row = lambda n: pl.BlockSpec((tm, n), lambda i: (i, 0))
    mod = pl.BlockSpec((None, r, d), lambda i: (i // tiles_per_group, 0, 0))
    packed = N_HEADS * HEAD_PACK
    half = QK_ROPE // 2
    lane = jnp.arange(LANES)
    inv_freq = jnp.exp(-math.log(ROPE_BASE) * (lane % half).astype(F32) / half)
    ang = (jnp.arange(tm) % min(tm, pos_period)).astype(F32)[:, None] * inv_freq[None, :]
    cr = jnp.where(lane < QK_ROPE, jnp.cos(ang), 0.0)
    sr = jnp.where(lane < QK_ROPE, jnp.sin(ang), 0.0)
    out_shape = [jax.ShapeDtypeStruct((t, packed), BF16),
                 jax.ShapeDtypeStruct((t, KV_RANK), F32),
                 jax.ShapeDtypeStruct((t, QK_ROPE), F32)]
    out_specs = [row(packed), row(KV_RANK), row(QK_ROPE)]
    if with_kv:
        out_shape += [jax.ShapeDtypeStruct((t, packed), BF16),
                      jax.ShapeDtypeStruct((t, packed), BF16)]
        out_specs += [row(packed), row(packed)]
    cast_args, cast_specs = [], []
    if cast_src is not None:
        src, first, n_rows = cast_src
        blk = n_rows // n_tiles
        sub = math.gcd(first, blk)
        for j in range(blk // sub):
            cast_args.append(src)
            cast_specs.append(pl.BlockSpec((sub, d), lambda i, j=j: (first // sub + i * (blk // sub) + j, 0)))
        out_shape.append(jax.ShapeDtypeStruct((n_rows, d), BF16))
        out_specs.append(pl.BlockSpec((blk, d), lambda i: (i, 0)))
    return pl.pallas_call(
        functools.partial(_mla_in_kernel, pos_base=pos_base, pos_period=pos_period,
                          n_cast=len(cast_args), with_kv=with_kv),
        out_shape=out_shape,
        grid=(n_tiles,),
        in_specs=[row(d), mod, mod, _resident((tm, LANES)), _resident((tm, LANES)),
                  _resident(w["wa"].shape), _resident(w["wuq"].shape), _resident(w["wukv"].shape),
                  _resident((1, Q_RANK)), _resident((1, KV_RANK))] + cast_specs,
        out_specs=out_specs,
        compiler_params=_params("arbitrary"),
        name="mla_in",
    )(x, sh, sc, cr, sr, w["wa"], w["wuq"], w["wukv"], w["gq"], w["gkv"], *cast_args)


def _hg_in_kernel(x_ref, sh_ref, sc_ref, wh_ref, lb_ref, *refs):
    qh_ref, kh_ref, vh_ref, gate_ref, g_ref = refs[-5:]
    n_cast = (len(refs) - 5) // 2
    for src, dst in zip(refs[:n_cast], refs[n_cast:2 * n_cast]):
        dst[...] = src[...].astype(BF16)
    x = x_ref[...]
    h = (_rms(x) * (1.0 + sc_ref[...]) + sh_ref[...]).astype(BF16)
    n = HG_WIDTH
    hf = _dot(h, wh_ref[n:2 * n, :], _NT)
    lb = lb_ref[...]
    f = lb + (1.0 - lb) * jax.nn.sigmoid(hf)
    kh_ref[...] = (1.0 - f).astype(BF16)
    g_ref[...] = jnp.log2(f)
    hq = _dot(h, wh_ref[0:n, :], _NT)
    qh_ref[...] = (hq * jax.nn.sigmoid(hq) * (HG_DIM ** -0.5)).astype(BF16)
    hg = _dot(h, wh_ref[3 * n:4 * n, :], _NT)
    gate_ref[...] = (hg * jax.nn.sigmoid(hg)).astype(BF16)
    vh_ref[...] = _dot(h, wh_ref[2 * n:3 * n, :], _NT).astype(BF16)


def _hg_in(x, sh, sc, w, tm, cast=()):
    t, d = x.shape
    n_tiles = t // tm
    tiles_per_group = n_tiles // sh.shape[0]
    r = sh.shape[1]
    row = lambda n: pl.BlockSpec((tm, n), lambda i: (i, 0))
    mod = pl.BlockSpec((None, r, d), lambda i: (i // tiles_per_group, 0, 0))
    half = jax.ShapeDtypeStruct((t, HG_WIDTH), BF16)
    cast_specs = [pl.BlockSpec((m.shape[0] // n_tiles, m.shape[1]), lambda i: (i, 0)) for m in cast]
    return pl.pallas_call(
        _hg_in_kernel,
        out_shape=[jax.ShapeDtypeStruct(m.shape, BF16) for m in cast]
        + [half, half, half, half, jax.ShapeDtypeStruct((t, HG_WIDTH), F32)],
        grid=(n_tiles,),
        in_specs=[row(d), mod, mod, _resident(w["wh"].shape), _resident((1, HG_WIDTH))] + cast_specs,
        out_specs=cast_specs + [row(HG_WIDTH)] * 5,
        compiler_params=_params("arbitrary"),
        name="hg_in",
    )(x, sh, sc, w["wh"], w["lb"], *cast)


def _attn_kernel(q_ref, qn_ref, k_ref, v_ref, o_ref, s0_ref, s1_ref, mx0_ref, mx1_ref, m_ref, acc_ref,
                 *, tk, nh):
    qi = pl.program_id(2)
    diag_ok = (lax.broadcasted_iota(jnp.int32, (tk, tk), 1) // CHUNK
               <= lax.broadcasted_iota(jnp.int32, (tk, tk), 0) // CHUNK)
    streams = [(hd, r) for hd in range(nh) for r in range(2)]
    sid = lambda hd, r: 2 * hd + r

    def kv_rows(blk):
        return pl.ds(pl.multiple_of(blk * tk, tk), tk)

    def rows(r):
        return slice(r * tk, (r + 1) * tk)

    def head(hd):
        return slice(hd * HEAD_PACK, (hd + 1) * HEAD_PACK)

    def lanes(x, n):
        return jnp.concatenate([x] * (n // LANES), axis=1)

    def row_max(s):
        return jnp.broadcast_to(jnp.max(s, axis=-1, keepdims=True), (tk, LANES))

    def scores(q, hd, r, blk):
        return _dot(q[rows(r), head(hd)], k_ref[kv_rows(blk), head(hd)], _NT)

    def produce(q, hd, r, blk, s_ref, mx_ref):
        s = scores(q, hd, r, blk)
        s_ref[sid(hd, r)] = s
        mx_ref[sid(hd, r)] = row_max(s)

    def block_softmax(s, mx, hd, blk):
        p = jnp.exp2(s - lanes(mx, tk)).astype(BF16)
        return mx, _dot(p, v_ref[kv_rows(blk), head(hd)])

    def merge(a, b):
        (ma, acc_a), (mb, acc_b) = a, b
        m = jnp.maximum(ma, mb)
        return m, (lanes(jnp.exp2(ma - m), HEAD_PACK) * acc_a
                   + lanes(jnp.exp2(mb - m), HEAD_PACK) * acc_b)

    def update(hd, r, s, mx, blk):
        i = sid(hd, r)
        m_old = m_ref[i]
        m_new = jnp.maximum(m_old, mx)
        m_ref[i] = m_new
        p = jnp.exp2(s - lanes(m_new, tk)).astype(BF16)
        acc_ref[i] = (lanes(jnp.exp2(m_old - m_new), HEAD_PACK) * acc_ref[i]
                      + _dot(p, v_ref[kv_rows(blk), head(hd)]))

    @pl.when(qi == 0)
    def _():
        for hd, r in streams:
            produce(q_ref, hd, r, 0, s0_ref, mx0_ref)

    for i in range(2 * nh):
        m_ref[i] = jnp.full((tk, LANES), -1e30, F32)
        acc_ref[i] = jnp.zeros((tk, HEAD_PACK), F32)

    def kv_pair(blk):
        for (cur, cmx), (nxt, nmx) in (((s0_ref, mx0_ref), (s1_ref, mx1_ref)),
                                       ((s1_ref, mx1_ref), (s0_ref, mx0_ref))):
            for hd, r in streams:
                produce(q_ref, hd, r, blk + 1, nxt, nmx)
            for hd, r in streams:
                update(hd, r, cur[sid(hd, r)], cmx[sid(hd, r)], blk)
            blk = blk + 1

    def body(j, carry):
        kv_pair(4 * j)
        kv_pair(4 * j + 2)
        return carry

    lax.fori_loop(0, qi // 2, body, 0)

    @pl.when(qi % 2 == 1)
    def _():
        kv_pair(2 * (qi - 1))

    for hd in range(nh):
        top, bot = sid(hd, 0), sid(hd, 1)
        d0 = jnp.where(diag_ok, s0_ref[top], -1e30)
        d1 = jnp.where(diag_ok, scores(q_ref, hd, 1, 2 * qi + 1), -1e30)
        part_a = block_softmax(d0, row_max(d0), hd, 2 * qi)
        part_b = block_softmax(s0_ref[bot], mx0_ref[bot], hd, 2 * qi)
        part_c = block_softmax(d1, row_max(d1), hd, 2 * qi + 1)
        finals = [merge((m_ref[top], acc_ref[top]), part_a),
                  merge(merge((m_ref[bot], acc_ref[bot]), part_b), part_c)]
        for r in range(2):
            acc = finals[r][1]
            o_ref[rows(r), hd * V_HEAD:(hd + 1) * V_HEAD] = (acc[:, :V_HEAD] / acc[:, V_HEAD:]).astype(BF16)
    for hd, r in streams:
        produce(qn_ref, hd, r, 0, s0_ref, mx0_ref)


def _attention(q, kf, vp, batch, seq, tk, nh):
    tq = 2 * tk
    nq = seq // tq
    return pl.pallas_call(
        functools.partial(_attn_kernel, tk=tk, nh=nh),
        out_shape=jax.ShapeDtypeStruct((batch * seq, N_HEADS * V_HEAD), BF16),
        grid=(batch, N_HEADS // nh, nq),
        in_specs=[pl.BlockSpec((tq, nh * HEAD_PACK), lambda b, h, i: (b * nq + i, h)),
                  pl.BlockSpec((tq, nh * HEAD_PACK), lambda b, h, i: (b * nq + jnp.minimum(i + 1, nq - 1), h)),
                  pl.BlockSpec((seq, nh * HEAD_PACK), lambda b, h, i: (b, h)),
                  pl.BlockSpec((seq, nh * HEAD_PACK), lambda b, h, i: (b, h))],
        out_specs=pl.BlockSpec((tq, nh * V_HEAD), lambda b, h, i: (b * nq + i, h)),
        scratch_shapes=[pltpu.VMEM((2 * nh, tk, tk), F32), pltpu.VMEM((2 * nh, tk, tk), F32),
                        pltpu.VMEM((2 * nh, tk, LANES), F32), pltpu.VMEM((2 * nh, tk, LANES), F32),
                        pltpu.VMEM((2 * nh, tk, LANES), F32), pltpu.VMEM((2 * nh, tk, HEAD_PACK), F32)],
        compiler_params=_params("arbitrary", "arbitrary", "arbitrary"),
        name="attn",
    )(q, q, kf, vp)


def _qlat_kernel(qn_ref, qp_ref, wuk_ref, ql_ref, qpe_ref):
    nb = ql_ref.shape[0]
    ql = _dot(qn_ref[...], wuk_ref[...], _NT)
    ql_ref[...] = ql.reshape(nb, -1, KV_RANK)
    qpe_ref[...] = qp_ref[...].astype(F32).reshape(nb, -1, LANES)


def _q_latent(q, wuk_t, nb, ln):
    t = q.shape[0]
    return pl.pallas_call(
        _qlat_kernel,
        out_shape=[jax.ShapeDtypeStruct((nb, N_HEADS, ln, KV_RANK), F32),
                   jax.ShapeDtypeStruct((nb, N_HEADS, ln, LANES), F32)],
        grid=(N_HEADS,),
        in_specs=[pl.BlockSpec((t, QK_NOPE), lambda h: (0, 2 * h)),
                  pl.BlockSpec((t, QK_NOPE), lambda h: (0, 2 * h + 1)),
                  pl.BlockSpec((None, KV_RANK, QK_NOPE), lambda h: (h, 0, 0))],
        out_specs=[pl.BlockSpec((nb, None, ln, KV_RANK), lambda h: (0, h, 0, 0)),
                   pl.BlockSpec((nb, None, ln, LANES), lambda h: (0, h, 0, 0))],
        compiler_params=_params("arbitrary"),
        name="q_latent",
    )(q, q, wuk_t)


def _dec_attn_kernel(ql_ref, qpe_ref, cckv_ref, ckpe_ref, nckv_ref, nkpe_ref, wuv_ref, o_ref, *, ln):
    rows = N_HEADS * ln
    for b in range(ql_ref.shape[0]):
        new = slice(b * ln, (b + 1) * ln)
        ql = ql_ref[b].reshape(rows, KV_RANK).astype(BF16)
        qp = qpe_ref[b].reshape(rows, LANES)[:, :QK_ROPE].astype(BF16)
        ckc = cckv_ref[b].astype(BF16)
        kpc = ckpe_ref[b].astype(BF16)
        ckn = nckv_ref[new, :].astype(BF16)
        kpn = nkpe_ref[new, :].astype(BF16)
        s_c = _dot(ql, ckc, _NT) + _dot(qp, kpc)
        s_n = _dot(ql, ckn, _NT) + _dot(qp, kpn, _NT)
        m = jnp.maximum(jnp.max(s_c, axis=-1, keepdims=True), jnp.max(s_n, axis=-1, keepdims=True))
        p_c = jnp.exp2(s_c - m)
        p_n = jnp.exp2(s_n - m)
        l = jnp.sum(p_c, axis=-1, keepdims=True) + jnp.sum(p_n, axis=-1, keepdims=True)
        ctx = (_dot(p_c.astype(BF16), ckc) + _dot(p_n.astype(BF16), ckn)) / l
        ctx = ctx.astype(BF16)
        for hd in range(N_HEADS):
            o_ref[new, hd * V_HEAD:(hd + 1) * V_HEAD] = _dot(
                ctx[hd * ln:(hd + 1) * ln], wuv_ref[hd]).astype(BF16)


def _dec_attention(ql, qpe, cache_ckv, cache_kpe_t, ckv_new, kpe_new, wuv_t, ln):
    nb, past = cache_ckv.shape[0], cache_ckv.shape[1]
    bs = 4 if nb % 4 == 0 else 1
    return pl.pallas_call(
        functools.partial(_dec_attn_kernel, ln=ln),
        out_shape=jax.ShapeDtypeStruct((nb * ln, N_HEADS * V_HEAD), BF16),
        grid=(nb // bs,),
        in_specs=[pl.BlockSpec((bs, N_HEADS, ln, KV_RANK), lambda b: (b, 0, 0, 0)),
                  pl.BlockSpec((bs, N_HEADS, ln, LANES), lambda b: (b, 0, 0, 0)),
                  pl.BlockSpec((bs, past, KV_RANK), lambda b: (b, 0, 0)),
                  pl.BlockSpec((bs, QK_ROPE, past), lambda b: (b, 0, 0)),
                  pl.BlockSpec((bs * ln, KV_RANK), lambda b: (b, 0)),
                  pl.BlockSpec((bs * ln, QK_ROPE), lambda b: (b, 0)),
                  _resident(wuv_t.shape)],
        out_specs=pl.BlockSpec((bs * ln, N_HEADS * V_HEAD), lambda b: (b, 0)),
        compiler_params=_params("arbitrary"),
        name="dec_attn",
    )(ql, qpe, cache_ckv, cache_kpe_t, ckv_new, kpe_new, wuv_t)


def _gla_consts(ln, width):
    row = lax.broadcasted_iota(jnp.int32, (8, width), 0)
    t_idx = lax.broadcasted_iota(jnp.int32, (ln, ln), 0)
    s_idx = lax.broadcasted_iota(jnp.int32, (ln, ln), 1)
    level_mask = []
    for lev in range(int(math.log2(ln))):
        level_mask.append((((t_idx ^ s_idx) >> (lev + 1)) == 0)
                          & ((t_idx & (1 << lev)) != 0) & ((s_idx & (1 << lev)) == 0))
    return {
        "scan": [row >= sft for sft in (1, 2, 4)],
        "odd": (row & 1).astype(F32),
        "low4": (row & 4) == 0,
        "sign1": jnp.where((row & 2) != 0, 1.0, -1.0).astype(F32),
        "sign2": jnp.where((row & 4) != 0, 1.0, -1.0).astype(F32),
        "level_mask": level_mask,
    }


def _gla_chunk(qb, kb, vb, g, st_ref, loc_ref, cst, ln):
    nv = ln // 8
    width = g.shape[1]
    grp = lambda x: [x[8 * i:8 * i + 8, :] for i in range(nv)]
    cat = lambda xs: jnp.concatenate(xs, axis=0)
    head = lambda x, hd: x[:, hd * HG_DIM:(hd + 1) * HG_DIM]
    q, k = grp(qb.astype(F32)), grp(kb.astype(F32))
    gs = grp(g)

    loc = gs
    for sft, keep in zip((1, 2, 4), cst["scan"]):
        loc = [x + jnp.where(keep, pltpu.roll(x, sft, 0), 0.0) for x in loc]
    loc_ref[...] = cat(loc)
    bcast = lambda r: jnp.broadcast_to(loc_ref[r:r + 1, :], (8, width))
    tot = [bcast(8 * i + 7) for i in range(nv)]
    pin = [tot[0]]
    for i in range(1, nv):
        pin.append(pin[-1] + tot[i])
    b = [loc[0]] + [loc[i] + pin[i - 1] for i in range(1, nv)]
    b_last = pin[-1]

    exps = [[gs[i] * cst["odd"] for i in range(nv)],
            [(loc[i] - jnp.where(cst["low4"], bcast(8 * i + 1), bcast(8 * i + 5))) * cst["sign1"]
             for i in range(nv)],
            [(loc[i] - bcast(8 * i + 3)) * cst["sign2"] for i in range(nv)]]
    upper = [[True] * nv, [True] * nv, [True] * nv]
    lower = [[True] * nv, [True] * nv, [True] * nv]
    for lev in range(3, int(math.log2(ln))):
        m = 1 << (lev - 3)
        e, up, lo = [], [], []
        for i in range(nv):
            mid = (i // (2 * m)) * 2 * m + m - 1
            is_up = (i // m) % 2 == 1
            if is_up:
                e.append(loc[i] if mid == i - 1 else b[i] - pin[mid])
            else:
                e.append(tot[i] - loc[i] if mid == i else pin[mid] - b[i])
            up.append(is_up)
            lo.append(not is_up)
        exps.append(e)
        upper.append(up)
        lower.append(lo)

    a = [jnp.zeros((ln, ln), F32)] * HG_HEADS
    for lev, e in enumerate(exps):
        w = [jnp.exp2(x) for x in e]
        qt = cat([q[i] * w[i] if upper[lev][i] else q[i] for i in range(nv)]).astype(BF16)
        kt = cat([k[i] * w[i] if lower[lev][i] else k[i] for i in range(nv)]).astype(BF16)
        a = [jnp.where(cst["level_mask"][lev], _dot(head(qt, hd), head(kt, hd), _NT), a[hd])
             for hd in range(HG_HEADS)]

    qd = cat([q[i] * jnp.exp2(b[i]) for i in range(nv)]).astype(BF16)
    kd = cat([k[i] * jnp.exp2(b_last - b[i]) for i in range(nv)]).astype(BF16)
    qk = cat([q[i] * k[i] for i in range(nv)])
    vf = vb.astype(F32)
    decay = jnp.exp2(b_last[0:1, :])
    outs = []
    for hd in range(HG_HEADS):
        st = st_ref[hd]
        v_h = head(vb, hd)
        o = _dot(head(qd, hd), st.astype(BF16), _NT) + _dot(a[hd].astype(BF16), v_h)
        outs.append(o + jnp.sum(head(qk, hd), axis=-1, keepdims=True) * head(vf, hd))
        st_ref[hd] = st * head(decay, hd) + _dot(v_h, head(kd, hd), _TN)
    return outs


def _hgrn_kernel(*refs, ln, n_chunks, has_init):
    if has_init:
        qh_ref, kh_ref, vh_ref, gate_ref, g_ref, gw_ref, s0_ref, o_ref, sout_ref, st_ref, b_ref = refs
    else:
        qh_ref, kh_ref, vh_ref, gate_ref, g_ref, gw_ref, o_ref, sout_ref, st_ref, b_ref = refs
        s0_ref = None
    si = pl.program_id(1)

    @pl.when(si == 0)
    def _():
        for hd in range(HG_HEADS):
            st_ref[hd] = s0_ref[hd].T if has_init else jnp.zeros((HG_DIM, HG_DIM), F32)

    gw = gw_ref[...]
    cst = _gla_consts(ln, HG_WIDTH)

    def body(c, carry):
        rows = pl.ds(pl.multiple_of(c * ln, ln), ln)
        outs = _gla_chunk(qh_ref[rows, :], kh_ref[rows, :], vh_ref[rows, :], g_ref[rows, :],
                          st_ref, b_ref, cst, ln)
        for hd in range(HG_HEADS):
            lanes = slice(hd * HG_DIM, (hd + 1) * HG_DIM)
            o_ref[rows, lanes] = (_rms(outs[hd]) * gw * gate_ref[rows, lanes].astype(F32)).astype(BF16)
        return carry

    lax.fori_loop(0, n_chunks, body, 0, unroll=8 if n_chunks % 8 == 0 else 1)

    @pl.when(si == pl.num_programs(1) - 1)
    def _():
        for hd in range(HG_HEADS):
            sout_ref[hd] = st_ref[hd].T


def _hgrn(qh, kh, vh, gate, g, gw, state0, batch, seq, ln, ts):
    ns = seq // ts
    has_init = state0 is not None
    row = pl.BlockSpec((ts, HG_WIDTH), lambda b, s: (b * ns + s, 0))
    st_spec = pl.BlockSpec((None, HG_HEADS, HG_DIM, HG_DIM), lambda b, s: (b, 0, 0, 0))
    in_specs = [row] * 5 + [_resident((1, HG_DIM))]
    args = [qh, kh, vh, gate, g, gw]
    if has_init:
        in_specs.append(st_spec)
        args.append(state0)
    return pl.pallas_call(
        functools.partial(_hgrn_kernel, ln=ln, n_chunks=ts // ln, has_init=has_init),
        out_shape=[jax.ShapeDtypeStruct((batch * seq, HG_WIDTH), BF16),
                   jax.ShapeDtypeStruct((batch, HG_HEADS, HG_DIM, HG_DIM), F32)],
        grid=(batch, ns),
        in_specs=in_specs,
        out_specs=[row, st_spec],
        scratch_shapes=[pltpu.VMEM((HG_HEADS, HG_DIM, HG_DIM), F32),
                        pltpu.VMEM((ln, HG_WIDTH), F32)],
        compiler_params=_params("arbitrary", "arbitrary"),
        name="hgrn",
    )(*args)


def _out_proj_kernel(x_ref, gt_ref, om_ref, oh_ref, w_ref, o_ref):
    n = om_ref.shape[1]
    mix = _dot(om_ref[...], w_ref[0:n, :]) + _dot(oh_ref[...], w_ref[n:, :])
    o_ref[...] = x_ref[...] + gt_ref[...] * mix


def _out_proj(x, gt, o_mla, o_hg, w_out, tm):
    t, d = x.shape
    n_tiles = t // tm
    tiles_per_group = n_tiles // gt.shape[0]
    r = gt.shape[1]
    row = lambda n: pl.BlockSpec((tm, n), lambda i: (i, 0))
    return pl.pallas_call(
        _out_proj_kernel,
        out_shape=jax.ShapeDtypeStruct((t, d), F32),
        grid=(n_tiles,),
        in_specs=[row(d), pl.BlockSpec((None, r, d), lambda i: (i // tiles_per_group, 0, 0)),
                  row(o_mla.shape[1]), row(o_hg.shape[1]), _resident(w_out.shape)],
        out_specs=row(d),
        compiler_params=_params("arbitrary"),
        name="out_proj",
    )(x, gt, o_mla, o_hg, w_out)


def _ffn_kernel(x_ref, sh_ref, sc_ref, gt_ref, w1_ref, w2_ref, gf_ref, o_ref, h_ref):
    f = pl.program_id(1)
    last = pl.num_programs(1) - 1

    def mlp(h):
        a = jnp.maximum(_dot(h, w1_ref[...]), 0.0)
        return _dot((a * a).astype(BF16), w2_ref[...])

    @pl.when(f == 0)
    def _():
        h = (_rms(x_ref[...]) * (1.0 + sc_ref[...]) + sh_ref[...]).astype(BF16)
        h_ref[...] = h
        o_ref[...] = mlp(h)

    @pl.when(jnp.logical_and(f > 0, f < last))
    def _():
        o_ref[...] += mlp(h_ref[...])

    @pl.when(f == last)
    def _():
        x2 = x_ref[...] + gt_ref[...] * (o_ref[...] + mlp(h_ref[...]))
        o_ref[...] = _rms(x2) * gf_ref[...]


def _ffn(x, sh, sc, gt, w1, w2, gf, tm, tf):
    t, d = x.shape
    dff = w1.shape[1]
    n_tiles = t // tm
    tiles_per_group = n_tiles // sh.shape[0]
    r = sh.shape[1]
    row = pl.BlockSpec((tm, d), lambda i, f: (i, 0))
    mod = pl.BlockSpec((None, r, d), lambda i, f: (i // tiles_per_group, 0, 0))
    return pl.pallas_call(
        _ffn_kernel,
        out_shape=jax.ShapeDtypeStruct((t, d), F32),
        grid=(n_tiles, dff // tf),
        in_specs=[row, mod, mod, mod,
                  pl.BlockSpec((d, tf), lambda i, f: (0, f)),
                  pl.BlockSpec((tf, d), lambda i, f: (f, 0)),
                  pl.BlockSpec((1, d), lambda i, f: (0, 0))],
        out_specs=row,
        scratch_shapes=[pltpu.VMEM((tm, d), BF16)],
        compiler_params=_params("arbitrary", "arbitrary", vmem_limit=FFN_VMEM_LIMIT),
        name="ffn",
    )(x, sh, sc, gt, w1, w2, gf)


def _rot_half_cols(w, axis=-1):
    lo, hi = jnp.split(w, 2, axis=axis)
    return jnp.concatenate([-hi, lo], axis=axis)


def _prep_weights(w_in, w_uq, g_q, g_kv, w_uk, w_uv, lb, g_hgrn, w_out, w_ff1, w_ff2, g_final):
    d = w_in.shape[0]
    o_kpe = Q_RANK + KV_RANK
    o_hg = o_kpe + QK_ROPE
    w_in_t = jnp.swapaxes(w_in, 0, 1)
    w_mla_t = lax.optimization_barrier(w_in_t[:o_hg])
    w_kpe_t = w_mla_t[o_kpe:]
    q_scale = ATTN_SCALE * math.log2(math.e)
    wq = w_uq.reshape(Q_RANK, N_HEADS, QK_NOPE + QK_ROPE) * q_scale
    wq_pe = wq[..., QK_NOPE:]
    wuq = jnp.concatenate([wq[..., :QK_NOPE], wq_pe, _rot_half_cols(wq_pe)], axis=-1)
    return {
        "wa": jnp.concatenate([w_mla_t, _rot_half_cols(w_kpe_t, axis=0)], axis=0).astype(BF16),
        "wh_src": (w_in_t, o_hg, w_in_t.shape[0] - o_hg),
        "wuq": wuq.reshape(Q_RANK, N_HEADS * HEAD_PACK).astype(BF16),
        "wukv": jnp.concatenate([w_uk.reshape(KV_RANK, -1), w_uv.reshape(KV_RANK, -1)], axis=-1).astype(BF16),
        "wuk_t": jnp.transpose(w_uk, (1, 0, 2)).astype(BF16),
        "wuv_t": jnp.transpose(w_uv, (1, 0, 2)).astype(BF16),
        "gq": g_q.reshape(1, -1), "gkv": g_kv.reshape(1, -1),
        "lb": lb.reshape(1, -1), "ghg": g_hgrn.reshape(1, -1),
        "wout_f32": w_out, "w1_f32": w_ff1, "w2_f32": w_ff2,
        "gfin": g_final.reshape(1, d),
    }


def _layer(x, mods, w, batch, seq, cache, past):
    sh1, sc1, gt1, sh2, sc2, gt2 = mods
    t = x.shape[0]
    tm = min(512, t)
    prompt = cache is None
    if prompt:
        q, ckv, kpe, kf, v, w["wh"] = _mla_in(x, sh1, sc1, w, tm, True, past, seq, w["wh_src"])
        o_mla = _attention(q, kf, v, batch, seq, 512, 2)
    else:
        cache_ckv, cache_kpe, state0 = cache
        q, ckv, kpe = _mla_in(x, sh1, sc1, w, tm, False, past, seq)
        ql, qpe = _q_latent(q, w["wuk_t"], batch, seq)
        o_mla = _dec_attention(ql, qpe, cache_ckv, cache_kpe, ckv, kpe, w["wuv_t"], seq)
    if prompt:
        w["w1"], w["w2"], w["wout"], qh, kh, vh, gate, g = _hg_in(
            x, sh1, sc1, w, tm, (w["w1_f32"], w["w2_f32"], w["wout_f32"]))
        o_hg, s_fin = _hgrn(qh, kh, vh, gate, g, w["ghg"], None, batch, seq, CHUNK, 1024)
    else:
        qh, kh, vh, gate, g = _hg_in(x, sh1, sc1, w, tm)
        o_hg, s_fin = _hgrn(qh, kh, vh, gate, g, w["ghg"], state0, batch, seq, seq, seq)
    x1 = _out_proj(x, gt1, o_mla, o_hg, w["wout"], tm)
    y = _ffn(x1, sh2, sc2, gt2, w["w1"], w["w2"], w["gfin"], min(1024, t), 1024)
    return y, ckv, kpe, s_fin


def kernel(x_prompt, x_sample, c_prompt, c_sample, cache_ckv, cache_kpe, state_hgrn, w_ada, b_ada, w_in, w_uq, g_q, g_kv, w_uk, w_uv, hg_lower_bounds, g_hgrn, w_out, w_ff1, w_ff2, g_final):
    depth = w_ada.shape[0]
    assert depth == 1, "single-layer step"
    bp, sp, d = x_prompt.shape
    bs, ss, _ = x_sample.shape
    past = cache_ckv.shape[2]

    lb_all = jnp.cumsum(jax.nn.softmax(hg_lower_bounds.astype(F32), axis=0), axis=0)
    w = _prep_weights(w_in[0], w_uq[0], g_q[0], g_kv[0], w_uk[0], w_uv[0], lb_all[0], g_hgrn[0],
                      w_out[0], w_ff1[0], w_ff2[0], g_final)

    c_all = jnp.concatenate([c_prompt, c_sample], axis=0)
    n_c = c_all.shape[0]
    c_pad = jnp.pad(c_all, ((0, -n_c % 8), (0, 0)))
    mod = _modulation(c_pad, w_ada[0], b_ada[0])[:n_c]
    mod_p = [m.reshape(bp, 1, d) for m in jnp.split(mod[:bp], 6, axis=-1)]
    mod_s = [jnp.repeat(m, ss, axis=0).reshape(1, bs * ss, d) for m in jnp.split(mod[bp:], 6, axis=-1)]

    yp, ckv_p, kpe_p, st_p = _layer(x_prompt.reshape(bp * sp, d), mod_p, w, bp, sp, None, 0)
    ys, ckv_s, kpe_s, st_s = _layer(x_sample.reshape(bs * ss, d), mod_s, w, bs, ss,
                                    (cache_ckv[0], jnp.swapaxes(cache_kpe[0], 1, 2), state_hgrn[0]), past)
    return (yp.reshape(bp, sp, d), ys.reshape(bs, ss, d),
            ckv_p.reshape(1, bp, sp, KV_RANK), kpe_p.reshape(1, bp, sp, QK_ROPE), st_p[None],
            ckv_s.reshape(1, bs, ss, KV_RANK), kpe_s.reshape(1, bs, ss, QK_ROPE), st_s[None])
```

```python
import functools
import math

import jax
import jax.numpy as jnp
from jax import lax
from jax.experimental import pallas as pl
from jax.experimental.pallas import tpu as pltpu

F32 = jnp.float32
BF16 = jnp.bfloat16

EPS = 1e-6
CHUNK = 64
N_HEADS = 8
QK_NOPE = 128
QK_ROPE = 64
V_HEAD = 128
Q_RANK = 512
KV_RANK = 512
ROPE_BASE = 10000.0
ATTN_SCALE = (QK_NOPE + QK_ROPE) ** -0.5
HG_HEADS = 8
HG_DIM = 128
HG_WIDTH = HG_HEADS * HG_DIM
HEAD_PACK = 2 * QK_NOPE
LANES = 128
MASKED = -1e30
VMEM_LIMIT = 56 * 1024 * 1024
FFN_VMEM_LIMIT = 62 * 1024 * 1024

PROJ_ROWS = 512
FFN_ROWS, FFN_COLS = 1024, 1024
MOD_COLS = 1024
ATTN_KV_ROWS = 512
ATTN_HEADS = 2
HGRN_ROWS = 1024
HGRN_UNROLL = 8
DEC_SEQS = 4

_NT = (((1,), (1,)), ((), ()))
_TN = (((0,), (0,)), ((), ()))


def _rms(x):
    return x * lax.rsqrt(jnp.mean(x * x, axis=-1, keepdims=True) + EPS)


def _dot(a, b, dims=None):
    if dims is None:
        return jnp.dot(a, b, preferred_element_type=F32)
    return lax.dot_general(a, b, dims, preferred_element_type=F32)


def _resident(shape):
    return pl.BlockSpec(shape, lambda *_: (0,) * len(shape), pipeline_mode=pl.Buffered(1))


def _params(*sem, vmem_limit=VMEM_LIMIT):
    return pltpu.CompilerParams(dimension_semantics=sem, vmem_limit_bytes=vmem_limit)


def _mod_kernel(c_ref, w_ref, b_ref, o_ref):
    c = c_ref[...]
    a = (c * jax.nn.sigmoid(c)).astype(BF16)
    o_ref[...] = _dot(a, w_ref[...].astype(BF16)) + b_ref[...]


def _modulation(c_all, w_ada, b_ada):
    m, d = c_all.shape
    n = w_ada.shape[1]
    tn = MOD_COLS
    return pl.pallas_call(
        _mod_kernel,
        out_shape=jax.ShapeDtypeStruct((m, n), F32),
        grid=(n // tn,),
        in_specs=[pl.BlockSpec((m, d), lambda j: (0, 0)),
                  pl.BlockSpec((d, tn), lambda j: (0, j)),
                  pl.BlockSpec((1, tn), lambda j: (0, j))],
        out_specs=pl.BlockSpec((m, tn), lambda j: (0, j)),
        compiler_params=_params("arbitrary"),
        name="mod",
    )(c_all, w_ada, b_ada.reshape(1, n))


def _mla_in_kernel(x_ref, sh_ref, sc_ref, cr_ref, sr_ref, wa_ref, wuq_ref, wukv_ref, gq_ref, gkv_ref,
                   *refs, pos_base, pos_period, n_cast, with_kv):
    cast_refs, (q_ref, ckv_ref, kpe_ref), refs = refs[:n_cast], refs[n_cast:n_cast + 3], refs[n_cast + 3:]
    if n_cast:
        rows = cast_refs[0].shape[0]
        for j, src in enumerate(cast_refs):
            refs[-1][j * rows:(j + 1) * rows, :] = src[...].astype(BF16)
    x = x_ref[...]
    tm = x.shape[0]
    h = (_rms(x) * (1.0 + sc_ref[...]) + sh_ref[...]).astype(BF16)

    half = QK_ROPE // 2
    lane = lax.broadcasted_iota(jnp.int32, (8, LANES), 1)
    base = (pos_base + lax.rem(pl.program_id(0) * tm, pos_period)).astype(F32)
    ang_b = base * jnp.exp(-math.log(ROPE_BASE) * lax.rem(lane, half).astype(F32) / half)
    cb, sb = jnp.cos(ang_b)[0:1], jnp.sin(ang_b)[0:1]
    cr, sr = cr_ref[...], sr_ref[...]
    cos = cr * cb - sr * sb
    sin = sr * cb + cr * sb

    def rope(t):
        return t * cos + pltpu.roll(t, QK_ROPE, 1) * sin

    a = _dot(h, wa_ref[...], _NT)
    cqn = (_rms(a[:, :Q_RANK]) * gq_ref[...]).astype(BF16)
    ckvn = _rms(a[:, Q_RANK:Q_RANK + KV_RANK]) * gkv_ref[...]
    ckv_ref[...] = ckvn

    q = _dot(cqn, wuq_ref[...])
    for hd in range(N_HEADS):
        lo = hd * HEAD_PACK
        q_ref[:, lo:lo + QK_NOPE] = q[:, lo:lo + QK_NOPE].astype(BF16)
        q_ref[:, lo + QK_NOPE:lo + HEAD_PACK] = rope(q[:, lo + QK_NOPE:lo + HEAD_PACK]).astype(BF16)

    kp = rope(a[:, Q_RANK + KV_RANK:])
    kpe_ref[...] = kp[:, :QK_ROPE]

    if with_kv:
        kf_ref, v_ref = refs[:2]
        kv = _dot(ckvn.astype(BF16), wukv_ref[...])
        kpb = kp.astype(BF16)
        for hd in range(N_HEADS):
            lo = hd * HEAD_PACK
            kf_ref[:, lo:lo + QK_NOPE] = kv[:, hd * QK_NOPE:(hd + 1) * QK_NOPE].astype(BF16)
            kf_ref[:, lo + QK_NOPE:lo + HEAD_PACK] = kpb
            vo = N_HEADS * QK_NOPE + hd * V_HEAD
            v_ref[:, lo:lo + V_HEAD] = kv[:, vo:vo + V_HEAD].astype(BF16)
            v_ref[:, lo + V_HEAD:lo + HEAD_PACK] = jnp.ones((kv.shape[0], HEAD_PACK - V_HEAD), BF16)


def _mla_in(x, sh, sc, w, tm, with_kv, pos_base, pos_period, cast_src=None):
    t, d = x.shape
    n_tiles = t // tm
    tiles_per_group = n_tiles // sh.shape[0]
    r = sh.shape[1]
    row = lambda n: pl.BlockSpec((tm, n), lambda i: (i, 0))
    mod = pl.BlockSpec((None, r, d), lambda i: (i // tiles_per_group, 0, 0))
    packed = N_HEADS * HEAD_PACK
    half = QK_ROPE // 2
    lane = jnp.arange(LANES)
    inv_freq = jnp.exp(-math.log(ROPE_BASE) * (lane % half).astype(F32) / half)
    ang = (jnp.arange(tm) % min(tm, pos_period)).astype(F32)[:, None] * inv_freq[None, :]
    cr = jnp.where(lane < QK_ROPE, jnp.cos(ang), 0.0)
    sr = jnp.where(lane < QK_ROPE, jnp.sin(ang), 0.0)
    out_shape = [jax.ShapeDtypeStruct((t, packed), BF16),
                 jax.ShapeDtypeStruct((t, KV_RANK), F32),
                 jax.ShapeDtypeStruct((t, QK_ROPE), F32)]
    out_specs = [row(packed), row(KV_RANK), row(QK_ROPE)]
    if with_kv:
        out_shape += [jax.ShapeDtypeStruct((t, packed), BF16),
                      jax.ShapeDtypeStruct((t, packed), BF16)]
        out_specs += [row(packed), row(packed)]
    cast_args, cast_specs = [], []
    if cast_src is not None:
        src, first, n_rows = cast_src
        blk = n_rows // n_tiles
        sub = math.gcd(first, blk)
        for j in range(blk // sub):
            cast_args.append(src)
            cast_specs.append(pl.BlockSpec((sub, d), lambda i, j=j: (first // sub + i * (blk // sub) + j, 0)))
        out_shape.append(jax.ShapeDtypeStruct((n_rows, d), BF16))
        out_specs.append(pl.BlockSpec((blk, d), lambda i: (i, 0)))
    return pl.pallas_call(
        functools.partial(_mla_in_kernel, pos_base=pos_base, pos_period=pos_period,
                          n_cast=len(cast_args), with_kv=with_kv),
        out_shape=out_shape,
        grid=(n_tiles,),
        in_specs=[row(d), mod, mod, _resident((tm, LANES)), _resident((tm, LANES)),
                  _resident(w["wa"].shape), _resident(w["wuq"].shape), _resident(w["wukv"].shape),
                  _resident((1, Q_RANK)), _resident((1, KV_RANK))] + cast_specs,
        out_specs=out_specs,
        compiler_params=_params("arbitrary"),
        name="mla_in",
    )(x, sh, sc, cr, sr, w["wa"], w["wuq"], w["wukv"], w["gq"], w["gkv"], *cast_args)


def _hg_in_kernel(x_ref, sh_ref, sc_ref, wh_ref, lb_ref, *refs):
    qh_ref, kh_ref, vh_ref, gate_ref, g_ref = refs[-5:]
    n_cast = (len(refs) - 5) // 2
    for src, dst in zip(refs[:n_cast], refs[n_cast:2 * n_cast]):
        dst[...] = src[...].astype(BF16)
    x = x_ref[...]
    h = (_rms(x) * (1.0 + sc_ref[...]) + sh_ref[...]).astype(BF16)
    n = HG_WIDTH
    hf = _dot(h, wh_ref[n:2 * n, :], _NT)
    lb = lb_ref[...]
    f = lb + (1.0 - lb) * jax.nn.sigmoid(hf)
    kh_ref[...] = (1.0 - f).astype(BF16)
    g_ref[...] = jnp.log2(f)
    hq = _dot(h, wh_ref[0:n, :], _NT)
    qh_ref[...] = (hq * jax.nn.sigmoid(hq) * (HG_DIM ** -0.5)).astype(BF16)
    hg = _dot(h, wh_ref[3 * n:4 * n, :], _NT)
    gate_ref[...] = (hg * jax.nn.sigmoid(hg)).astype(BF16)
    vh_ref[...] = _dot(h, wh_ref[2 * n:3 * n, :], _NT).astype(BF16)


def _hg_in(x, sh, sc, w, tm, cast=()):
    t, d = x.shape
    n_tiles = t // tm
    tiles_per_group = n_tiles // sh.shape[0]
    r = sh.shape[1]
    row = lambda n: pl.BlockSpec((tm, n), lambda i: (i, 0))
    mod = pl.BlockSpec((None, r, d), lambda i: (i // tiles_per_group, 0, 0))
    half = jax.ShapeDtypeStruct((t, HG_WIDTH), BF16)
    cast_specs = [pl.BlockSpec((m.shape[0] // n_tiles, m.shape[1]), lambda i: (i, 0)) for m in cast]
    return pl.pallas_call(
        _hg_in_kernel,
        out_shape=[jax.ShapeDtypeStruct(m.shape, BF16) for m in cast]
        + [half, half, half, half, jax.ShapeDtypeStruct((t, HG_WIDTH), F32)],
        grid=(n_tiles,),
        in_specs=[row(d), mod, mod, _resident(w["wh"].shape), _resident((1, HG_WIDTH))] + cast_specs,
        out_specs=cast_specs + [row(HG_WIDTH)] * 5,
        compiler_params=_params("arbitrary"),
        name="hg_in",
    )(x, sh, sc, w["wh"], w["lb"], *cast)


def _attn_kernel(q_ref, qn_ref, k_ref, v_ref, o_ref, s0_ref, s1_ref, mx0_ref, mx1_ref, m_ref, acc_ref,
                 *, tk, nh):
    qi = pl.program_id(2)
    diag_ok = (lax.broadcasted_iota(jnp.int32, (tk, tk), 1) // CHUNK
               <= lax.broadcasted_iota(jnp.int32, (tk, tk), 0) // CHUNK)
    streams = [(hd, r) for hd in range(nh) for r in range(2)]
    sid = lambda hd, r: 2 * hd + r

    def kv_rows(blk):
        return pl.ds(pl.multiple_of(blk * tk, tk), tk)

    def rows(r):
        return slice(r * tk, (r + 1) * tk)

    def head(hd):
        return slice(hd * HEAD_PACK, (hd + 1) * HEAD_PACK)

    def lanes(x, n):
        return jnp.concatenate([x] * (n // LANES), axis=1)

    def row_max(s):
        return jnp.broadcast_to(jnp.max(s, axis=-1, keepdims=True), (tk, LANES))

    def scores(q, hd, r, blk):
        return _dot(q[rows(r), head(hd)], k_ref[kv_rows(blk), head(hd)], _NT)

    def produce(q, hd, r, blk, s_ref, mx_ref):
        s = scores(q, hd, r, blk)
        s_ref[sid(hd, r)] = s
        mx_ref[sid(hd, r)] = row_max(s)

    def block_softmax(s, mx, hd, blk):
        p = jnp.exp2(s - lanes(mx, tk)).astype(BF16)
        return mx, _dot(p, v_ref[kv_rows(blk), head(hd)])

    def merge(a, b):
        (ma, acc_a), (mb, acc_b) = a, b
        m = jnp.maximum(ma, mb)
        return m, (lanes(jnp.exp2(ma - m), HEAD_PACK) * acc_a
                   + lanes(jnp.exp2(mb - m), HEAD_PACK) * acc_b)

    def update(hd, r, s, mx, blk):
        i = sid(hd, r)
        m_old = m_ref[i]
        m_new = jnp.maximum(m_old, mx)
        m_ref[i] = m_new
        p = jnp.exp2(s - lanes(m_new, tk)).astype(BF16)
        acc_ref[i] = (lanes(jnp.exp2(m_old - m_new), HEAD_PACK) * acc_ref[i]
                      + _dot(p, v_ref[kv_rows(blk), head(hd)]))

    @pl.when(qi == 0)
    def _():
        for hd, r in streams:
            produce(q_ref, hd, r, 0, s0_ref, mx0_ref)

    for i in range(2 * nh):
        m_ref[i] = jnp.full((tk, LANES), MASKED, F32)
        acc_ref[i] = jnp.zeros((tk, HEAD_PACK), F32)

    def kv_pair(blk):
        for (cur, cmx), (nxt, nmx) in (((s0_ref, mx0_ref), (s1_ref, mx1_ref)),
                                       ((s1_ref, mx1_ref), (s0_ref, mx0_ref))):
            for hd, r in streams:
                produce(q_ref, hd, r, blk + 1, nxt, nmx)
            for hd, r in streams:
                update(hd, r, cur[sid(hd, r)], cmx[sid(hd, r)], blk)
            blk = blk + 1

    def body(j, carry):
        kv_pair(4 * j)
        kv_pair(4 * j + 2)
        return carry

    lax.fori_loop(0, qi // 2, body, 0)

    @pl.when(qi % 2 == 1)
    def _():
        kv_pair(2 * (qi - 1))

    for hd in range(nh):
        top, bot = sid(hd, 0), sid(hd, 1)
        d0 = jnp.where(diag_ok, s0_ref[top], MASKED)
        d1 = jnp.where(diag_ok, scores(q_ref, hd, 1, 2 * qi + 1), MASKED)
        part_a = block_softmax(d0, row_max(d0), hd, 2 * qi)
        part_b = block_softmax(s0_ref[bot], mx0_ref[bot], hd, 2 * qi)
        part_c = block_softmax(d1, row_max(d1), hd, 2 * qi + 1)
        finals = [merge((m_ref[top], acc_ref[top]), part_a),
                  merge(merge((m_ref[bot], acc_ref[bot]), part_b), part_c)]
        for r in range(2):
            acc = finals[r][1]
            o_ref[rows(r), hd * V_HEAD:(hd + 1) * V_HEAD] = (acc[:, :V_HEAD] / acc[:, V_HEAD:]).astype(BF16)
    for hd, r in streams:
        produce(qn_ref, hd, r, 0, s0_ref, mx0_ref)


def _attention(q, kf, vp, batch, seq, tk, nh):
    tq = 2 * tk
    nq = seq // tq
    return pl.pallas_call(
        functools.partial(_attn_kernel, tk=tk, nh=nh),
        out_shape=jax.ShapeDtypeStruct((batch * seq, N_HEADS * V_HEAD), BF16),
        grid=(batch, N_HEADS // nh, nq),
        in_specs=[pl.BlockSpec((tq, nh * HEAD_PACK), lambda b, h, i: (b * nq + i, h)),
                  pl.BlockSpec((tq, nh * HEAD_PACK), lambda b, h, i: (b * nq + jnp.minimum(i + 1, nq - 1), h)),
                  pl.BlockSpec((seq, nh * HEAD_PACK), lambda b, h, i: (b, h)),
                  pl.BlockSpec((seq, nh * HEAD_PACK), lambda b, h, i: (b, h))],
        out_specs=pl.BlockSpec((tq, nh * V_HEAD), lambda b, h, i: (b * nq + i, h)),
        scratch_shapes=[pltpu.VMEM((2 * nh, tk, tk), F32), pltpu.VMEM((2 * nh, tk, tk), F32),
                        pltpu.VMEM((2 * nh, tk, LANES), F32), pltpu.VMEM((2 * nh, tk, LANES), F32),
                        pltpu.VMEM((2 * nh, tk, LANES), F32), pltpu.VMEM((2 * nh, tk, HEAD_PACK), F32)],
        compiler_params=_params("arbitrary", "arbitrary", "arbitrary"),
        name="attn",
    )(q, q, kf, vp)


def _qlat_kernel(qn_ref, qp_ref, wuk_ref, ql_ref, qpe_ref):
    nb = ql_ref.shape[0]
    ql = _dot(qn_ref[...], wuk_ref[...], _NT)
    ql_ref[...] = ql.reshape(nb, -1, KV_RANK)
    qpe_ref[...] = qp_ref[...].astype(F32).reshape(nb, -1, LANES)


def _q_latent(q, wuk_t, nb, ln):
    t = q.shape[0]
    return pl.pallas_call(
        _qlat_kernel,
        out_shape=[jax.ShapeDtypeStruct((nb, N_HEADS, ln, KV_RANK), F32),
                   jax.ShapeDtypeStruct((nb, N_HEADS, ln, LANES), F32)],
        grid=(N_HEADS,),
        in_specs=[pl.BlockSpec((t, QK_NOPE), lambda h: (0, 2 * h)),
                  pl.BlockSpec((t, QK_NOPE), lambda h: (0, 2 * h + 1)),
                  pl.BlockSpec((None, KV_RANK, QK_NOPE), lambda h: (h, 0, 0))],
        out_specs=[pl.BlockSpec((nb, None, ln, KV_RANK), lambda h: (0, h, 0, 0)),
                   pl.BlockSpec((nb, None, ln, LANES), lambda h: (0, h, 0, 0))],
        compiler_params=_params("arbitrary"),
        name="q_latent",
    )(q, q, wuk_t)


def _dec_attn_kernel(ql_ref, qpe_ref, cckv_ref, ckpe_ref, nckv_ref, nkpe_ref, wuv_ref, o_ref, *, ln):
    rows = N_HEADS * ln
    for b in range(ql_ref.shape[0]):
        new = slice(b * ln, (b + 1) * ln)
        ql = ql_ref[b].reshape(rows, KV_RANK).astype(BF16)
        qp = qpe_ref[b].reshape(rows, LANES)[:, :QK_ROPE].astype(BF16)
        ckc = cckv_ref[b].astype(BF16)
        kpc = ckpe_ref[b].astype(BF16)
        ckn = nckv_ref[new, :].astype(BF16)
        kpn = nkpe_ref[new, :].astype(BF16)
        s_c = _dot(ql, ckc, _NT) + _dot(qp, kpc)
        s_n = _dot(ql, ckn, _NT) + _dot(qp, kpn, _NT)
        m = jnp.maximum(jnp.max(s_c, axis=-1, keepdims=True), jnp.max(s_n, axis=-1, keepdims=True))
        p_c = jnp.exp2(s_c - m)
        p_n = jnp.exp2(s_n - m)
        l = jnp.sum(p_c, axis=-1, keepdims=True) + jnp.sum(p_n, axis=-1, keepdims=True)
        ctx = (_dot(p_c.astype(BF16), ckc) + _dot(p_n.astype(BF16), ckn)) / l
        ctx = ctx.astype(BF16)
        for hd in range(N_HEADS):
            o_ref[new, hd * V_HEAD:(hd + 1) * V_HEAD] = _dot(
                ctx[hd * ln:(hd + 1) * ln], wuv_ref[hd]).astype(BF16)


def _dec_attention(ql, qpe, cache_ckv, cache_kpe_t, ckv_new, kpe_new, wuv_t, ln):
    nb, past = cache_ckv.shape[0], cache_ckv.shape[1]
    bs = DEC_SEQS if nb % DEC_SEQS == 0 else 1
    return pl.pallas_call(
        functools.partial(_dec_attn_kernel, ln=ln),
        out_shape=jax.ShapeDtypeStruct((nb * ln, N_HEADS * V_HEAD), BF16),
        grid=(nb // bs,),
        in_specs=[pl.BlockSpec((bs, N_HEADS, ln, KV_RANK), lambda b: (b, 0, 0, 0)),
                  pl.BlockSpec((bs, N_HEADS, ln, LANES), lambda b: (b, 0, 0, 0)),
                  pl.BlockSpec((bs, past, KV_RANK), lambda b: (b, 0, 0)),
                  pl.BlockSpec((bs, QK_ROPE, past), lambda b: (b, 0, 0)),
                  pl.BlockSpec((bs * ln, KV_RANK), lambda b: (b, 0)),
                  pl.BlockSpec((bs * ln, QK_ROPE), lambda b: (b, 0)),
                  _resident(wuv_t.shape)],
        out_specs=pl.BlockSpec((bs * ln, N_HEADS * V_HEAD), lambda b: (b, 0)),
        compiler_params=_params("arbitrary"),
        name="dec_attn",
    )(ql, qpe, cache_ckv, cache_kpe_t, ckv_new, kpe_new, wuv_t)


def _gla_consts(ln, width):
    row = lax.broadcasted_iota(jnp.int32, (8, width), 0)
    t_idx = lax.broadcasted_iota(jnp.int32, (ln, ln), 0)
    s_idx = lax.broadcasted_iota(jnp.int32, (ln, ln), 1)
    level_mask = []
    for lev in range(int(math.log2(ln))):
        level_mask.append((((t_idx ^ s_idx) >> (lev + 1)) == 0)
                          & ((t_idx & (1 << lev)) != 0) & ((s_idx & (1 << lev)) == 0))
    return {
        "scan": [row >= sft for sft in (1, 2, 4)],
        "odd": (row & 1).astype(F32),
        "low4": (row & 4) == 0,
        "sign1": jnp.where((row & 2) != 0, 1.0, -1.0).astype(F32),
        "sign2": jnp.where((row & 4) != 0, 1.0, -1.0).astype(F32),
        "level_mask": level_mask,
    }


def _gla_chunk(qb, kb, vb, g, st_ref, loc_ref, cst, ln):
    nv = ln // 8
    width = g.shape[1]
    grp = lambda x: [x[8 * i:8 * i + 8, :] for i in range(nv)]
    cat = lambda xs: jnp.concatenate(xs, axis=0)
    head = lambda x, hd: x[:, hd * HG_DIM:(hd + 1) * HG_DIM]
    q, k = grp(qb.astype(F32)), grp(kb.astype(F32))
    gs = grp(g)

    loc = gs
    for sft, keep in zip((1, 2, 4), cst["scan"]):
        loc = [x + jnp.where(keep, pltpu.roll(x, sft, 0), 0.0) for x in loc]
    loc_ref[...] = cat(loc)
    bcast = lambda r: jnp.broadcast_to(loc_ref[r:r + 1, :], (8, width))
    tot = [bcast(8 * i + 7) for i in range(nv)]
    pin = [tot[0]]
    for i in range(1, nv):
        pin.append(pin[-1] + tot[i])
    b = [loc[0]] + [loc[i] + pin[i - 1] for i in range(1, nv)]
    b_last = pin[-1]

    exps = [[gs[i] * cst["odd"] for i in range(nv)],
            [(loc[i] - jnp.where(cst["low4"], bcast(8 * i + 1), bcast(8 * i + 5))) * cst["sign1"]
             for i in range(nv)],
            [(loc[i] - bcast(8 * i + 3)) * cst["sign2"] for i in range(nv)]]
    upper = [[True] * nv, [True] * nv, [True] * nv]
    lower = [[True] * nv, [True] * nv, [True] * nv]
    for lev in range(3, int(math.log2(ln))):
        m = 1 << (lev - 3)
        e, up, lo = [], [], []
        for i in range(nv):
            mid = (i // (2 * m)) * 2 * m + m - 1
            is_up = (i // m) % 2 == 1
            if is_up:
                e.append(loc[i] if mid == i - 1 else b[i] - pin[mid])
            else:
                e.append(tot[i] - loc[i] if mid == i else pin[mid] - b[i])
            up.append(is_up)
            lo.append(not is_up)
        exps.append(e)
        upper.append(up)
        lower.append(lo)

    a = [jnp.zeros((ln, ln), F32)] * HG_HEADS
    for lev, e in enumerate(exps):
        w = [jnp.exp2(x) for x in e]
        qt = cat([q[i] * w[i] if upper[lev][i] else q[i] for i in range(nv)]).astype(BF16)
        kt = cat([k[i] * w[i] if lower[lev][i] else k[i] for i in range(nv)]).astype(BF16)
        a = [jnp.where(cst["level_mask"][lev], _dot(head(qt, hd), head(kt, hd), _NT), a[hd])
             for hd in range(HG_HEADS)]

    qd = cat([q[i] * jnp.exp2(b[i]) for i in range(nv)]).astype(BF16)
    kd = cat([k[i] * jnp.exp2(b_last - b[i]) for i in range(nv)]).astype(BF16)
    qk = cat([q[i] * k[i] for i in range(nv)])
    vf = vb.astype(F32)
    decay = jnp.exp2(b_last[0:1, :])
    outs = []
    for hd in range(HG_HEADS):
        st = st_ref[hd]
        v_h = head(vb, hd)
        o = _dot(head(qd, hd), st.astype(BF16), _NT) + _dot(a[hd].astype(BF16), v_h)
        outs.append(o + jnp.sum(head(qk, hd), axis=-1, keepdims=True) * head(vf, hd))
        st_ref[hd] = st * head(decay, hd) + _dot(v_h, head(kd, hd), _TN)
    return outs


def _hgrn_kernel(*refs, ln, n_chunks, has_init, nseq, one_step):
    if has_init:
        qh_ref, kh_ref, vh_ref, gate_ref, g_ref, gw_ref, s0_ref, o_ref, sout_ref, st_ref, b_ref = refs
    else:
        qh_ref, kh_ref, vh_ref, gate_ref, g_ref, gw_ref, o_ref, sout_ref, st_ref, b_ref = refs
        s0_ref = None
    si = pl.program_id(1)
    first = (lambda f: f()) if one_step else pl.when(si == 0)
    final = (lambda f: f()) if one_step else pl.when(si == pl.num_programs(1) - 1)
    gw = gw_ref[...]
    cst = _gla_consts(ln, HG_WIDTH)

    for sq in range(nseq):
        st_sq, b_sq = st_ref.at[sq], b_ref.at[sq]

        @first
        def _():
            for hd in range(HG_HEADS):
                st_sq[hd] = s0_ref[sq, hd].T if has_init else jnp.zeros((HG_DIM, HG_DIM), F32)

        def body(c, carry):
            rows = pl.ds(pl.multiple_of((sq * n_chunks + c) * ln, ln), ln)
            outs = _gla_chunk(qh_ref[rows, :], kh_ref[rows, :], vh_ref[rows, :], g_ref[rows, :],
                              st_sq, b_sq, cst, ln)
            for hd in range(HG_HEADS):
                lanes = slice(hd * HG_DIM, (hd + 1) * HG_DIM)
                o_ref[rows, lanes] = (_rms(outs[hd]) * gw * gate_ref[rows, lanes].astype(F32)).astype(BF16)
            return carry

        lax.fori_loop(0, n_chunks, body, 0, unroll=HGRN_UNROLL if n_chunks % HGRN_UNROLL == 0 else 1)

        @final
        def _():
            for hd in range(HG_HEADS):
                sout_ref[sq, hd] = st_sq[hd].T


def _hgrn(qh, kh, vh, gate, g, gw, state0, batch, seq, ln, ts, nseq=1):
    ns = seq // ts
    assert nseq == 1 or ns == 1
    has_init = state0 is not None
    row = pl.BlockSpec((nseq * ts, HG_WIDTH), lambda b, s: (b * ns + s, 0))
    st_spec = pl.BlockSpec((nseq, HG_HEADS, HG_DIM, HG_DIM), lambda b, s: (b, 0, 0, 0))
    in_specs = [row] * 5 + [_resident((1, HG_DIM))]
    args = [qh, kh, vh, gate, g, gw]
    if has_init:
        in_specs.append(st_spec)
        args.append(state0)
    return pl.pallas_call(
        functools.partial(_hgrn_kernel, ln=ln, n_chunks=ts // ln, has_init=has_init, nseq=nseq,
                          one_step=ns == 1),
        out_shape=[jax.ShapeDtypeStruct((batch * seq, HG_WIDTH), BF16),
                   jax.ShapeDtypeStruct((batch, HG_HEADS, HG_DIM, HG_DIM), F32)],
        grid=(batch // nseq, ns),
        in_specs=in_specs,
        out_specs=[row, st_spec],
        scratch_shapes=[pltpu.VMEM((nseq, HG_HEADS, HG_DIM, HG_DIM), F32),
                        pltpu.VMEM((nseq, ln, HG_WIDTH), F32)],
        compiler_params=_params("arbitrary", "arbitrary"),
        name="hgrn",
    )(*args)


def _out_proj_kernel(x_ref, gt_ref, om_ref, oh_ref, w_ref, o_ref):
    n = om_ref.shape[1]
    mix = _dot(om_ref[...], w_ref[0:n, :]) + _dot(oh_ref[...], w_ref[n:, :])
    o_ref[...] = x_ref[...] + gt_ref[...] * mix


def _out_proj(x, gt, o_mla, o_hg, w_out, tm):
    t, d = x.shape
    n_tiles = t // tm
    tiles_per_group = n_tiles // gt.shape[0]
    r = gt.shape[1]
    row = lambda n: pl.BlockSpec((tm, n), lambda i: (i, 0))
    return pl.pallas_call(
        _out_proj_kernel,
        out_shape=jax.ShapeDtypeStruct((t, d), F32),
        grid=(n_tiles,),
        in_specs=[row(d), pl.BlockSpec((None, r, d), lambda i: (i // tiles_per_group, 0, 0)),
                  row(o_mla.shape[1]), row(o_hg.shape[1]), _resident(w_out.shape)],
        out_specs=row(d),
        compiler_params=_params("arbitrary"),
        name="out_proj",
    )(x, gt, o_mla, o_hg, w_out)


def _ffn_kernel(x_ref, sh_ref, sc_ref, gt_ref, w1_ref, w2_ref, gf_ref, o_ref, h_ref):
    f = pl.program_id(1)
    last = pl.num_programs(1) - 1

    def mlp(h):
        a = jnp.maximum(_dot(h, w1_ref[...]), 0.0)
        return _dot((a * a).astype(BF16), w2_ref[...])

    @pl.when(f == 0)
    def _():
        h = (_rms(x_ref[...]) * (1.0 + sc_ref[...]) + sh_ref[...]).astype(BF16)
        h_ref[...] = h
        o_ref[...] = mlp(h)

    @pl.when(jnp.logical_and(f > 0, f < last))
    def _():
        o_ref[...] += mlp(h_ref[...])

    @pl.when(f == last)
    def _():
        x2 = x_ref[...] + gt_ref[...] * (o_ref[...] + mlp(h_ref[...]))
        o_ref[...] = _rms(x2) * gf_ref[...]


def _ffn(x, sh, sc, gt, w1, w2, gf, tm, tf):
    t, d = x.shape
    dff = w1.shape[1]
    n_tiles = t // tm
    tiles_per_group = n_tiles // sh.shape[0]
    r = sh.shape[1]
    row = pl.BlockSpec((tm, d), lambda i, f: (i, 0))
    mod = pl.BlockSpec((None, r, d), lambda i, f: (i // tiles_per_group, 0, 0))
    return pl.pallas_call(
        _ffn_kernel,
        out_shape=jax.ShapeDtypeStruct((t, d), F32),
        grid=(n_tiles, dff // tf),
        in_specs=[row, mod, mod, mod,
                  pl.BlockSpec((d, tf), lambda i, f: (0, f)),
                  pl.BlockSpec((tf, d), lambda i, f: (f, 0)),
                  pl.BlockSpec((1, d), lambda i, f: (0, 0))],
        out_specs=row,
        scratch_shapes=[pltpu.VMEM((tm, d), BF16)],
        compiler_params=_params("arbitrary", "arbitrary", vmem_limit=FFN_VMEM_LIMIT),
        name="ffn",
    )(x, sh, sc, gt, w1, w2, gf)


def _rot_half_cols(w, axis=-1):
    lo, hi = jnp.split(w, 2, axis=axis)
    return jnp.concatenate([-hi, lo], axis=axis)


def _prep_weights(w_in, w_uq, g_q, g_kv, w_uk, w_uv, lb, g_hgrn, w_out, w_ff1, w_ff2, g_final):
    d = w_in.shape[0]
    o_kpe = Q_RANK + KV_RANK
    o_hg = o_kpe + QK_ROPE
    w_in_t = jnp.swapaxes(w_in, 0, 1)
    w_mla_t = lax.optimization_barrier(w_in_t[:o_hg])
    w_kpe_t = w_mla_t[o_kpe:]
    q_scale = ATTN_SCALE * math.log2(math.e)
    wq = w_uq.reshape(Q_RANK, N_HEADS, QK_NOPE + QK_ROPE) * q_scale
    wq_pe = wq[..., QK_NOPE:]
    wuq = jnp.concatenate([wq[..., :QK_NOPE], wq_pe, _rot_half_cols(wq_pe)], axis=-1)
    return {
        "wa": jnp.concatenate([w_mla_t, _rot_half_cols(w_kpe_t, axis=0)], axis=0).astype(BF16),
        "wh_src": (w_in_t, o_hg, w_in_t.shape[0] - o_hg),
        "wuq": wuq.reshape(Q_RANK, N_HEADS * HEAD_PACK).astype(BF16),
        "wukv": jnp.concatenate([w_uk.reshape(KV_RANK, -1), w_uv.reshape(KV_RANK, -1)], axis=-1).astype(BF16),
        "wuk_t": jnp.transpose(w_uk, (1, 0, 2)).astype(BF16),
        "wuv_t": jnp.transpose(w_uv, (1, 0, 2)).astype(BF16),
        "gq": g_q.reshape(1, -1), "gkv": g_kv.reshape(1, -1),
        "lb": lb.reshape(1, -1), "ghg": g_hgrn.reshape(1, -1),
        "wout_f32": w_out, "w1_f32": w_ff1, "w2_f32": w_ff2,
        "gfin": g_final.reshape(1, d),
    }


def _layer(x, mods, w, batch, seq, cache, past):
    sh1, sc1, gt1, sh2, sc2, gt2 = mods
    t = x.shape[0]
    tm = min(PROJ_ROWS, t)
    prompt = cache is None
    if prompt:
        q, ckv, kpe, kf, v, w["wh"] = _mla_in(x, sh1, sc1, w, tm, True, past, seq, w["wh_src"])
        o_mla = _attention(q, kf, v, batch, seq, ATTN_KV_ROWS, ATTN_HEADS)
    else:
        cache_ckv, cache_kpe, state0 = cache
        q, ckv, kpe = _mla_in(x, sh1, sc1, w, tm, False, past, seq)
        ql, qpe = _q_latent(q, w["wuk_t"], batch, seq)
        o_mla = _dec_attention(ql, qpe, cache_ckv, cache_kpe, ckv, kpe, w["wuv_t"], seq)
    if prompt:
        w["w1"], w["w2"], w["wout"], qh, kh, vh, gate, g = _hg_in(
            x, sh1, sc1, w, tm, (w["w1_f32"], w["w2_f32"], w["wout_f32"]))
        o_hg, s_fin = _hgrn(qh, kh, vh, gate, g, w["ghg"], None, batch, seq, CHUNK, HGRN_ROWS)
    else:
        qh, kh, vh, gate, g = _hg_in(x, sh1, sc1, w, tm)
        o_hg, s_fin = _hgrn(qh, kh, vh, gate, g, w["ghg"], state0, batch, seq, seq, seq,
                            DEC_SEQS if batch % DEC_SEQS == 0 else 1)
    x1 = _out_proj(x, gt1, o_mla, o_hg, w["wout"], tm)
    y = _ffn(x1, sh2, sc2, gt2, w["w1"], w["w2"], w["gfin"], min(FFN_ROWS, t), FFN_COLS)
    return y, ckv, kpe, s_fin


def kernel(x_prompt, x_sample, c_prompt, c_sample, cache_ckv, cache_kpe, state_hgrn, w_ada, b_ada, w_in, w_uq, g_q, g_kv, w_uk, w_uv, hg_lower_bounds, g_hgrn, w_out, w_ff1, w_ff2, g_final):
    depth = w_ada.shape[0]
    assert depth == 1, "single-layer step"
    bp, sp, d = x_prompt.shape
    bs, ss, _ = x_sample.shape
    past = cache_ckv.shape[2]

    lb_all = jnp.cumsum(jax.nn.softmax(hg_lower_bounds.astype(F32), axis=0), axis=0)
    w = _prep_weights(w_in[0], w_uq[0], g_q[0], g_kv[0], w_uk[0], w_uv[0], lb_all[0], g_hgrn[0],
                      w_out[0], w_ff1[0], w_ff2[0], g_final)

    c_all = jnp.concatenate([c_prompt, c_sample], axis=0)
    n_c = c_all.shape[0]
    c_pad = jnp.pad(c_all, ((0, -n_c % 8), (0, 0)))
    mod = _modulation(c_pad, w_ada[0], b_ada[0])[:n_c]
    mod_p = [m.reshape(bp, 1, d) for m in jnp.split(mod[:bp], 6, axis=-1)]
    mod_s = [jnp.repeat(m, ss, axis=0).reshape(1, bs * ss, d) for m in jnp.split(mod[bp:], 6, axis=-1)]

    yp, ckv_p, kpe_p, st_p = _layer(x_prompt.reshape(bp * sp, d), mod_p, w, bp, sp, None, 0)
    ys, ckv_s, kpe_s, st_s = _layer(x_sample.reshape(bs * ss, d), mod_s, w, bs, ss,
                                    (cache_ckv[0], jnp.swapaxes(cache_kpe[0], 1, 2), state_hgrn[0]), past)
    return (yp.reshape(bp, sp, d), ys.reshape(bs, ss, d),
            ckv_p.reshape(1, bp, sp, KV_RANK), kpe_p.reshape(1, bp, sp, QK_ROPE), st_p[None],
            ckv_s.reshape(1, bs, ss, KV_RANK), kpe_s.reshape(1, bs, ss, QK_ROPE), st_s[None])
```

```python
import functools
import math

import jax
import jax.numpy as jnp
from jax import lax
from jax.experimental import pallas as pl
from jax.experimental.pallas import tpu as pltpu

F32 = jnp.float32
BF16 = jnp.bfloat16

EPS = 1e-6
CHUNK = 64
N_HEADS = 8
QK_NOPE = 128
QK_ROPE = 64
V_HEAD = 128
Q_RANK = 512
KV_RANK = 512
ROPE_BASE = 10000.0
ATTN_SCALE = (QK_NOPE + QK_ROPE) ** -0.5
HG_HEADS = 8
HG_DIM = 128
HG_WIDTH = HG_HEADS * HG_DIM
HEAD_PACK = 2 * QK_NOPE
LANES = 128
MASKED = -1e30
VMEM_LIMIT = 56 * 1024 * 1024
FFN_VMEM_LIMIT = 62 * 1024 * 1024

PROJ_ROWS = 512
FFN_ROWS, FFN_COLS = 1024, 1024
MOD_COLS = 1024
ATTN_KV_ROWS = 512
ATTN_HEADS = 2
HGRN_ROWS = 1024
HGRN_UNROLL = 8
DEC_SEQS = 4

_NT = (((1,), (1,)), ((), ()))
_TN = (((0,), (0,)), ((), ()))


def _rms(x):
    return x * lax.rsqrt(jnp.mean(x * x, axis=-1, keepdims=True) + EPS)


def _dot(a, b, dims=None):
    if dims is None:
        return jnp.dot(a, b, preferred_element_type=F32)
    return lax.dot_general(a, b, dims, preferred_element_type=F32)


def _resident(shape):
    return pl.BlockSpec(shape, lambda *_: (0,) * len(shape), pipeline_mode=pl.Buffered(1))


def _params(*sem, vmem_limit=VMEM_LIMIT):
    return pltpu.CompilerParams(dimension_semantics=sem, vmem_limit_bytes=vmem_limit)


def _mod_kernel(c_ref, w_ref, b_ref, o_ref):
    c = c_ref[...]
    a = (c * jax.nn.sigmoid(c)).astype(BF16)
    o_ref[...] = _dot(a, w_ref[...].astype(BF16)) + b_ref[...]


def _modulation(c_all, w_ada, b_ada):
    m, d = c_all.shape
    n = w_ada.shape[1]
    tn = MOD_COLS
    return pl.pallas_call(
        _mod_kernel,
        out_shape=jax.ShapeDtypeStruct((m, n), F32),
        grid=(n // tn,),
        in_specs=[pl.BlockSpec((m, d), lambda j: (0, 0)),
                  pl.BlockSpec((d, tn), lambda j: (0, j)),
                  pl.BlockSpec((1, tn), lambda j: (0, j))],
        out_specs=pl.BlockSpec((m, tn), lambda j: (0, j)),
        compiler_params=_params("arbitrary"),
        name="mod",
    )(c_all, w_ada, b_ada.reshape(1, n))


def _mla_in_kernel(x_ref, sh_ref, sc_ref, cr_ref, sr_ref, wa_ref, wuq_ref, wukv_ref, gq_ref, gkv_ref,
                   *refs, pos_base, pos_period, n_cast, with_kv):
    cast_refs, (q_ref, ckv_ref, kpe_ref), refs = refs[:n_cast], refs[n_cast:n_cast + 3], refs[n_cast + 3:]
    if n_cast:
        rows = cast_refs[0].shape[0]
        for j, src in enumerate(cast_refs):
            refs[-1][j * rows:(j + 1) * rows, :] = src[...].astype(BF16)
    x = x_ref[...]
    tm = x.shape[0]
    h = (_rms(x) * (1.0 + sc_ref[...]) + sh_ref[...]).astype(BF16)

    half = QK_ROPE // 2
    lane = lax.broadcasted_iota(jnp.int32, (8, LANES), 1)
    base = (pos_base + lax.rem(pl.program_id(0) * tm, pos_period)).astype(F32)
    ang_b = base * jnp.exp(-math.log(ROPE_BASE) * lax.rem(lane, half).astype(F32) / half)
    cb, sb = jnp.cos(ang_b)[0:1], jnp.sin(ang_b)[0:1]
    cr, sr = cr_ref[...], sr_ref[...]
    cos = cr * cb - sr * sb
    sin = sr * cb + cr * sb

    def rope(t):
        return t * cos + pltpu.roll(t, QK_ROPE, 1) * sin

    a = _dot(h, wa_ref[...], _NT)
    cqn = (_rms(a[:, :Q_RANK]) * gq_ref[...]).astype(BF16)
    ckvn = _rms(a[:, Q_RANK:Q_RANK + KV_RANK]) * gkv_ref[...]
    ckv_ref[...] = ckvn

    q = _dot(cqn, wuq_ref[...])
    for hd in range(N_HEADS):
        lo = hd * HEAD_PACK
        q_ref[:, lo:lo + QK_NOPE] = q[:, lo:lo + QK_NOPE].astype(BF16)
        q_ref[:, lo + QK_NOPE:lo + HEAD_PACK] = rope(q[:, lo + QK_NOPE:lo + HEAD_PACK]).astype(BF16)

    kp = rope(a[:, Q_RANK + KV_RANK:])
    kpe_ref[...] = kp[:, :QK_ROPE]

    if with_kv:
        kf_ref, v_ref = refs[:2]
        kv = _dot(ckvn.astype(BF16), wukv_ref[...])
        kpb = kp.astype(BF16)
        for hd in range(N_HEADS):
            lo = hd * HEAD_PACK
            kf_ref[:, lo:lo + QK_NOPE] = kv[:, hd * QK_NOPE:(hd + 1) * QK_NOPE].astype(BF16)
            kf_ref[:, lo + QK_NOPE:lo + HEAD_PACK] = kpb
            vo = N_HEADS * QK_NOPE + hd * V_HEAD
            v_ref[:, lo:lo + V_HEAD] = kv[:, vo:vo + V_HEAD].astype(BF16)
            v_ref[:, lo + V_HEAD:lo + HEAD_PACK] = jnp.ones((kv.shape[0], HEAD_PACK - V_HEAD), BF16)


def _mla_in(x, sh, sc, w, tm, with_kv, pos_base, pos_period, cast_src=None):
    t, d = x.shape
    n_tiles = t // tm
    tiles_per_group = n_tiles // sh.shape[0]
    r = sh.shape[1]
    row = lambda n: pl.BlockSpec((tm, n), lambda i: (i, 0))
    mod = pl.BlockSpec((None, r, d), lambda i: (i // tiles_per_group, 0, 0))
    packed = N_HEADS * HEAD_PACK
    half = QK_ROPE // 2
    lane = jnp.arange(LANES)
    inv_freq = jnp.exp(-math.log(ROPE_BASE) * (lane % half).astype(F32) / half)
    ang = (jnp.arange(tm) % min(tm, pos_period)).astype(F32)[:, None] * inv_freq[None, :]
    cr = jnp.where(lane < QK_ROPE, jnp.cos(ang), 0.0)
    sr = jnp.where(lane < QK_ROPE, jnp.sin(ang), 0.0)
    out_shape = [jax.ShapeDtypeStruct((t, packed), BF16),
                 jax.ShapeDtypeStruct((t, KV_RANK), F32),
                 jax.ShapeDtypeStruct((t, QK_ROPE), F32)]
    out_specs = [row(packed), row(KV_RANK), row(QK_ROPE)]
    if with_kv:
        out_shape += [jax.ShapeDtypeStruct((t, packed), BF16),
                      jax.ShapeDtypeStruct((t, packed), BF16)]
        out_specs += [row(packed), row(packed)]
    cast_args, cast_specs = [], []
    if cast_src is not None:
        src, first, n_rows = cast_src
        blk = n_rows // n_tiles
        sub = math.gcd(first, blk)
        for j in range(blk // sub):
            cast_args.append(src)
            cast_specs.append(pl.BlockSpec((sub, d), lambda i, j=j: (first // sub + i * (blk // sub) + j, 0)))
        out_shape.append(jax.ShapeDtypeStruct((n_rows, d), BF16))
        out_specs.append(pl.BlockSpec((blk, d), lambda i: (i, 0)))
    return pl.pallas_call(
        functools.partial(_mla_in_kernel, pos_base=pos_base, pos_period=pos_period,
                          n_cast=len(cast_args), with_kv=with_kv),
        out_shape=out_shape,
        grid=(n_tiles,),
        in_specs=[row(d), mod, mod, _resident((tm, LANES)), _resident((tm, LANES)),
                  _resident(w["wa"].shape), _resident(w["wuq"].shape), _resident(w["wukv"].shape),
                  _resident((1, Q_RANK)), _resident((1, KV_RANK))] + cast_specs,
        out_specs=out_specs,
        compiler_params=_params("arbitrary"),
        name="mla_in",
    )(x, sh, sc, cr, sr, w["wa"], w["wuq"], w["wukv"], w["gq"], w["gkv"], *cast_args)


def _hg_in_kernel(x_ref, sh_ref, sc_ref, wh_ref, lb_ref, *refs):
    qh_ref, kh_ref, vh_ref, gate_ref, g_ref = refs[-5:]
    n_cast = (len(refs) - 5) // 2
    for src, dst in zip(refs[:n_cast], refs[n_cast:2 * n_cast]):
        dst[...] = src[...].astype(BF16)
    x = x_ref[...]
    h = (_rms(x) * (1.0 + sc_ref[...]) + sh_ref[...]).astype(BF16)
    n = HG_WIDTH
    hf = _dot(h, wh_ref[n:2 * n, :], _NT)
    lb = lb_ref[...]
    f = lb + (1.0 - lb) * jax.nn.sigmoid(hf)
    kh_ref[...] = (1.0 - f).astype(BF16)
    g_ref[...] = jnp.log2(f)
    hq = _dot(h, wh_ref[0:n, :], _NT)
    qh_ref[...] = (hq * jax.nn.sigmoid(hq) * (HG_DIM ** -0.5)).astype(BF16)
    hg = _dot(h, wh_ref[3 * n:4 * n, :], _NT)
    gate_ref[...] = (hg * jax.nn.sigmoid(hg)).astype(BF16)
    vh_ref[...] = _dot(h, wh_ref[2 * n:3 * n, :], _NT).astype(BF16)


def _hg_in(x, sh, sc, w, tm, cast=()):
    t, d = x.shape
    n_tiles = t // tm
    tiles_per_group = n_tiles // sh.shape[0]
    r = sh.shape[1]
    row = lambda n: pl.BlockSpec((tm, n), lambda i: (i, 0))
    mod = pl.BlockSpec((None, r, d), lambda i: (i // tiles_per_group, 0, 0))
    half = jax.ShapeDtypeStruct((t, HG_WIDTH), BF16)
    cast_specs = [pl.BlockSpec((m.shape[0] // n_tiles, m.shape[1]), lambda i: (i, 0)) for m in cast]
    return pl.pallas_call(
        _hg_in_kernel,
        out_shape=[jax.ShapeDtypeStruct(m.shape, BF16) for m in cast]
        + [half, half, half, half, jax.ShapeDtypeStruct((t, HG_WIDTH), F32)],
        grid=(n_tiles,),
        in_specs=[row(d), mod, mod, _resident(w["wh"].shape), _resident((1, HG_WIDTH))] + cast_specs,
        out_specs=cast_specs + [row(HG_WIDTH)] * 5,
        compiler_params=_params("arbitrary"),
        name="hg_in",
    )(x, sh, sc, w["wh"], w["lb"], *cast)


def _attn_kernel(q_ref, qn_ref, k_ref, v_ref, o_ref, s0_ref, s1_ref, mx0_ref, mx1_ref, m_ref, acc_ref,
                 *, tk, nh):
    qi = pl.program_id(2)
    diag_ok = (lax.broadcasted_iota(jnp.int32, (tk, tk), 1) // CHUNK
               <= lax.broadcasted_iota(jnp.int32, (tk, tk), 0) // CHUNK)
    streams = [(hd, r) for hd in range(nh) for r in range(2)]
    sid = lambda hd, r: 2 * hd + r

    def kv_rows(blk):
        return pl.ds(pl.multiple_of(blk * tk, tk), tk)

    def rows(r):
        return slice(r * tk, (r + 1) * tk)

    def head(hd):
        return slice(hd * HEAD_PACK, (hd + 1) * HEAD_PACK)

    def lanes(x, n):
        return jnp.concatenate([x] * (n // LANES), axis=1)

    def row_max(s):
        return jnp.broadcast_to(jnp.max(s, axis=-1, keepdims=True), (tk, LANES))

    def scores(q, hd, r, blk):
        return _dot(q[rows(r), head(hd)], k_ref[kv_rows(blk), head(hd)], _NT)

    def produce(q, hd, r, blk, s_ref, mx_ref):
        s = scores(q, hd, r, blk)
        s_ref[sid(hd, r)] = s
        mx_ref[sid(hd, r)] = row_max(s)

    def block_softmax(s, mx, hd, blk):
        p = jnp.exp2(s - lanes(mx, tk)).astype(BF16)
        return mx, _dot(p, v_ref[kv_rows(blk), head(hd)])

    def merge(a, b):
        (ma, acc_a), (mb, acc_b) = a, b
        m = jnp.maximum(ma, mb)
        return m, (lanes(jnp.exp2(ma - m), HEAD_PACK) * acc_a
                   + lanes(jnp.exp2(mb - m), HEAD_PACK) * acc_b)

    def update(hd, r, s, mx, blk):
        i = sid(hd, r)
        m_old = m_ref[i]
        m_new = jnp.maximum(m_old, mx)
        m_ref[i] = m_new
        p = jnp.exp2(s - lanes(m_new, tk)).astype(BF16)
        acc_ref[i] = (lanes(jnp.exp2(m_old - m_new), HEAD_PACK) * acc_ref[i]
                      + _dot(p, v_ref[kv_rows(blk), head(hd)]))

    @pl.when(qi == 0)
    def _():
        for hd, r in streams:
            produce(q_ref, hd, r, 0, s0_ref, mx0_ref)

    for i in range(2 * nh):
        m_ref[i] = jnp.full((tk, LANES), MASKED, F32)
        acc_ref[i] = jnp.zeros((tk, HEAD_PACK), F32)

    def kv_pair(blk):
        for (cur, cmx), (nxt, nmx) in (((s0_ref, mx0_ref), (s1_ref, mx1_ref)),
                                       ((s1_ref, mx1_ref), (s0_ref, mx0_ref))):
            for hd, r in streams:
                produce(q_ref, hd, r, blk + 1, nxt, nmx)
            for hd, r in streams:
                update(hd, r, cur[sid(hd, r)], cmx[sid(hd, r)], blk)
            blk = blk + 1

    def body(j, carry):
        kv_pair(4 * j)
        kv_pair(4 * j + 2)
        return carry

    lax.fori_loop(0, qi // 2, body, 0)

    @pl.when(qi % 2 == 1)
    def _():
        kv_pair(2 * (qi - 1))

    for hd in range(nh):
        top, bot = sid(hd, 0), sid(hd, 1)
        d0 = jnp.where(diag_ok, s0_ref[top], MASKED)
        d1 = jnp.where(diag_ok, scores(q_ref, hd, 1, 2 * qi + 1), MASKED)
        part_a = block_softmax(d0, row_max(d0), hd, 2 * qi)
        part_b = block_softmax(s0_ref[bot], mx0_ref[bot], hd, 2 * qi)
        part_c = block_softmax(d1, row_max(d1), hd, 2 * qi + 1)
        finals = [merge((m_ref[top], acc_ref[top]), part_a),
                  merge(merge((m_ref[bot], acc_ref[bot]), part_b), part_c)]
        for r in range(2):
            acc = finals[r][1]
            o_ref[rows(r), hd * V_HEAD:(hd + 1) * V_HEAD] = (acc[:, :V_HEAD] / acc[:, V_HEAD:]).astype(BF16)
    for hd, r in streams:
        produce(qn_ref, hd, r, 0, s0_ref, mx0_ref)


def _attention(q, kf, vp, batch, seq, tk, nh):
    tq = 2 * tk
    nq = seq // tq
    return pl.pallas_call(
        functools.partial(_attn_kernel, tk=tk, nh=nh),
        out_shape=jax.ShapeDtypeStruct((batch * seq, N_HEADS * V_HEAD), BF16),
        grid=(batch, N_HEADS // nh, nq),
        in_specs=[pl.BlockSpec((tq, nh * HEAD_PACK), lambda b, h, i: (b * nq + i, h)),
                  pl.BlockSpec((tq, nh * HEAD_PACK), lambda b, h, i: (b * nq + jnp.minimum(i + 1, nq - 1), h)),
                  pl.BlockSpec((seq, nh * HEAD_PACK), lambda b, h, i: (b, h)),
                  pl.BlockSpec((seq, nh * HEAD_PACK), lambda b, h, i: (b, h))],
        out_specs=pl.BlockSpec((tq, nh * V_HEAD), lambda b, h, i: (b * nq + i, h)),
        scratch_shapes=[pltpu.VMEM((2 * nh, tk, tk), F32), pltpu.VMEM((2 * nh, tk, tk), F32),
                        pltpu.VMEM((2 * nh, tk, LANES), F32), pltpu.VMEM((2 * nh, tk, LANES), F32),
                        pltpu.VMEM((2 * nh, tk, LANES), F32), pltpu.VMEM((2 * nh, tk, HEAD_PACK), F32)],
        compiler_params=_params("arbitrary", "arbitrary", "arbitrary"),
        name="attn",
    )(q, q, kf, vp)


def _qlat_kernel(qn_ref, qp_ref, wuk_ref, ql_ref, qpe_ref):
    nb = ql_ref.shape[0]
    ql = _dot(qn_ref[...], wuk_ref[...], _NT)
    ql_ref[...] = ql.reshape(nb, -1, KV_RANK)
    qpe_ref[...] = qp_ref[...].astype(F32).reshape(nb, -1, LANES)


def _q_latent(q, wuk_t, nb, ln):
    t = q.shape[0]
    return pl.pallas_call(
        _qlat_kernel,
        out_shape=[jax.ShapeDtypeStruct((nb, N_HEADS, ln, KV_RANK), F32),
                   jax.ShapeDtypeStruct((nb, N_HEADS, ln, LANES), F32)],
        grid=(N_HEADS,),
        in_specs=[pl.BlockSpec((t, QK_NOPE), lambda h: (0, 2 * h)),
                  pl.BlockSpec((t, QK_NOPE), lambda h: (0, 2 * h + 1)),
                  pl.BlockSpec((None, KV_RANK, QK_NOPE), lambda h: (h, 0, 0))],
        out_specs=[pl.BlockSpec((nb, None, ln, KV_RANK), lambda h: (0, h, 0, 0)),
                   pl.BlockSpec((nb, None, ln, LANES), lambda h: (0, h, 0, 0))],
        compiler_params=_params("arbitrary"),
        name="q_latent",
    )(q, q, wuk_t)


def _dec_attn_kernel(ql_ref, qpe_ref, cckv_ref, ckpe_ref, nckv_ref, nkpe_ref, wuv_ref, o_ref, *, ln):
    rows = N_HEADS * ln
    nb = ql_ref.shape[0]
    ctxs = []
    for b in range(nb):
        new = slice(b * ln, (b + 1) * ln)
        ql = ql_ref[b].reshape(rows, KV_RANK).astype(BF16)
        qp = qpe_ref[b].reshape(rows, LANES)[:, :QK_ROPE].astype(BF16)
        ckc = cckv_ref[b].astype(BF16)
        kpc = ckpe_ref[b].astype(BF16)
        ckn = nckv_ref[new, :].astype(BF16)
        kpn = nkpe_ref[new, :].astype(BF16)
        s_c = _dot(ql, ckc, _NT) + _dot(qp, kpc)
        s_n = _dot(ql, ckn, _NT) + _dot(qp, kpn, _NT)
        m = jnp.maximum(jnp.max(s_c, axis=-1, keepdims=True), jnp.max(s_n, axis=-1, keepdims=True))
        p_c = jnp.exp2(s_c - m)
        p_n = jnp.exp2(s_n - m)
        l = jnp.sum(p_c, axis=-1, keepdims=True) + jnp.sum(p_n, axis=-1, keepdims=True)
        ctxs.append((_dot(p_c.astype(BF16), ckc) + _dot(p_n.astype(BF16), ckn)) / l)
    for hd in range(N_HEADS):
        ctx_h = jnp.concatenate([c[hd * ln:(hd + 1) * ln] for c in ctxs], axis=0).astype(BF16)
        o_ref[:, hd * V_HEAD:(hd + 1) * V_HEAD] = _dot(ctx_h, wuv_ref[hd]).astype(BF16)


def _dec_attention(ql, qpe, cache_ckv, cache_kpe_t, ckv_new, kpe_new, wuv_t, ln):
    nb, past = cache_ckv.shape[0], cache_ckv.shape[1]
    bs = DEC_SEQS if nb % DEC_SEQS == 0 else 1
    return pl.pallas_call(
        functools.partial(_dec_attn_kernel, ln=ln),
        out_shape=jax.ShapeDtypeStruct((nb * ln, N_HEADS * V_HEAD), BF16),
        grid=(nb // bs,),
        in_specs=[pl.BlockSpec((bs, N_HEADS, ln, KV_RANK), lambda b: (b, 0, 0, 0)),
                  pl.BlockSpec((bs, N_HEADS, ln, LANES), lambda b: (b, 0, 0, 0)),
                  pl.BlockSpec((bs, past, KV_RANK), lambda b: (b, 0, 0)),
                  pl.BlockSpec((bs, QK_ROPE, past), lambda b: (b, 0, 0)),
                  pl.BlockSpec((bs * ln, KV_RANK), lambda b: (b, 0)),
                  pl.BlockSpec((bs * ln, QK_ROPE), lambda b: (b, 0)),
                  _resident(wuv_t.shape)],
        out_specs=pl.BlockSpec((bs * ln, N_HEADS * V_HEAD), lambda b: (b, 0)),
        compiler_params=_params("arbitrary"),
        name="dec_attn",
    )(ql, qpe, cache_ckv, cache_kpe_t, ckv_new, kpe_new, wuv_t)


def _gla_consts(ln, width):
    row = lax.broadcasted_iota(jnp.int32, (8, width), 0)
    t_idx = lax.broadcasted_iota(jnp.int32, (ln, ln), 0)
    s_idx = lax.broadcasted_iota(jnp.int32, (ln, ln), 1)
    level_mask = []
    for lev in range(int(math.log2(ln))):
        level_mask.append((((t_idx ^ s_idx) >> (lev + 1)) == 0)
                          & ((t_idx & (1 << lev)) != 0) & ((s_idx & (1 << lev)) == 0))
    return {
        "scan": [row >= sft for sft in (1, 2, 4)],
        "odd": (row & 1).astype(F32),
        "low4": (row & 4) == 0,
        "sign1": jnp.where((row & 2) != 0, 1.0, -1.0).astype(F32),
        "sign2": jnp.where((row & 4) != 0, 1.0, -1.0).astype(F32),
        "level_mask": level_mask,
    }


def _gla_chunk(qb, kb, vb, g, st_ref, loc_ref, cst, ln):
    nv = ln // 8
    width = g.shape[1]
    grp = lambda x: [x[8 * i:8 * i + 8, :] for i in range(nv)]
    cat = lambda xs: jnp.concatenate(xs, axis=0)
    head = lambda x, hd: x[:, hd * HG_DIM:(hd + 1) * HG_DIM]
    q, k = grp(qb.astype(F32)), grp(kb.astype(F32))
    gs = grp(g)

    loc = gs
    for sft, keep in zip((1, 2, 4), cst["scan"]):
        loc = [x + jnp.where(keep, pltpu.roll(x, sft, 0), 0.0) for x in loc]
    loc_ref[...] = cat(loc)
    bcast = lambda r: jnp.broadcast_to(loc_ref[r:r + 1, :], (8, width))
    tot = [bcast(8 * i + 7) for i in range(nv)]
    pin = [tot[0]]
    for i in range(1, nv):
        pin.append(pin[-1] + tot[i])
    b = [loc[0]] + [loc[i] + pin[i - 1] for i in range(1, nv)]
    b_last = pin[-1]

    exps = [[gs[i] * cst["odd"] for i in range(nv)],
            [(loc[i] - jnp.where(cst["low4"], bcast(8 * i + 1), bcast(8 * i + 5))) * cst["sign1"]
             for i in range(nv)],
            [(loc[i] - bcast(8 * i + 3)) * cst["sign2"] for i in range(nv)]]
    upper = [[True] * nv, [True] * nv, [True] * nv]
    lower = [[True] * nv, [True] * nv, [True] * nv]
    for lev in range(3, int(math.log2(ln))):
        m = 1 << (lev - 3)
        e, up, lo = [], [], []
        for i in range(nv):
            mid = (i // (2 * m)) * 2 * m + m - 1
            is_up = (i // m) % 2 == 1
            if is_up:
                e.append(loc[i] if mid == i - 1 else b[i] - pin[mid])
            else:
                e.append(tot[i] - loc[i] if mid == i else pin[mid] - b[i])
            up.append(is_up)
            lo.append(not is_up)
        exps.append(e)
        upper.append(up)
        lower.append(lo)

    a = [jnp.zeros((ln, ln), F32)] * HG_HEADS
    for lev, e in enumerate(exps):
        w = [jnp.exp2(x) for x in e]
        qt = cat([q[i] * w[i] if upper[lev][i] else q[i] for i in range(nv)]).astype(BF16)
        kt = cat([k[i] * w[i] if lower[lev][i] else k[i] for i in range(nv)]).astype(BF16)
        a = [jnp.where(cst["level_mask"][lev], _dot(head(qt, hd), head(kt, hd), _NT), a[hd])
             for hd in range(HG_HEADS)]

    qd = cat([q[i] * jnp.exp2(b[i]) for i in range(nv)]).astype(BF16)
    kd = cat([k[i] * jnp.exp2(b_last - b[i]) for i in range(nv)]).astype(BF16)
    qk = cat([q[i] * k[i] for i in range(nv)])
    vf = vb.astype(F32)
    decay = jnp.exp2(b_last[0:1, :])
    outs = []
    for hd in range(HG_HEADS):
        st = st_ref[hd]
        v_h = head(vb, hd)
        o = _dot(head(qd, hd), st.astype(BF16), _NT) + _dot(a[hd].astype(BF16), v_h)
        outs.append(o + jnp.sum(head(qk, hd), axis=-1, keepdims=True) * head(vf, hd))
        st_ref[hd] = st * head(decay, hd) + _dot(v_h, head(kd, hd), _TN)
    return outs


def _hgrn_kernel(*refs, ln, n_chunks, has_init, nseq, one_step):
    if has_init:
        qh_ref, kh_ref, vh_ref, gate_ref, g_ref, gw_ref, s0_ref, o_ref, sout_ref, st_ref, b_ref = refs
    else:
        qh_ref, kh_ref, vh_ref, gate_ref, g_ref, gw_ref, o_ref, sout_ref, st_ref, b_ref = refs
        s0_ref = None
    si = pl.program_id(1)
    first = (lambda f: f()) if one_step else pl.when(si == 0)
    final = (lambda f: f()) if one_step else pl.when(si == pl.num_programs(1) - 1)
    gw = gw_ref[...]
    cst = _gla_consts(ln, HG_WIDTH)

    for sq in range(nseq):
        st_sq, b_sq = st_ref.at[sq], b_ref.at[sq]

        @first
        def _():
            for hd in range(HG_HEADS):
                st_sq[hd] = s0_ref[sq, hd].T if has_init else jnp.zeros((HG_DIM, HG_DIM), F32)

        def body(c, carry):
            rows = pl.ds(pl.multiple_of((sq * n_chunks + c) * ln, ln), ln)
            outs = _gla_chunk(qh_ref[rows, :], kh_ref[rows, :], vh_ref[rows, :], g_ref[rows, :],
                              st_sq, b_sq, cst, ln)
            for hd in range(HG_HEADS):
                lanes = slice(hd * HG_DIM, (hd + 1) * HG_DIM)
                o_ref[rows, lanes] = (_rms(outs[hd]) * gw * gate_ref[rows, lanes].astype(F32)).astype(BF16)
            return carry

        lax.fori_loop(0, n_chunks, body, 0, unroll=HGRN_UNROLL if n_chunks % HGRN_UNROLL == 0 else 1)

        @final
        def _():
            for hd in range(HG_HEADS):
                sout_ref[sq, hd] = st_sq[hd].T


def _hgrn(qh, kh, vh, gate, g, gw, state0, batch, seq, ln, ts, nseq=1):
    ns = seq // ts
    assert nseq == 1 or ns == 1
    has_init = state0 is not None
    row = pl.BlockSpec((nseq * ts, HG_WIDTH), lambda b, s: (b * ns + s, 0))
    st_spec = pl.BlockSpec((nseq, HG_HEADS, HG_DIM, HG_DIM), lambda b, s: (b, 0, 0, 0))
    in_specs = [row] * 5 + [_resident((1, HG_DIM))]
    args = [qh, kh, vh, gate, g, gw]
    if has_init:
        in_specs.append(st_spec)
        args.append(state0)
    return pl.pallas_call(
        functools.partial(_hgrn_kernel, ln=ln, n_chunks=ts // ln, has_init=has_init, nseq=nseq,
                          one_step=ns == 1),
        out_shape=[jax.ShapeDtypeStruct((batch * seq, HG_WIDTH), BF16),
                   jax.ShapeDtypeStruct((batch, HG_HEADS, HG_DIM, HG_DIM), F32)],
        grid=(batch // nseq, ns),
        in_specs=in_specs,
        out_specs=[row, st_spec],
        scratch_shapes=[pltpu.VMEM((nseq, HG_HEADS, HG_DIM, HG_DIM), F32),
                        pltpu.VMEM((nseq, ln, HG_WIDTH), F32)],
        compiler_params=_params("arbitrary", "arbitrary"),
        name="hgrn",
    )(*args)


def _out_proj_kernel(x_ref, gt_ref, om_ref, oh_ref, w_ref, o_ref):
    n = om_ref.shape[1]
    mix = _dot(om_ref[...], w_ref[0:n, :]) + _dot(oh_ref[...], w_ref[n:, :])
    o_ref[...] = x_ref[...] + gt_ref[...] * mix


def _out_proj(x, gt, o_mla, o_hg, w_out, tm):
    t, d = x.shape
    n_tiles = t // tm
    tiles_per_group = n_tiles // gt.shape[0]
    r = gt.shape[1]
    row = lambda n: pl.BlockSpec((tm, n), lambda i: (i, 0))
    return pl.pallas_call(
        _out_proj_kernel,
        out_shape=jax.ShapeDtypeStruct((t, d), F32),
        grid=(n_tiles,),
        in_specs=[row(d), pl.BlockSpec((None, r, d), lambda i: (i // tiles_per_group, 0, 0)),
                  row(o_mla.shape[1]), row(o_hg.shape[1]), _resident(w_out.shape)],
        out_specs=row(d),
        compiler_params=_params("arbitrary"),
        name="out_proj",
    )(x, gt, o_mla, o_hg, w_out)


def _ffn_kernel(x_ref, sh_ref, sc_ref, gt_ref, w1_ref, w2_ref, gf_ref, o_ref, h_ref):
    f = pl.program_id(1)
    last = pl.num_programs(1) - 1

    def mlp(h):
        a = jnp.maximum(_dot(h, w1_ref[...]), 0.0)
        return _dot((a * a).astype(BF16), w2_ref[...])

    @pl.when(f == 0)
    def _():
        h = (_rms(x_ref[...]) * (1.0 + sc_ref[...]) + sh_ref[...]).astype(BF16)
        h_ref[...] = h
        o_ref[...] = mlp(h)

    @pl.when(jnp.logical_and(f > 0, f < last))
    def _():
        o_ref[...] += mlp(h_ref[...])

    @pl.when(f == last)
    def _():
        x2 = x_ref[...] + gt_ref[...] * (o_ref[...] + mlp(h_ref[...]))
        o_ref[...] = _rms(x2) * gf_ref[...]


def _ffn(x, sh, sc, gt, w1, w2, gf, tm, tf):
    t, d = x.shape
    dff = w1.shape[1]
    n_tiles = t // tm
    tiles_per_group = n_tiles // sh.shape[0]
    r = sh.shape[1]
    row = pl.BlockSpec((tm, d), lambda i, f: (i, 0))
    mod = pl.BlockSpec((None, r, d), lambda i, f: (i // tiles_per_group, 0, 0))
    return pl.pallas_call(
        _ffn_kernel,
        out_shape=jax.ShapeDtypeStruct((t, d), F32),
        grid=(n_tiles, dff // tf),
        in_specs=[row, mod, mod, mod,
                  pl.BlockSpec((d, tf), lambda i, f: (0, f)),
                  pl.BlockSpec((tf, d), lambda i, f: (f, 0)),
                  pl.BlockSpec((1, d), lambda i, f: (0, 0))],
        out_specs=row,
        scratch_shapes=[pltpu.VMEM((tm, d), BF16)],
        compiler_params=_params("arbitrary", "arbitrary", vmem_limit=FFN_VMEM_LIMIT),
        name="ffn",
    )(x, sh, sc, gt, w1, w2, gf)


def _rot_half_cols(w, axis=-1):
    lo, hi = jnp.split(w, 2, axis=axis)
    return jnp.concatenate([-hi, lo], axis=axis)


def _prep_weights(w_in, w_uq, g_q, g_kv, w_uk, w_uv, lb, g_hgrn, w_out, w_ff1, w_ff2, g_final):
    d = w_in.shape[0]
    o_kpe = Q_RANK + KV_RANK
    o_hg = o_kpe + QK_ROPE
    w_in_t = jnp.swapaxes(w_in, 0, 1)
    w_mla_t = lax.optimization_barrier(w_in_t[:o_hg])
    w_kpe_t = w_mla_t[o_kpe:]
    q_scale = ATTN_SCALE * math.log2(math.e)
    wq = w_uq.reshape(Q_RANK, N_HEADS, QK_NOPE + QK_ROPE) * q_scale
    wq_pe = wq[..., QK_NOPE:]
    wuq = jnp.concatenate([wq[..., :QK_NOPE], wq_pe, _rot_half_cols(wq_pe)], axis=-1)
    return {
        "wa": jnp.concatenate([w_mla_t, _rot_half_cols(w_kpe_t, axis=0)], axis=0).astype(BF16),
        "wh_src": (w_in_t, o_hg, w_in_t.shape[0] - o_hg),
        "wuq": wuq.reshape(Q_RANK, N_HEADS * HEAD_PACK).astype(BF16),
        "wukv": jnp.concatenate([w_uk.reshape(KV_RANK, -1), w_uv.reshape(KV_RANK, -1)], axis=-1).astype(BF16),
        "wuk_t": jnp.transpose(w_uk, (1, 0, 2)).astype(BF16),
        "wuv_t": jnp.transpose(w_uv, (1, 0, 2)).astype(BF16),
        "gq": g_q.reshape(1, -1), "gkv": g_kv.reshape(1, -1),
        "lb": lb.reshape(1, -1), "ghg": g_hgrn.reshape(1, -1),
        "wout_f32": w_out, "w1_f32": w_ff1, "w2_f32": w_ff2,
        "gfin": g_final.reshape(1, d),
    }


def _layer(x, mods, w, batch, seq, cache, past):
    sh1, sc1, gt1, sh2, sc2, gt2 = mods
    t = x.shape[0]
    tm = min(PROJ_ROWS, t)
    prompt = cache is None
    if prompt:
        q, ckv, kpe, kf, v, w["wh"] = _mla_in(x, sh1, sc1, w, tm, True, past, seq, w["wh_src"])
        o_mla = _attention(q, kf, v, batch, seq, ATTN_KV_ROWS, ATTN_HEADS)
    else:
        cache_ckv, cache_kpe, state0 = cache
        q, ckv, kpe = _mla_in(x, sh1, sc1, w, tm, False, past, seq)
        ql, qpe = _q_latent(q, w["wuk_t"], batch, seq)
        o_mla = _dec_attention(ql, qpe, cache_ckv, cache_kpe, ckv, kpe, w["wuv_t"], seq)
    if prompt:
        w["w1"], w["w2"], w["wout"], qh, kh, vh, gate, g = _hg_in(
            x, sh1, sc1, w, tm, (w["w1_f32"], w["w2_f32"], w["wout_f32"]))
        o_hg, s_fin = _hgrn(qh, kh, vh, gate, g, w["ghg"], None, batch, seq, CHUNK, HGRN_ROWS)
    else:
        qh, kh, vh, gate, g = _hg_in(x, sh1, sc1, w, tm)
        o_hg, s_fin = _hgrn(qh, kh, vh, gate, g, w["ghg"], state0, batch, seq, seq, seq,
                            DEC_SEQS if batch % DEC_SEQS == 0 else 1)
    x1 = _out_proj(x, gt1, o_mla, o_hg, w["wout"], tm)
    y = _ffn(x1, sh2, sc2, gt2, w["w1"], w["w2"], w["gfin"], min(FFN_ROWS, t), FFN_COLS)
    return y, ckv, kpe, s_fin


def kernel(x_prompt, x_sample, c_prompt, c_sample, cache_ckv, cache_kpe, state_hgrn, w_ada, b_ada, w_in, w_uq, g_q, g_kv, w_uk, w_uv, hg_lower_bounds, g_hgrn, w_out, w_ff1, w_ff2, g_final):
    depth = w_ada.shape[0]
    assert depth == 1, "single-layer step"
    bp, sp, d = x_prompt.shape
    bs, ss, _ = x_sample.shape
    past = cache_ckv.shape[2]

    lb_all = jnp.cumsum(jax.nn.softmax(hg_lower_bounds.astype(F32), axis=0), axis=0)
    w = _prep_weights(w_in[0], w_uq[0], g_q[0], g_kv[0], w_uk[0], w_uv[0], lb_all[0], g_hgrn[0],
                      w_out[0], w_ff1[0], w_ff2[0], g_final)

    c_all = jnp.concatenate([c_prompt, c_sample], axis=0)
    n_c = c_all.shape[0]
    c_pad = jnp.pad(c_all, ((0, -n_c % 8), (0, 0)))
    mod = _modulation(c_pad, w_ada[0], b_ada[0])[:n_c]
    mod_p = [m.reshape(bp, 1, d) for m in jnp.split(mod[:bp], 6, axis=-1)]
    mod_s = [jnp.repeat(m, ss, axis=0).reshape(1, bs * ss, d) for m in jnp.split(mod[bp:], 6, axis=-1)]

    yp, ckv_p, kpe_p, st_p = _layer(x_prompt.reshape(bp * sp, d), mod_p, w, bp, sp, None, 0)
    ys, ckv_s, kpe_s, st_s = _layer(x_sample.reshape(bs * ss, d), mod_s, w, bs, ss,
                                    (cache_ckv[0], jnp.swapaxes(cache_kpe[0], 1, 2), state_hgrn[0]), past)
    return (yp.reshape(bp, sp, d), ys.reshape(bs, ss, d),
            ckv_p.reshape(1, bp, sp, KV_RANK), kpe_p.reshape(1, bp, sp, QK_ROPE), st_p[None],
            ckv_s.reshape(1, bs, ss, KV_RANK), kpe_s.reshape(1, bs, ss, QK_ROPE), st_s[None])
```
